```python
import math
import jax, jax.numpy as jnp
from jax import lax
import numpy as np

D_MODEL = 2048
BATCH = 4
SEQ = 2048
DEPTH = 4

HEAD_DIM = 64
A_HEADS = 16
A_KV_HEADS = 2
A_REP = A_HEADS // A_KV_HEADS
A_WINDOW = 128
B_HEADS = 16
B_PATTERNS = ((128, 1), (512, 4), (2048, 16))
BLK = 128
NUM_BUCKETS = 32
REL_MAX_DISTANCE = 2048
N_ATTN_HEADS = A_HEADS + B_HEADS
A_WIDTH = A_HEADS * HEAD_DIM
A_KV_WIDTH = A_KV_HEADS * HEAD_DIM
B_WIDTH = B_HEADS * HEAD_DIM
EVEN_MIX = A_WIDTH + B_WIDTH
EVEN_IN = 2 * A_WIDTH + 2 * A_KV_WIDTH + 4 * B_WIDTH
C_WIDTH = 2 * D_MODEL
C_GROUPS = 16
C_GROUP_DIM = C_WIDTH // C_GROUPS
C_CHUNK = 128
C_IN = 3 * C_WIDTH
N_EVEN = (DEPTH + 1) // 2
N_ODD = DEPTH // 2
EPS = 1e-6
NEG = -1e30
SCALE = HEAD_DIM ** -0.5

kernel_name = 'hybrid_swa_dilated_sgu_trunk'


def _rms(x, g):
    xf = x.astype(jnp.float32)
    y = xf * lax.rsqrt(jnp.mean(xf * xf, axis=-1, keepdims=True) + EPS)
    return y * g.astype(jnp.float32)


def _t5_bucket(dist):
    n = np.maximum(dist, 0)
    max_exact = NUM_BUCKETS // 2
    large = max_exact + (np.log(np.maximum(n, 1) / max_exact)
                         / np.log(REL_MAX_DISTANCE / max_exact)
                         * (NUM_BUCKETS - max_exact)).astype(np.int32)
    large = np.minimum(large, NUM_BUCKETS - 1)
    return np.where(n < max_exact, n, large).astype(np.int32)


def _block_dist():
    a = np.arange(BLK)[:, None]
    b = np.arange(2 * BLK)[None, :]
    return BLK + a - b


def _rel_bias_block(table, dil):
    bucket = _t5_bucket(_block_dist() * dil)
    return table.astype(jnp.float32)[bucket].transpose(2, 0, 1)


def _band_mask(nb, max_dist):
    dist = _block_dist()
    key_idx = np.arange(nb)[:, None, None] * BLK - BLK + np.arange(2 * BLK)[None, None, :]
    return (dist >= 0)[None] & (dist <= max_dist)[None] & (key_idx >= 0)


def _blocks(x, nb):
    L = x.shape[-2]
    x = jnp.pad(x, [(0, 0)] * (x.ndim - 2) + [(0, nb * BLK - L), (0, 0)])
    return x.reshape(x.shape[:-2] + (nb, BLK, x.shape[-1]))


def _with_prev(xb):
    prev = jnp.pad(xb, [(0, 0)] * (xb.ndim - 3) + [(1, 0), (0, 0), (0, 0)])[..., :-1, :, :]
    return jnp.concatenate([prev, xb], axis=-2)


def _band_parts(q, k, v, bias, max_dist):
    L = q.shape[-2]
    nb = -(-L // BLK)
    qb = _blocks(q, nb)
    kb = _with_prev(_blocks(k, nb))
    vb = _with_prev(_blocks(v, nb))
    s = jnp.einsum('bgrnqd,bgnkd->bgrnqk', qb, kb) + bias[:, :, None]
    s = jnp.where(_band_mask(nb, max_dist), s, NEG)
    m = jnp.max(s, axis=-1)
    p = jnp.exp(s - m[..., None])
    l = jnp.sum(p, axis=-1)
    u = jnp.einsum('bgrnqk,bgnkd->bgrnqd', p, vb)
    m = m.reshape(m.shape[:-2] + (nb * BLK,))[..., :L]
    l = l.reshape(l.shape[:-2] + (nb * BLK,))[..., :L]
    u = u.reshape(u.shape[:-3] + (nb * BLK, HEAD_DIM))[..., :L, :]
    return m, l, u


def _mixer_a(q, k, v, gq, gk, sinks, table):
    Bn, S, _ = q.shape
    q = _rms(q.reshape(Bn, S, A_HEADS, HEAD_DIM), gq) * SCALE
    k = _rms(k.reshape(Bn, S, A_KV_HEADS, HEAD_DIM), gk)
    v = v.reshape(Bn, S, A_KV_HEADS, HEAD_DIM).astype(jnp.float32)
    q = q.reshape(Bn, S, A_KV_HEADS, A_REP, HEAD_DIM).transpose(0, 2, 3, 1, 4)
    k = k.transpose(0, 2, 1, 3)
    v = v.transpose(0, 2, 1, 3)
    bias = _rel_bias_block(table, 1).reshape(A_KV_HEADS, A_REP, BLK, 2 * BLK)
    m, l, u = _band_parts(q, k, v, bias, A_WINDOW - 1)
    snk = sinks.astype(jnp.float32).reshape(1, A_KV_HEADS, A_REP, 1)
    mx = jnp.maximum(m, snk)
    w = jnp.exp(m - mx)
    o = u * (w / (l * w + jnp.exp(snk - mx)))[..., None]
    return o.transpose(0, 3, 1, 2, 4).reshape(Bn, S, A_WIDTH)


def _mixer_b(q, k, v, gq, gk, table):
    Bn, S, _ = q.shape
    q = (_rms(q.reshape(Bn, S, B_HEADS, HEAD_DIM), gq) * SCALE).transpose(0, 2, 1, 3)
    k = _rms(k.reshape(Bn, S, B_HEADS, HEAD_DIM), gk).transpose(0, 2, 1, 3)
    v = v.reshape(Bn, S, B_HEADS, HEAD_DIM).astype(jnp.float32).transpose(0, 2, 1, 3)
    ms, ls, us = [], [], []
    for window, dil in B_PATTERNS:
        L = S // dil
        def strided(t):
            return t.reshape(Bn, B_HEADS, L, dil, HEAD_DIM).transpose(0, 1, 3, 2, 4).reshape(Bn, B_HEADS * dil, L, HEAD_DIM)
        bias = jnp.repeat(_rel_bias_block(table, dil), dil, axis=0)[:, None]
        m, l, u = _band_parts(strided(q)[:, :, None], strided(k), strided(v), bias, window // dil)
        ms.append(m.reshape(Bn, B_HEADS, dil, L).transpose(0, 1, 3, 2).reshape(Bn, B_HEADS, S))
        ls.append(l.reshape(Bn, B_HEADS, dil, L).transpose(0, 1, 3, 2).reshape(Bn, B_HEADS, S))
        us.append(u.reshape(Bn, B_HEADS, dil, L, HEAD_DIM).transpose(0, 1, 3, 2, 4).reshape(Bn, B_HEADS, S, HEAD_DIM))
    m_all = jnp.stack(ms)
    mx = jnp.max(m_all, axis=0)
    w = jnp.exp(m_all - mx)
    num = jnp.sum(w[..., None] * jnp.stack(us), axis=0)
    den = jnp.sum(w * jnp.stack(ls), axis=0)
    o = num / den[..., None]
    return o.transpose(0, 2, 1, 3).reshape(Bn, S, B_WIDTH)


def _even_layer(x, ln_g, w_in, qk_g, sinks, w_out, rel_bias):
    h = _rms(x, ln_g).astype(x.dtype)
    z = h @ w_in
    cuts = np.cumsum([A_WIDTH, A_KV_WIDTH, A_KV_WIDTH, A_WIDTH, B_WIDTH, B_WIDTH, B_WIDTH]).tolist()
    qa, ka, va, ga, qb, kb, vb, gb = jnp.split(z, cuts, axis=-1)
    ya = _mixer_a(qa, ka, va, qk_g[0], qk_g[1], sinks, rel_bias[:, :A_HEADS])
    yb = _mixer_b(qb, kb, vb, qk_g[2], qk_g[3], rel_bias[:, A_HEADS:])
    y = jnp.concatenate([ya * jax.nn.silu(ga.astype(jnp.float32)),
                         yb * jax.nn.silu(gb.astype(jnp.float32))], axis=-1)
    return x + y.astype(x.dtype) @ w_out


def _odd_layer(x, ln_g, w_in, v_g, w_s, b_s, w_out):
    Bn, S, _ = x.shape
    h = _rms(x, ln_g).astype(x.dtype)
    z = h @ w_in
    uv = jax.nn.gelu(z[..., :2 * C_WIDTH].astype(jnp.float32), approximate=False)
    gate = jax.nn.silu(z[..., 2 * C_WIDTH:].astype(jnp.float32))
    u, v = uv[..., :C_WIDTH], uv[..., C_WIDTH:]
    v = _rms(v, v_g).reshape(Bn, S // C_CHUNK, C_CHUNK, C_GROUPS, C_GROUP_DIM)
    ws = w_s.astype(jnp.float32) * np.tril(np.ones((C_CHUNK, C_CHUNK), np.float32))
    s = jnp.einsum('gts,bnsgc->bntgc', ws, v) + b_s.astype(jnp.float32).T[:, :, None]
    y = u * s.reshape(Bn, S, C_WIDTH) * gate
    return x + y.astype(x.dtype) @ w_out


def setup_inputs(seed: int = 0) -> dict:
    key = jax.random.key(seed)
    ks = jax.random.split(key, 14)
    f32 = jnp.float32
    nrm = lambda k, shape, sc: jax.random.normal(k, shape, f32) * sc
    return {
        'x': nrm(ks[0], (BATCH, SEQ, D_MODEL), 1.0),
        'ev_ln_g': 1.0 + nrm(ks[1], (N_EVEN, D_MODEL), 0.01),
        'ev_w_in': nrm(ks[2], (N_EVEN, D_MODEL, EVEN_IN), D_MODEL ** -0.5),
        'ev_qk_g': 1.0 + nrm(ks[3], (N_EVEN, 4, HEAD_DIM), 0.01),
        'ev_sinks': nrm(ks[4], (N_EVEN, A_HEADS), 1.0),
        'ev_w_out': nrm(ks[5], (N_EVEN, EVEN_MIX, D_MODEL), EVEN_MIX ** -0.5),
        'od_ln_g': 1.0 + nrm(ks[6], (N_ODD, D_MODEL), 0.01),
        'od_w_in': nrm(ks[7], (N_ODD, D_MODEL, C_IN), D_MODEL ** -0.5),
        'od_v_g': 1.0 + nrm(ks[8], (N_ODD, C_WIDTH), 0.01),
        'od_w_s': nrm(ks[9], (N_ODD, C_GROUPS, C_CHUNK, C_CHUNK), C_CHUNK ** -0.5),
        'od_b_s': 1.0 + nrm(ks[10], (N_ODD, C_GROUPS, C_CHUNK), 0.01),
        'od_w_out': nrm(ks[11], (N_ODD, C_WIDTH, D_MODEL), C_WIDTH ** -0.5),
        'rel_bias': nrm(ks[12], (NUM_BUCKETS, N_ATTN_HEADS), 0.2),
    }


def reference(x, ev_ln_g, ev_w_in, ev_qk_g, ev_sinks, ev_w_out, od_ln_g, od_w_in,
              od_v_g, od_w_s, od_b_s, od_w_out, rel_bias):
    for i in range(DEPTH):
        j = i // 2
        if i % 2 == 0:
            x = _even_layer(x, ev_ln_g[j], ev_w_in[j], ev_qk_g[j], ev_sinks[j], ev_w_out[j], rel_bias)
        else:
            x = _odd_layer(x, od_ln_g[j], od_w_in[j], od_v_g[j], od_w_s[j], od_b_s[j], od_w_out[j])
    return x
```

```python
import functools
import math

import numpy as np
import jax
import jax.numpy as jnp
from jax import lax
from jax.experimental import pallas as pl
from jax.experimental.pallas import tpu as pltpu

F32 = jnp.float32
BF16 = jnp.bfloat16

D_MODEL = 2048
HEAD_DIM = 64
A_HEADS = 16
A_KV_HEADS = 2
B_HEADS = 16
BLK = 128
NUM_BUCKETS = 32
REL_MAX_DISTANCE = 2048
A_WIDTH = A_HEADS * HEAD_DIM
B_WIDTH = B_HEADS * HEAD_DIM
EVEN_IN = 6400
C_WIDTH = 2 * D_MODEL
C_GROUPS = 16
C_GROUP_DIM = C_WIDTH // C_GROUPS
C_CHUNK = 128
EPS = 1e-6
NEG = -1e30
SCALE = HEAD_DIM ** -0.5
PATTERNS = ((1, 127), (1, 128), (4, 128), (16, 128))

LANES = 128
VMEM_LIMIT = 56 * 1024 * 1024

_QA, _KA, _VA, _GA = 0, 8, 9, 10
_QB, _KB, _VB, _GB = 18, 26, 34, 42


def _params(sem):
    return pltpu.CompilerParams(dimension_semantics=sem, vmem_limit_bytes=VMEM_LIMIT)


def _bucket_tables():
    a = np.arange(BLK)[:, None]
    b = np.arange(2 * BLK)[None, :]
    dist = BLK + a - b
    max_exact = NUM_BUCKETS // 2
    out = []
    for dil, max_dist in PATTERNS:
        n = np.maximum(dist * dil, 0)
        large = max_exact + (np.log(np.maximum(n, 1) / max_exact)
                             / np.log(REL_MAX_DISTANCE / max_exact)
                             * (NUM_BUCKETS - max_exact)).astype(np.int32)
        large = np.minimum(large, NUM_BUCKETS - 1)
        bucket = np.where(n < max_exact, n, large).astype(np.int32)
        valid = (dist >= 0) & (dist <= max_dist)
        out.append(np.where(valid, bucket, -1).astype(np.int32))
    return np.stack(out)


def _bias_kernel(tbl_ref, bucket_ref, out_ref):
    t = pl.program_id(0)
    h = pl.program_id(1)
    col = h + jnp.where(t > 0, A_HEADS, 0)
    bk = bucket_ref[...]
    acc = jnp.full(bk.shape, NEG, F32)
    for b in range(NUM_BUCKETS):
        acc = jnp.where(bk == b, tbl_ref[b, col], acc)
    out_ref[...] = acc


def _bias_tables(rel_bias):
    buckets = jnp.asarray(_bucket_tables())
    return pl.pallas_call(
        _bias_kernel,
        grid=(4, 16),
        in_specs=[pl.BlockSpec(memory_space=pltpu.SMEM),
                  pl.BlockSpec((None, BLK, 2 * BLK), lambda t, h: (t, 0, 0))],
        out_specs=pl.BlockSpec((None, None, BLK, 2 * BLK), lambda t, h: (t, h, 0, 0)),
        out_shape=jax.ShapeDtypeStruct((4, 16, BLK, 2 * BLK), F32),
        compiler_params=_params(("arbitrary", "arbitrary")),
        name="bias_tables",
    )(rel_bias.astype(F32), buckets)


def _norm_rows(x_ref, g_ref, h_scr, tm):
    def body(c, carry):
        r = pl.ds(c * 128, 128)
        x = x_ref[r, :]
        ms = jnp.mean(x * x, axis=-1, keepdims=True)
        h_scr[r, :] = (x * lax.rsqrt(ms + EPS) * g_ref[...]).astype(BF16)
        return carry
    lax.fori_loop(0, tm // 128, body, 0)


def _even_in_kernel(x_ref, g_ref, w_ref, o_ref, h_scr, *, tm):
    @pl.when(pl.program_id(1) == 0)
    def _():
        _norm_rows(x_ref, g_ref, h_scr, tm)
    o_ref[...] = jnp.dot(h_scr[...], w_ref[...], preferred_element_type=F32).astype(o_ref.dtype)


def _even_in(x2, g, w, *, tm=1024, tn=1280):
    m, d = x2.shape
    n = w.shape[1]
    return pl.pallas_call(
        functools.partial(_even_in_kernel, tm=tm),
        grid=(m // tm, n // tn),
        in_specs=[pl.BlockSpec((tm, d), lambda i, j: (i, 0)),
                  pl.BlockSpec((1, d), lambda i, j: (0, 0)),
                  pl.BlockSpec((d, tn), lambda i, j: (0, j))],
        out_specs=pl.BlockSpec((tm, tn), lambda i, j: (i, j)),
        out_shape=jax.ShapeDtypeStruct((m, n), BF16),
        scratch_shapes=[pltpu.VMEM((tm, d), BF16)],
        compiler_params=_params(("arbitrary", "arbitrary")),
        name="even_in",
    )(x2, g, w)


def _gelu(x):
    return 0.5 * x * (1.0 + lax.erf(x * (1.0 / math.sqrt(2.0))))


def _silu(x):
    return x * jax.nn.sigmoid(x)


def _odd_in_kernel(x_ref, g_ref, wu_ref, wv_ref, wg_ref, ug_ref, v_ref, ssq_ref, h_scr, *, tm):
    j = pl.program_id(1)

    @pl.when(j == 0)
    def _():
        _norm_rows(x_ref, g_ref, h_scr, tm)
        ssq_ref[...] = jnp.zeros_like(ssq_ref)

    h = h_scr[...]
    u = jnp.dot(h, wu_ref[...], preferred_element_type=F32)
    gt = jnp.dot(h, wg_ref[...], preferred_element_type=F32)
    ug_ref[...] = (_gelu(u) * _silu(gt)).astype(ug_ref.dtype)
    v = _gelu(jnp.dot(h, wv_ref[...], preferred_element_type=F32))
    v_ref[...] = v.astype(v_ref.dtype)
    ssq_ref[...] += jnp.sum(v * v, axis=-1, keepdims=True)


def _odd_in(x2, g, w, *, tm=1024, tn=512):
    m, d = x2.shape
    nb = C_WIDTH // tn
    return pl.pallas_call(
        functools.partial(_odd_in_kernel, tm=tm),
        grid=(m // tm, nb),
        in_specs=[pl.BlockSpec((tm, d), lambda i, j: (i, 0)),
                  pl.BlockSpec((1, d), lambda i, j: (0, 0)),
                  pl.BlockSpec((d, tn), lambda i, j: (0, j)),
                  pl.BlockSpec((d, tn), lambda i, j: (0, j + nb)),
                  pl.BlockSpec((d, tn), lambda i, j: (0, j + 2 * nb))],
        out_specs=[pl.BlockSpec((tm, tn), lambda i, j: (i, j)),
                   pl.BlockSpec((tm, tn), lambda i, j: (i, j)),
                   pl.BlockSpec((tm, LANES), lambda i, j: (i, 0))],
        out_shape=[jax.ShapeDtypeStruct((m, C_WIDTH), BF16),
                   jax.ShapeDtypeStruct((m, C_WIDTH), BF16),
                   jax.ShapeDtypeStruct((m, LANES), F32)],
        scratch_shapes=[pltpu.VMEM((tm, d), BF16)],
        compiler_params=_params(("arbitrary", "arbitrary")),
        name="odd_in",
    )(x2, g, w, w, w)


def _head_sumsq(x):
    r = lax.broadcasted_iota(jnp.int32, (LANES, LANES), 0) // HEAD_DIM
    c = lax.broadcasted_iota(jnp.int32, (LANES, LANES), 1) // HEAD_DIM
    ones_bd = (r == c).astype(BF16)
    x2 = x * x
    hi = x2.astype(BF16)
    r1 = x2 - hi.astype(F32)
    mid = r1.astype(BF16)
    lo = (r1 - mid.astype(F32)).astype(BF16)
    dot = lambda a: jnp.dot(a, ones_bd, preferred_element_type=F32)
    return dot(hi) + dot(mid) + dot(lo)


def _head_rms(x, gain):
    return x * lax.rsqrt(_head_sumsq(x) * (1.0 / HEAD_DIM) + EPS) * gain


def _band_block(q, kk, vv, bias0, bias1, first):
    lo = lax.broadcasted_iota(jnp.int32, (BLK, LANES), 1) < HEAD_DIM
    prev_cols = lax.broadcasted_iota(jnp.int32, (BLK, 2 * BLK), 1) < BLK
    kill = jnp.logical_and(first, prev_cols)
    ones = jnp.ones((2 * BLK, LANES), BF16)
    res = []
    for sel, bias in ((lo, bias0), (jnp.logical_not(lo), bias1)):
        qm = jnp.where(sel, q, jnp.zeros_like(q))
        s = lax.dot_general(qm, kk, (((1,), (1,)), ((), ())), preferred_element_type=F32)
        s = s + jnp.where(kill, NEG, bias)
        m = jnp.max(s, axis=-1, keepdims=True)
        p = jnp.exp(s - m).astype(BF16)
        u = jnp.dot(p, vv, preferred_element_type=F32)
        l = jnp.dot(p, ones, preferred_element_type=F32)
        res.append((jnp.broadcast_to(m, (BLK, LANES)), l, u))
    return tuple(jnp.where(lo, a, b) for a, b in zip(res[0], res[1]))


def _attn_a_kernel(q_ref, k_ref, v_ref, g_ref, gq_ref, gk_ref, snk_ref, bias_ref, o_ref,
                   qn_scr, kk_scr, vv_scr, *, seq):
    grp = pl.program_id(1) // (A_HEADS // A_KV_HEADS // 2)
    keep = (lax.broadcasted_iota(jnp.int32, (BLK, LANES), 1) // HEAD_DIM) == grp
    zeros = jnp.zeros((BLK, LANES), BF16)
    kk_scr[pl.ds(0, BLK), :] = zeros
    vv_scr[pl.ds(0, BLK), :] = zeros

    def prep(c, carry):
        r = pl.ds(c * BLK, BLK)
        ro = pl.ds(BLK + c * BLK, BLK)
        qn_scr[r, :] = (_head_rms(q_ref[r, :].astype(F32), gq_ref[...]) * SCALE).astype(BF16)
        kn = _head_rms(k_ref[r, :].astype(F32), gk_ref[...])
        kk_scr[ro, :] = jnp.where(keep, kn, pltpu.roll(kn, HEAD_DIM, 1)).astype(BF16)
        v = v_ref[r, :].astype(F32)
        vv_scr[ro, :] = jnp.where(keep, v, pltpu.roll(v, HEAD_DIM, 1)).astype(BF16)
        return carry
    lax.fori_loop(0, seq // BLK, prep, 0)

    snk = snk_ref[...]

    def block(n, carry):
        r = pl.ds(n * BLK, BLK)
        rk = pl.ds(n * BLK, 2 * BLK)
        m, l, u = _band_block(qn_scr[r, :], kk_scr[rk, :], vv_scr[rk, :],
                              bias_ref[0], bias_ref[1], n == 0)
        mx = jnp.maximum(m, snk)
        w = jnp.exp(m - mx)
        o = u * (w / (l * w + jnp.exp(snk - mx)))
        o_ref[r, :] = (o * _silu(g_ref[r, :].astype(F32))).astype(o_ref.dtype)
        return carry
    lax.fori_loop(0, seq // BLK, block, 0)


def _attn_a(z3, gq2, gk2, snk3, bias_a):
    bsz, seq, _ = z3.shape
    col = lambda off: pl.BlockSpec((None, seq, LANES), lambda b, p: (b, 0, off + p))
    fixed = lambda off: pl.BlockSpec((None, seq, LANES), lambda b, p: (b, 0, off))
    return pl.pallas_call(
        functools.partial(_attn_a_kernel, seq=seq),
        grid=(bsz, A_HEADS // 2),
        in_specs=[col(_QA), fixed(_KA), fixed(_VA), col(_GA),
                  pl.BlockSpec((1, LANES), lambda b, p: (0, 0)),
                  pl.BlockSpec((1, LANES), lambda b, p: (0, 0)),
                  pl.BlockSpec((None, 1, LANES), lambda b, p: (p, 0, 0)),
                  pl.BlockSpec((2, BLK, 2 * BLK), lambda b, p: (p, 0, 0))],
        out_specs=pl.BlockSpec((None, seq, LANES), lambda b, p: (b, 0, p)),
        out_shape=jax.ShapeDtypeStruct((bsz, seq, A_WIDTH), BF16),
        scratch_shapes=[pltpu.VMEM((seq, LANES), BF16),
                        pltpu.VMEM((seq + BLK, LANES), BF16),
                        pltpu.VMEM((seq + BLK, LANES), BF16)],
        compiler_params=_params(("arbitrary", "arbitrary")),
        name="attn_a",
    )(z3, z3, z3, z3, gq2, gk2, snk3, bias_a)


_PAD_B = 4 * BLK


def _attn_b_kernel(q_ref, k_ref, v_ref, g_ref, gq_ref, gk_ref, bias_ref, o_ref,
                   qn_scr, kn_scr, vf_scr, m_scr, l_scr, u_scr, *, seq):
    zeros = jnp.zeros((_PAD_B, LANES), F32)
    kn_scr[pl.ds(0, _PAD_B), :] = zeros
    vf_scr[pl.ds(0, _PAD_B), :] = zeros

    def prep(c, carry):
        r = pl.ds(c * BLK, BLK)
        ro = pl.ds(_PAD_B + c * BLK, BLK)
        qn_scr[r, :] = _head_rms(q_ref[r, :].astype(F32), gq_ref[...]) * SCALE
        kn_scr[ro, :] = _head_rms(k_ref[r, :].astype(F32), gk_ref[...])
        vf_scr[ro, :] = v_ref[r, :].astype(F32)
        return carry
    lax.fori_loop(0, seq // BLK, prep, 0)

    for pi, (dil, _) in enumerate(PATTERNS[1:]):
        blocks_per_class = seq // (dil * BLK)
        span = dil * BLK

        def block(it, carry, pi=pi, dil=dil, blocks_per_class=blocks_per_class, span=span):
            n = lax.shift_right_logical(it, int(math.log2(dil)))
            c = lax.bitwise_and(it, dil - 1)
            start = n * span + c
            if dil == 1:
                rq = pl.ds(start, BLK)
                rkp = pl.ds(_PAD_B + start - span, BLK)
                rkc = pl.ds(_PAD_B + start, BLK)
            else:
                rq = pl.ds(start, BLK, stride=dil)
                rkc = pl.ds(_PAD_B + start, BLK, stride=dil)
                rkp = rkc if blocks_per_class == 1 else pl.ds(_PAD_B + start - span, BLK, stride=dil)
            q = qn_scr[rq, :].astype(BF16)
            kk = jnp.concatenate([kn_scr[rkp, :], kn_scr[rkc, :]], axis=0).astype(BF16)
            vv = jnp.concatenate([vf_scr[rkp, :], vf_scr[rkc, :]], axis=0).astype(BF16)
            m, l, u = _band_block(q, kk, vv, bias_ref[pi, 0], bias_ref[pi, 1], n == 0)
            m_scr[pi, rq, :] = m
            l_scr[pi, rq, :] = l
            u_scr[pi, rq, :] = u
            return carry
        lax.fori_loop(0, blocks_per_class * dil, block, 0)

    def combine(c, carry):
        r = pl.ds(c * BLK, BLK)
        ms = [m_scr[pi, r, :] for pi in range(3)]
        mx = jnp.maximum(jnp.maximum(ms[0], ms[1]), ms[2])
        num = jnp.zeros((BLK, LANES), F32)
        den = jnp.zeros((BLK, LANES), F32)
        for pi in range(3):
            w = jnp.exp(ms[pi] - mx)
            num = num + w * u_scr[pi, r, :]
            den = den + w * l_scr[pi, r, :]
        o_ref[r, :] = ((num / den) * _silu(g_ref[r, :].astype(F32))).astype(o_ref.dtype)
        return carry
    lax.fori_loop(0, seq // BLK, combine, 0)


def _attn_b(z3, gq2, gk2, bias_b):
    bsz, seq, _ = z3.shape
    col = lambda off: pl.BlockSpec((None, seq, LANES), lambda b, p: (b, 0, off + p))
    return pl.pallas_call(
        functools.partial(_attn_b_kernel, seq=seq),
        grid=(bsz, B_HEADS // 2),
        in_specs=[col(_QB), col(_KB), col(_VB), col(_GB),
                  pl.BlockSpec((1, LANES), lambda b, p: (0, 0)),
                  pl.BlockSpec((1, LANES), lambda b, p: (0, 0)),
                  pl.BlockSpec((3, 2, BLK, 2 * BLK), lambda b, p: (0, p, 0, 0))],
        out_specs=pl.BlockSpec((None, seq, LANES), lambda b, p: (b, 0, p)),
        out_shape=jax.ShapeDtypeStruct((bsz, seq, B_WIDTH), BF16),
        scratch_shapes=[pltpu.VMEM((seq, LANES), F32),
                        pltpu.VMEM((seq + _PAD_B, LANES), F32),
                        pltpu.VMEM((seq + _PAD_B, LANES), F32),
                        pltpu.VMEM((3, seq, LANES), F32),
                        pltpu.VMEM((3, seq, LANES), F32),
                        pltpu.VMEM((3, seq, LANES), F32)],
        compiler_params=_params(("arbitrary", "arbitrary")),
        name="attn_b",
    )(z3, z3, z3, z3, gq2, gk2, bias_b)


def _even_out_kernel(ya_ref, yb_ref, wa_ref, wb_ref, x_ref, o_ref):
    acc = jnp.dot(ya_ref[...], wa_ref[...], preferred_element_type=F32)
    acc = acc + jnp.dot(yb_ref[...], wb_ref[...], preferred_element_type=F32)
    o_ref[...] = x_ref[...] + acc


def _even_out(ya, yb, w, x2, *, tm=1024, tn=1024):
    m, d = x2.shape
    ka, kb = ya.shape[1], yb.shape[1]
    assert ka == kb
    return pl.pallas_call(
        _even_out_kernel,
        grid=(m // tm, d // tn),
        in_specs=[pl.BlockSpec((tm, ka), lambda i, j: (i, 0)),
                  pl.BlockSpec((tm, kb), lambda i, j: (i, 0)),
                  pl.BlockSpec((ka, tn), lambda i, j: (0, j)),
                  pl.BlockSpec((kb, tn), lambda i, j: (1, j)),
                  pl.BlockSpec((tm, tn), lambda i, j: (i, j))],
        out_specs=pl.BlockSpec((tm, tn), lambda i, j: (i, j)),
        out_shape=jax.ShapeDtypeStruct((m, d), F32),
        compiler_params=_params(("arbitrary", "arbitrary")),
        name="even_out",
    )(ya, yb, w, w, x2)


def _odd_out_kernel(ug_ref, v_ref, ssq_ref, vg_ref, ws_ref, bs_ref, w_ref, x_ref, o_ref, y_scr, *, tm):
    @pl.when(pl.program_id(1) == 0)
    def _():
        tril = (lax.broadcasted_iota(jnp.int32, (C_CHUNK, C_CHUNK), 0)
                >= lax.broadcasted_iota(jnp.int32, (C_CHUNK, C_CHUNK), 1))

        def chunk(c, carry):
            r = pl.ds(c * C_CHUNK, C_CHUNK)
            inv = lax.rsqrt(ssq_ref[r, :][:, :1] * (1.0 / C_WIDTH) + EPS)
            for g in range(C_GROUPS):
                cols = pl.ds(g * C_GROUP_DIM, C_GROUP_DIM)
                vn = (v_ref[r, cols].astype(F32) * inv * vg_ref[:, cols]).astype(BF16)
                wt = jnp.where(tril, ws_ref[g], 0.0).astype(BF16)
                s = jnp.dot(wt, vn, preferred_element_type=F32) + bs_ref[:, g:g + 1]
                y_scr[r, cols] = (ug_ref[r, cols].astype(F32) * s).astype(BF16)
            return carry
        lax.fori_loop(0, tm // C_CHUNK, chunk, 0)

    o_ref[...] = x_ref[...] + jnp.dot(y_scr[...], w_ref[...], preferred_element_type=F32)


def _odd_out(ug, v, ssq, vg, ws, bs_t, w, x2, *, tm=512, tn=512):
    m, d = x2.shape
    return pl.pallas_call(
        functools.partial(_odd_out_kernel, tm=tm),
        grid=(m // tm, d // tn),
        in_specs=[pl.BlockSpec((tm, C_WIDTH), lambda i, j: (i, 0)),
                  pl.BlockSpec((tm, C_WIDTH), lambda i, j: (i, 0)),
                  pl.BlockSpec((tm, LANES), lambda i, j: (i, 0)),
                  pl.BlockSpec((1, C_WIDTH), lambda i, j: (0, 0)),
                  pl.BlockSpec((C_GROUPS, C_CHUNK, C_CHUNK), lambda i, j: (0, 0, 0)),
                  pl.BlockSpec((C_CHUNK, C_GROUPS), lambda i, j: (0, 0)),
                  pl.BlockSpec((C_WIDTH, tn), lambda i, j: (0, j)),
                  pl.BlockSpec((tm, tn), lambda i, j: (i, j))],
        out_specs=pl.BlockSpec((tm, tn), lambda i, j: (i, j)),
        out_shape=jax.ShapeDtypeStruct((m, d), F32),
        scratch_shapes=[pltpu.VMEM((tm, C_WIDTH), BF16)],
        compiler_params=_params(("arbitrary", "arbitrary")),
        name="odd_out",
    )(ug, v, ssq, vg, ws, bs_t, w, x2)


def kernel(x, ev_ln_g, ev_w_in, ev_qk_g, ev_sinks, ev_w_out, od_ln_g, od_w_in, od_v_g, od_w_s,
           od_b_s, od_w_out, rel_bias):
    bsz, seq, d = x.shape
    depth = ev_ln_g.shape[0] + od_ln_g.shape[0]
    tabs = _bias_tables(rel_bias)
    x2 = x.reshape(bsz * seq, d)
    for i in range(depth):
        j = i // 2
        if i % 2 == 0:
            z = _even_in(x2, ev_ln_g[j].reshape(1, d), ev_w_in[j].astype(BF16))
            z3 = z.reshape(bsz, seq, EVEN_IN)
            gains = jnp.tile(ev_qk_g[j].astype(F32), (1, 2))
            snk3 = ev_sinks[j].astype(F32).reshape(A_HEADS // 2, 2, 1)
            snk3 = jnp.broadcast_to(snk3, (A_HEADS // 2, 2, HEAD_DIM)).reshape(A_HEADS // 2, 1, LANES)
            ya = _attn_a(z3, gains[0:1], gains[1:2], snk3, tabs[0])
            yb = _attn_b(z3, gains[2:3], gains[3:4], tabs[1:4])
            x2 = _even_out(ya.reshape(bsz * seq, A_WIDTH), yb.reshape(bsz * seq, B_WIDTH),
                           ev_w_out[j].astype(BF16), x2)
        else:
            ug, v, ssq = _odd_in(x2, od_ln_g[j].reshape(1, d), od_w_in[j].astype(BF16))
            x2 = _odd_out(ug, v, ssq, od_v_g[j].astype(F32).reshape(1, C_WIDTH),
                          od_w_s[j].astype(F32), od_b_s[j].astype(F32).T,
                          od_w_out[j].astype(BF16), x2)
    return x2.reshape(bsz, seq, d)
```

```python
import functools
import math

import numpy as np
import jax
import jax.numpy as jnp
from jax import lax
from jax.experimental import pallas as pl
from jax.experimental.pallas import tpu as pltpu

F32 = jnp.float32
BF16 = jnp.bfloat16

D_MODEL = 2048
HEAD_DIM = 64
A_HEADS = 16
A_KV_HEADS = 2
B_HEADS = 16
BLK = 128
NUM_BUCKETS = 32
REL_MAX_DISTANCE = 2048
A_WIDTH = A_HEADS * HEAD_DIM
B_WIDTH = B_HEADS * HEAD_DIM
EVEN_IN = 6400
C_WIDTH = 2 * D_MODEL
C_GROUPS = 16
C_GROUP_DIM = C_WIDTH // C_GROUPS
C_CHUNK = 128
EPS = 1e-6
NEG = -1e30
SCALE = HEAD_DIM ** -0.5
PATTERNS = ((1, 127), (1, 128), (4, 128), (16, 128))

LANES = 128
VMEM_LIMIT = 56 * 1024 * 1024

_QA, _KA, _VA, _GA = 0, 8, 9, 10
_QB, _KB, _VB, _GB = 18, 26, 34, 42


def _params(sem):
    return pltpu.CompilerParams(dimension_semantics=sem, vmem_limit_bytes=VMEM_LIMIT)


def _bucket_tables():
    a = np.arange(BLK)[:, None]
    b = np.arange(2 * BLK)[None, :]
    dist = BLK + a - b
    max_exact = NUM_BUCKETS // 2
    out = []
    for dil, max_dist in PATTERNS:
        n = np.maximum(dist * dil, 0)
        large = max_exact + (np.log(np.maximum(n, 1) / max_exact)
                             / np.log(REL_MAX_DISTANCE / max_exact)
                             * (NUM_BUCKETS - max_exact)).astype(np.int32)
        large = np.minimum(large, NUM_BUCKETS - 1)
        bucket = np.where(n < max_exact, n, large).astype(np.int32)
        valid = (dist >= 0) & (dist <= max_dist)
        out.append(np.where(valid, bucket, -1).astype(np.int32))
    return np.stack(out)


def _bias_kernel(tbl_ref, bucket_ref, out_ref):
    t = pl.program_id(0)
    h = pl.program_id(1)
    col = h + jnp.where(t > 0, A_HEADS, 0)
    bk = bucket_ref[...]
    acc = jnp.full(bk.shape, NEG, F32)
    for b in range(NUM_BUCKETS):
        acc = jnp.where(bk == b, tbl_ref[b, col], acc)
    out_ref[0] = acc
    prev_cols = lax.broadcasted_iota(jnp.int32, bk.shape, 1) < BLK
    out_ref[1] = jnp.where(prev_cols, NEG, acc)


def _bias_tables(rel_bias):
    buckets = jnp.asarray(_bucket_tables())
    return pl.pallas_call(
        _bias_kernel,
        grid=(4, 16),
        in_specs=[pl.BlockSpec(memory_space=pltpu.SMEM),
                  pl.BlockSpec((None, BLK, 2 * BLK), lambda t, h: (t, 0, 0))],
        out_specs=pl.BlockSpec((None, None, 2, BLK, 2 * BLK), lambda t, h: (t, h, 0, 0, 0)),
        out_shape=jax.ShapeDtypeStruct((4, 16, 2, BLK, 2 * BLK), F32),
        compiler_params=_params(("arbitrary", "arbitrary")),
        name="bias_tables",
    )(rel_bias.astype(F32), buckets)


def _norm_rows(x_ref, g_ref, h_scr, tm):
    def body(c, carry):
        r = pl.ds(c * 128, 128)
        x = x_ref[r, :]
        ms = jnp.mean(x * x, axis=-1, keepdims=True)
        h_scr[r, :] = (x * lax.rsqrt(ms + EPS) * g_ref[...]).astype(BF16)
        return carry
    lax.fori_loop(0, tm // 128, body, 0)


def _even_in_kernel(x_ref, g_ref, w_ref, o_ref, h_scr, *, tm):
    @pl.when(pl.program_id(1) == 0)
    def _():
        _norm_rows(x_ref, g_ref, h_scr, tm)
    o_ref[...] = jnp.dot(h_scr[...], w_ref[...], preferred_element_type=F32).astype(o_ref.dtype)


def _even_in(x2, g, w, *, tm=1024, tn=1280):
    m, d = x2.shape
    n = w.shape[1]
    return pl.pallas_call(
        functools.partial(_even_in_kernel, tm=tm),
        grid=(m // tm, n // tn),
        in_specs=[pl.BlockSpec((tm, d), lambda i, j: (i, 0)),
                  pl.BlockSpec((1, d), lambda i, j: (0, 0)),
                  pl.BlockSpec((d, tn), lambda i, j: (0, j))],
        out_specs=pl.BlockSpec((tm, tn), lambda i, j: (i, j)),
        out_shape=jax.ShapeDtypeStruct((m, n), BF16),
        scratch_shapes=[pltpu.VMEM((tm, d), BF16)],
        compiler_params=_params(("arbitrary", "arbitrary")),
        name="even_in",
    )(x2, g, w)


def _gelu(x):
    return 0.5 * x * (1.0 + lax.erf(x * (1.0 / math.sqrt(2.0))))


def _silu(x):
    return x * jax.nn.sigmoid(x)


def _odd_in_kernel(x_ref, g_ref, wu_ref, wv_ref, wg_ref, ug_ref, v_ref, ssq_ref, h_scr, *, tm):
    j = pl.program_id(1)

    @pl.when(j == 0)
    def _():
        _norm_rows(x_ref, g_ref, h_scr, tm)
        ssq_ref[...] = jnp.zeros_like(ssq_ref)

    h = h_scr[...]
    u = jnp.dot(h, wu_ref[...], preferred_element_type=F32)
    gt = jnp.dot(h, wg_ref[...], preferred_element_type=F32)
    ug_ref[...] = (_gelu(u) * _silu(gt)).astype(ug_ref.dtype)
    v = _gelu(jnp.dot(h, wv_ref[...], preferred_element_type=F32))
    v_ref[...] = v.astype(v_ref.dtype)
    ssq_ref[...] += jnp.sum(v * v, axis=-1, keepdims=True)


def _odd_in(x2, g, w, *, tm=1024, tn=512):
    m, d = x2.shape
    nb = C_WIDTH // tn
    return pl.pallas_call(
        functools.partial(_odd_in_kernel, tm=tm),
        grid=(m // tm, nb),
        in_specs=[pl.BlockSpec((tm, d), lambda i, j: (i, 0)),
                  pl.BlockSpec((1, d), lambda i, j: (0, 0)),
                  pl.BlockSpec((d, tn), lambda i, j: (0, j)),
                  pl.BlockSpec((d, tn), lambda i, j: (0, j + nb)),
                  pl.BlockSpec((d, tn), lambda i, j: (0, j + 2 * nb))],
        out_specs=[pl.BlockSpec((tm, tn), lambda i, j: (i, j)),
                   pl.BlockSpec((tm, tn), lambda i, j: (i, j)),
                   pl.BlockSpec((tm, LANES), lambda i, j: (i, 0))],
        out_shape=[jax.ShapeDtypeStruct((m, C_WIDTH), BF16),
                   jax.ShapeDtypeStruct((m, C_WIDTH), BF16),
                   jax.ShapeDtypeStruct((m, LANES), F32)],
        scratch_shapes=[pltpu.VMEM((tm, d), BF16)],
        compiler_params=_params(("arbitrary", "arbitrary")),
        name="odd_in",
    )(x2, g, w, w, w)


def _head_sumsq(x):
    r = lax.broadcasted_iota(jnp.int32, (LANES, LANES), 0) // HEAD_DIM
    c = lax.broadcasted_iota(jnp.int32, (LANES, LANES), 1) // HEAD_DIM
    ones_bd = (r == c).astype(BF16)
    x2 = x * x
    hi = x2.astype(BF16)
    r1 = x2 - hi.astype(F32)
    mid = r1.astype(BF16)
    lo = (r1 - mid.astype(F32)).astype(BF16)
    dot = lambda a: jnp.dot(a, ones_bd, preferred_element_type=F32)
    return dot(hi) + dot(mid) + dot(lo)


def _head_rms(x, gain):
    return x * lax.rsqrt(_head_sumsq(x) * (1.0 / HEAD_DIM) + EPS) * gain


def _band_block(q, kk, vv0, vv1, bias0, bias1):
    lo = lax.broadcasted_iota(jnp.int32, (BLK, LANES), 1) < HEAD_DIM
    res = []
    for sel, bias, vv in ((lo, bias0, vv0), (jnp.logical_not(lo), bias1, vv1)):
        qm = jnp.where(sel, q, jnp.zeros_like(q))
        s = lax.dot_general(qm, kk, (((1,), (1,)), ((), ())), preferred_element_type=F32) + bias
        m = jnp.max(s, axis=-1, keepdims=True)
        p = jnp.exp(s - m).astype(BF16)
        res.append((jnp.broadcast_to(m, (BLK, LANES)), jnp.dot(p, vv, preferred_element_type=F32)))
    (m0, ul0), (m1, ul1) = res
    l = pltpu.roll(jnp.where(lo, ul1, ul0), HEAD_DIM, 1)
    return jnp.where(lo, m0, m1), l, jnp.where(lo, ul0, ul1)


UNROLL = 4


def _attn_a_kernel(q_ref, k_ref, v_ref, g_ref, gq_ref, gk_ref, snk_ref, bias_ref, o_ref,
                   qn_scr, kk_scr, vv0_scr, vv1_scr, *, seq):
    grp = pl.program_id(1) // (A_HEADS // A_KV_HEADS // 2)
    lane = lax.broadcasted_iota(jnp.int32, (BLK, LANES), 1)
    keep = (lane // HEAD_DIM) == grp
    lo = lane < HEAD_DIM
    zeros = jnp.zeros((BLK, LANES), BF16)
    kk_scr[pl.ds(0, BLK), :] = zeros
    vv0_scr[pl.ds(0, BLK), :] = zeros
    vv1_scr[pl.ds(0, BLK), :] = zeros

    def prep(c, carry):
        r = pl.ds(c * BLK, BLK)
        ro = pl.ds(BLK + c * BLK, BLK)
        qn_scr[r, :] = (_head_rms(q_ref[r, :].astype(F32), gq_ref[...]) * SCALE).astype(BF16)
        kn = _head_rms(k_ref[r, :].astype(F32), gk_ref[...])
        kk_scr[ro, :] = jnp.where(keep, kn, pltpu.roll(kn, HEAD_DIM, 1)).astype(BF16)
        v = v_ref[r, :].astype(F32)
        vd = jnp.where(keep, v, pltpu.roll(v, HEAD_DIM, 1))
        vv0_scr[ro, :] = jnp.where(lo, vd, 1.0).astype(BF16)
        vv1_scr[ro, :] = jnp.where(lo, 1.0, vd).astype(BF16)
        return carry
    lax.fori_loop(0, seq // BLK, prep, 0)

    snk = snk_ref[...]

    def blocks(it, carry):
        for uu in range(UNROLL):
            n = it * UNROLL + uu
            first = (n == 0).astype(jnp.int32)
            r = pl.ds(n * BLK, BLK)
            rk = pl.ds(n * BLK, 2 * BLK)
            m, l, u = _band_block(qn_scr[r, :], kk_scr[rk, :], vv0_scr[rk, :], vv1_scr[rk, :],
                                  bias_ref[0, first], bias_ref[1, first])
            mx = jnp.maximum(m, snk)
            w = jnp.exp(m - mx)
            o = u * (w / (l * w + jnp.exp(snk - mx)))
            o_ref[r, :] = (o * _silu(g_ref[r, :].astype(F32))).astype(o_ref.dtype)
        return carry
    lax.fori_loop(0, seq // BLK // UNROLL, blocks, 0)


def _attn_a(z3, gq2, gk2, snk3, bias_a):
    bsz, seq, _ = z3.shape
    col = lambda off: pl.BlockSpec((None, seq, LANES), lambda b, p: (b, 0, off + p))
    fixed = lambda off: pl.BlockSpec((None, seq, LANES), lambda b, p: (b, 0, off))
    return pl.pallas_call(
        functools.partial(_attn_a_kernel, seq=seq),
        grid=(bsz, A_HEADS // 2),
        in_specs=[col(_QA), fixed(_KA), fixed(_VA), col(_GA),
                  pl.BlockSpec((1, LANES), lambda b, p: (0, 0)),
                  pl.BlockSpec((1, LANES), lambda b, p: (0, 0)),
                  pl.BlockSpec((None, 1, LANES), lambda b, p: (p, 0, 0)),
                  pl.BlockSpec((2, 2, BLK, 2 * BLK), lambda b, p: (p, 0, 0, 0))],
        out_specs=pl.BlockSpec((None, seq, LANES), lambda b, p: (b, 0, p)),
        out_shape=jax.ShapeDtypeStruct((bsz, seq, A_WIDTH), BF16),
        scratch_shapes=[pltpu.VMEM((seq, LANES), BF16),
                        pltpu.VMEM((seq + BLK, LANES), BF16),
                        pltpu.VMEM((seq + BLK, LANES), BF16),
                        pltpu.VMEM((seq + BLK, LANES), BF16)],
        compiler_params=_params(("arbitrary", "arbitrary")),
        name="attn_a",
    )(z3, z3, z3, z3, gq2, gk2, snk3, bias_a)


_PAD_B = 4 * BLK


def _attn_b_kernel(q_ref, k_ref, v_ref, g_ref, gq_ref, gk_ref, bias_ref, o_ref,
                   qn_scr, kn_scr, vv0_scr, vv1_scr, m_scr, l_scr, u_scr, *, seq):
    lo = lax.broadcasted_iota(jnp.int32, (BLK, LANES), 1) < HEAD_DIM
    zeros = jnp.zeros((_PAD_B, LANES), F32)
    kn_scr[pl.ds(0, _PAD_B), :] = zeros
    vv0_scr[pl.ds(0, _PAD_B), :] = zeros
    vv1_scr[pl.ds(0, _PAD_B), :] = zeros

    def prep(c, carry):
        r = pl.ds(c * BLK, BLK)
        ro = pl.ds(_PAD_B + c * BLK, BLK)
        qn_scr[r, :] = _head_rms(q_ref[r, :].astype(F32), gq_ref[...]) * SCALE
        kn_scr[ro, :] = _head_rms(k_ref[r, :].astype(F32), gk_ref[...])
        v = v_ref[r, :].astype(F32)
        vv0_scr[ro, :] = jnp.where(lo, v, 1.0)
        vv1_scr[ro, :] = jnp.where(lo, 1.0, v)
        return carry
    lax.fori_loop(0, seq // BLK, prep, 0)

    for pi, (dil, _) in enumerate(PATTERNS[1:]):
        blocks_per_class = seq // (dil * BLK)
        span = dil * BLK
        single = blocks_per_class == 1

        def blocks(it, carry, pi=pi, dil=dil, span=span, single=single):
            for uu in range(UNROLL):
                blk = it * UNROLL + uu
                n = lax.shift_right_logical(blk, int(math.log2(dil)))
                c = lax.bitwise_and(blk, dil - 1)
                start = n * span + c
                ds = (lambda s: pl.ds(s, BLK)) if dil == 1 else (lambda s: pl.ds(s, BLK, stride=dil))
                rq = ds(start)
                rkc = ds(_PAD_B + start)
                q = qn_scr[rq, :].astype(BF16)
                if single:
                    load = lambda ref: ref[rkc, :].astype(BF16)
                    bias = [bias_ref[pi, hh, 0, :, pl.ds(BLK, BLK)] for hh in range(2)]
                else:
                    rkp = ds(_PAD_B + start - span)
                    load = lambda ref: jnp.concatenate([ref[rkp, :], ref[rkc, :]], axis=0).astype(BF16)
                    first = (n == 0).astype(jnp.int32)
                    bias = [bias_ref[pi, hh, first] for hh in range(2)]
                m, l, u = _band_block(q, load(kn_scr), load(vv0_scr), load(vv1_scr), bias[0], bias[1])
                m_scr[pi, rq, :] = m
                l_scr[pi, rq, :] = l
                u_scr[pi, rq, :] = u
            return carry
        lax.fori_loop(0, blocks_per_class * dil // UNROLL, blocks, 0)

    def combine(c, carry):
        r = pl.ds(c * BLK, BLK)
        ms = [m_scr[pi, r, :] for pi in range(3)]
        mx = jnp.maximum(jnp.maximum(ms[0], ms[1]), ms[2])
        num = jnp.zeros((BLK, LANES), F32)
        den = jnp.zeros((BLK, LANES), F32)
        for pi in range(3):
            w = jnp.exp(ms[pi] - mx)
            num = num + w * u_scr[pi, r, :]
            den = den + w * l_scr[pi, r, :]
        o_ref[r, :] = ((num / den) * _silu(g_ref[r, :].astype(F32))).astype(o_ref.dtype)
        return carry
    lax.fori_loop(0, seq // BLK, combine, 0)


def _attn_b(z3, gq2, gk2, bias_b):
    bsz, seq, _ = z3.shape
    col = lambda off: pl.BlockSpec((None, seq, LANES), lambda b, p: (b, 0, off + p))
    return pl.pallas_call(
        functools.partial(_attn_b_kernel, seq=seq),
        grid=(bsz, B_HEADS // 2),
        in_specs=[col(_QB), col(_KB), col(_VB), col(_GB),
                  pl.BlockSpec((1, LANES), lambda b, p: (0, 0)),
                  pl.BlockSpec((1, LANES), lambda b, p: (0, 0)),
                  pl.BlockSpec((3, 2, 2, BLK, 2 * BLK), lambda b, p: (0, p, 0, 0, 0))],
        out_specs=pl.BlockSpec((None, seq, LANES), lambda b, p: (b, 0, p)),
        out_shape=jax.ShapeDtypeStruct((bsz, seq, B_WIDTH), BF16),
        scratch_shapes=[pltpu.VMEM((seq, LANES), F32),
                        pltpu.VMEM((seq + _PAD_B, LANES), F32),
                        pltpu.VMEM((seq + _PAD_B, LANES), F32),
                        pltpu.VMEM((seq + _PAD_B, LANES), F32),
                        pltpu.VMEM((3, seq, LANES), F32),
                        pltpu.VMEM((3, seq, LANES), F32),
                        pltpu.VMEM((3, seq, LANES), F32)],
        compiler_params=_params(("arbitrary", "arbitrary")),
        name="attn_b",
    )(z3, z3, z3, z3, gq2, gk2, bias_b)


def _even_out_kernel(ya_ref, yb_ref, wa_ref, wb_ref, x_ref, o_ref):
    acc = jnp.dot(ya_ref[...], wa_ref[...], preferred_element_type=F32)
    acc = acc + jnp.dot(yb_ref[...], wb_ref[...], preferred_element_type=F32)
    o_ref[...] = x_ref[...] + acc


def _even_out(ya, yb, w, x2, *, tm=1024, tn=1024):
    m, d = x2.shape
    ka, kb = ya.shape[1], yb.shape[1]
    assert ka == kb
    return pl.pallas_call(
        _even_out_kernel,
        grid=(m // tm, d // tn),
        in_specs=[pl.BlockSpec((tm, ka), lambda i, j: (i, 0)),
                  pl.BlockSpec((tm, kb), lambda i, j: (i, 0)),
                  pl.BlockSpec((ka, tn), lambda i, j: (0, j)),
                  pl.BlockSpec((kb, tn), lambda i, j: (1, j)),
                  pl.BlockSpec((tm, tn), lambda i, j: (i, j))],
        out_specs=pl.BlockSpec((tm, tn), lambda i, j: (i, j)),
        out_shape=jax.ShapeDtypeStruct((m, d), F32),
        compiler_params=_params(("arbitrary", "arbitrary")),
        name="even_out",
    )(ya, yb, w, w, x2)


def _odd_out_kernel(ug_ref, v_ref, ssq_ref, vg_ref, ws_ref, bs_ref, w_ref, x_ref, o_ref, y_scr, *, tm):
    @pl.when(pl.program_id(1) == 0)
    def _():
        tril = (lax.broadcasted_iota(jnp.int32, (C_CHUNK, C_CHUNK), 0)
                >= lax.broadcasted_iota(jnp.int32, (C_CHUNK, C_CHUNK), 1))

        def chunk(c, carry):
            r = pl.ds(c * C_CHUNK, C_CHUNK)
            inv = lax.rsqrt(ssq_ref[r, :][:, :1] * (1.0 / C_WIDTH) + EPS)
            for g in range(C_GROUPS):
                cols = pl.ds(g * C_GROUP_DIM, C_GROUP_DIM)
                vn = (v_ref[r, cols].astype(F32) * inv * vg_ref[:, cols]).astype(BF16)
                wt = jnp.where(tril, ws_ref[g], 0.0).astype(BF16)
                s = jnp.dot(wt, vn, preferred_element_type=F32) + bs_ref[:, g:g + 1]
                y_scr[r, cols] = (ug_ref[r, cols].astype(F32) * s).astype(BF16)
            return carry
        lax.fori_loop(0, tm // C_CHUNK, chunk, 0)

    o_ref[...] = x_ref[...] + jnp.dot(y_scr[...], w_ref[...], preferred_element_type=F32)


def _odd_out(ug, v, ssq, vg, ws, bs_t, w, x2, *, tm=512, tn=512):
    m, d = x2.shape
    return pl.pallas_call(
        functools.partial(_odd_out_kernel, tm=tm),
        grid=(m // tm, d // tn),
        in_specs=[pl.BlockSpec((tm, C_WIDTH), lambda i, j: (i, 0)),
                  pl.BlockSpec((tm, C_WIDTH), lambda i, j: (i, 0)),
                  pl.BlockSpec((tm, LANES), lambda i, j: (i, 0)),
                  pl.BlockSpec((1, C_WIDTH), lambda i, j: (0, 0)),
                  pl.BlockSpec((C_GROUPS, C_CHUNK, C_CHUNK), lambda i, j: (0, 0, 0)),
                  pl.BlockSpec((C_CHUNK, C_GROUPS), lambda i, j: (0, 0)),
                  pl.BlockSpec((C_WIDTH, tn), lambda i, j: (0, j)),
                  pl.BlockSpec((tm, tn), lambda i, j: (i, j))],
        out_specs=pl.BlockSpec((tm, tn), lambda i, j: (i, j)),
        out_shape=jax.ShapeDtypeStruct((m, d), F32),
        scratch_shapes=[pltpu.VMEM((tm, C_WIDTH), BF16)],
        compiler_params=_params(("arbitrary", "arbitrary")),
        name="odd_out",
    )(ug, v, ssq, vg, ws, bs_t, w, x2)


def kernel(x, ev_ln_g, ev_w_in, ev_qk_g, ev_sinks, ev_w_out, od_ln_g, od_w_in, od_v_g, od_w_s,
           od_b_s, od_w_out, rel_bias):
    bsz, seq, d = x.shape
    depth = ev_ln_g.shape[0] + od_ln_g.shape[0]
    tabs = _bias_tables(rel_bias)
    x2 = x.reshape(bsz * seq, d)
    for i in range(depth):
        j = i // 2
        if i % 2 == 0:
            z = _even_in(x2, ev_ln_g[j].reshape(1, d), ev_w_in[j].astype(BF16))
            z3 = z.reshape(bsz, seq, EVEN_IN)
            gains = jnp.tile(ev_qk_g[j].astype(F32), (1, 2))
            snk3 = ev_sinks[j].astype(F32).reshape(A_HEADS // 2, 2, 1)
            snk3 = jnp.broadcast_to(snk3, (A_HEADS // 2, 2, HEAD_DIM)).reshape(A_HEADS // 2, 1, LANES)
            ya = _attn_a(z3, gains[0:1], gains[1:2], snk3, tabs[0])
            yb = _attn_b(z3, gains[2:3], gains[3:4], tabs[1:4])
            x2 = _even_out(ya.reshape(bsz * seq, A_WIDTH), yb.reshape(bsz * seq, B_WIDTH),
                           ev_w_out[j].astype(BF16), x2)
        else:
            ug, v, ssq = _odd_in(x2, od_ln_g[j].reshape(1, d), od_w_in[j].astype(BF16))
            x2 = _odd_out(ug, v, ssq, od_v_g[j].astype(F32).reshape(1, C_WIDTH),
                          od_w_s[j].astype(F32), od_b_s[j].astype(F32).T,
                          od_w_out[j].astype(BF16), x2)
    return x2.reshape(bsz, seq, d)
```

```python
import functools
import math

import numpy as np
import jax
import jax.numpy as jnp
from jax import lax
from jax.experimental import pallas as pl
from jax.experimental.pallas import tpu as pltpu

F32 = jnp.float32
BF16 = jnp.bfloat16

D_MODEL = 2048
HEAD_DIM = 64
A_HEADS = 16
A_KV_HEADS = 2
B_HEADS = 16
BLK = 128
NUM_BUCKETS = 32
REL_MAX_DISTANCE = 2048
A_WIDTH = A_HEADS * HEAD_DIM
B_WIDTH = B_HEADS * HEAD_DIM
EVEN_IN = 6400
C_WIDTH = 2 * D_MODEL
C_GROUPS = 16
C_GROUP_DIM = C_WIDTH // C_GROUPS
C_CHUNK = 128
EPS = 1e-6
NEG = -1e30
SCALE = HEAD_DIM ** -0.5
PATTERNS = ((1, 128), (4, 128), (16, 128), (1, 127))
N_B_PATTERNS = 3

LANES = 128
VMEM_LIMIT = 56 * 1024 * 1024

_QA, _KA, _VA, _GA = 0, 8, 9, 10
_QB, _KB, _VB, _GB = 18, 26, 34, 42


def _params(sem):
    return pltpu.CompilerParams(dimension_semantics=sem, vmem_limit_bytes=VMEM_LIMIT)


def _bucket_tables():
    a = np.arange(BLK)[:, None]
    b = np.arange(2 * BLK)[None, :]
    dist = BLK + a - b
    max_exact = NUM_BUCKETS // 2
    out = []
    for dil, max_dist in PATTERNS:
        n = np.maximum(dist * dil, 0)
        large = max_exact + (np.log(np.maximum(n, 1) / max_exact)
                             / np.log(REL_MAX_DISTANCE / max_exact)
                             * (NUM_BUCKETS - max_exact)).astype(np.int32)
        large = np.minimum(large, NUM_BUCKETS - 1)
        bucket = np.where(n < max_exact, n, large).astype(np.int32)
        valid = (dist >= 0) & (dist <= max_dist)
        out.append(np.where(valid, bucket, -1).astype(np.int32))
    return np.stack(out)


def _bias_kernel(tbl_ref, bucket_ref, out_ref):
    t = pl.program_id(0)
    h = pl.program_id(1)
    col = h + jnp.where(t < N_B_PATTERNS, A_HEADS, 0)
    bk = bucket_ref[...]
    acc = jnp.full(bk.shape, NEG, F32)
    for b in range(NUM_BUCKETS):
        acc = jnp.where(bk == b, tbl_ref[b, col], acc)
    out_ref[0] = acc
    prev_cols = lax.broadcasted_iota(jnp.int32, bk.shape, 1) < BLK
    out_ref[1] = jnp.where(prev_cols, NEG, acc)


def _bias_tables(rel_bias):
    buckets = jnp.asarray(_bucket_tables())
    return pl.pallas_call(
        _bias_kernel,
        grid=(len(PATTERNS), 16),
        in_specs=[pl.BlockSpec(memory_space=pltpu.SMEM),
                  pl.BlockSpec((None, BLK, 2 * BLK), lambda t, h: (t, 0, 0))],
        out_specs=pl.BlockSpec((None, None, 2, BLK, 2 * BLK), lambda t, h: (t, h, 0, 0, 0)),
        out_shape=jax.ShapeDtypeStruct((len(PATTERNS), 16, 2, BLK, 2 * BLK), F32),
        compiler_params=_params(("arbitrary", "arbitrary")),
        name="bias_tables",
    )(rel_bias.astype(F32), buckets)


def _norm_rows(x_ref, g_ref, h_scr, tm):
    def body(c, carry):
        r = pl.ds(c * 128, 128)
        x = x_ref[r, :]
        ms = jnp.mean(x * x, axis=-1, keepdims=True)
        h_scr[r, :] = (x * lax.rsqrt(ms + EPS) * g_ref[...]).astype(BF16)
        return carry
    lax.fori_loop(0, tm // 128, body, 0)


def _even_in_kernel(x_ref, g_ref, w_ref, o_ref, h_scr, *, tm):
    @pl.when(pl.program_id(1) == 0)
    def _():
        _norm_rows(x_ref, g_ref, h_scr, tm)
    o_ref[...] = jnp.dot(h_scr[...], w_ref[...], preferred_element_type=F32).astype(o_ref.dtype)


def _even_in(x2, g, w, *, tm=1024, tn=1280):
    m, d = x2.shape
    n = w.shape[1]
    return pl.pallas_call(
        functools.partial(_even_in_kernel, tm=tm),
        grid=(m // tm, n // tn),
        in_specs=[pl.BlockSpec((tm, d), lambda i, j: (i, 0)),
                  pl.BlockSpec((1, d), lambda i, j: (0, 0)),
                  pl.BlockSpec((d, tn), lambda i, j: (0, j))],
        out_specs=pl.BlockSpec((tm, tn), lambda i, j: (i, j)),
        out_shape=jax.ShapeDtypeStruct((m, n), BF16),
        scratch_shapes=[pltpu.VMEM((tm, d), BF16)],
        compiler_params=_params(("arbitrary", "arbitrary")),
        name="even_in",
    )(x2, g, w)


def _gelu(x):
    return 0.5 * x * (1.0 + lax.erf(x * (1.0 / math.sqrt(2.0))))


def _silu(x):
    return x * jax.nn.sigmoid(x)


def _odd_in_kernel(x_ref, g_ref, wu_ref, wv_ref, wg_ref, ug_ref, v_ref, ssq_ref, h_scr, *, tm):
    j = pl.program_id(1)

    @pl.when(j == 0)
    def _():
        _norm_rows(x_ref, g_ref, h_scr, tm)
        ssq_ref[...] = jnp.zeros_like(ssq_ref)

    h = h_scr[...]
    u = jnp.dot(h, wu_ref[...], preferred_element_type=F32)
    gt = jnp.dot(h, wg_ref[...], preferred_element_type=F32)
    ug_ref[...] = (_gelu(u) * _silu(gt)).astype(ug_ref.dtype)
    v = _gelu(jnp.dot(h, wv_ref[...], preferred_element_type=F32))
    v_ref[...] = v.astype(v_ref.dtype)
    ssq_ref[...] += jnp.sum(v * v, axis=-1, keepdims=True)


def _odd_in(x2, g, w, *, tm=1024, tn=512):
    m, d = x2.shape
    nb = C_WIDTH // tn
    return pl.pallas_call(
        functools.partial(_odd_in_kernel, tm=tm),
        grid=(m // tm, nb),
        in_specs=[pl.BlockSpec((tm, d), lambda i, j: (i, 0)),
                  pl.BlockSpec((1, d), lambda i, j: (0, 0)),
                  pl.BlockSpec((d, tn), lambda i, j: (0, j)),
                  pl.BlockSpec((d, tn), lambda i, j: (0, j + nb)),
                  pl.BlockSpec((d, tn), lambda i, j: (0, j + 2 * nb))],
        out_specs=[pl.BlockSpec((tm, tn), lambda i, j: (i, j)),
                   pl.BlockSpec((tm, tn), lambda i, j: (i, j)),
                   pl.BlockSpec((tm, LANES), lambda i, j: (i, 0))],
        out_shape=[jax.ShapeDtypeStruct((m, C_WIDTH), BF16),
                   jax.ShapeDtypeStruct((m, C_WIDTH), BF16),
                   jax.ShapeDtypeStruct((m, LANES), F32)],
        scratch_shapes=[pltpu.VMEM((tm, d), BF16)],
        compiler_params=_params(("arbitrary", "arbitrary")),
        name="odd_in",
    )(x2, g, w, w, w)


PAD = BLK
UNROLL = 4


def _and(b, mask):
    return b & mask if isinstance(b, int) else lax.bitwise_and(b, mask)


def _shr(b, s):
    return b >> s if isinstance(b, int) else lax.shift_right_logical(b, s)


def _lane_lo(rows):
    return lax.broadcasted_iota(jnp.int32, (rows, LANES), 1) < HEAD_DIM


def _head_rms(x, gain):
    w = x.shape[1]
    r = lax.broadcasted_iota(jnp.int32, (w, w), 0) // HEAD_DIM
    c = lax.broadcasted_iota(jnp.int32, (w, w), 1) // HEAD_DIM
    ones_bd = (r == c).astype(BF16)
    x2 = x * x
    hi = x2.astype(BF16)
    lo = (x2 - hi.astype(F32)).astype(BF16)
    ssq = (jnp.dot(hi, ones_bd, preferred_element_type=F32)
           + jnp.dot(lo, ones_bd, preferred_element_type=F32))
    return x * lax.rsqrt(ssq * (1.0 / HEAD_DIM) + EPS) * gain


def _scores_softmax(qm, kk, bias):
    s = lax.dot_general(qm, kk, (((1,), (1,)), ((), ())), preferred_element_type=F32) + bias
    m = jnp.max(s, axis=-1, keepdims=True)
    return m, jnp.exp(s - m).astype(BF16)


def _pipelined(n_groups, stage1, stage2):
    stage1(0)

    def body(g, carry):
        stage2(g - 1)
        stage1(g)
        return carry
    lax.fori_loop(1, n_groups, body, 0)
    stage2(n_groups - 1)


def _attn_a_kernel(q0_ref, q1_ref, q2_ref, q3_ref, k_ref, v_ref, g0_ref, g1_ref, g2_ref, g3_ref,
                   gq_ref, gk_ref, snk_ref, bias_ref, o_ref, qb, kb, v0b, v1b, p_scr, m_scr, *, seq):
    q_refs = (q0_ref, q1_ref, q2_ref, q3_ref)
    g_refs = (g0_ref, g1_ref, g2_ref, g3_ref)
    npair = len(q_refs)
    nblk = seq // BLK
    lane = lax.broadcasted_iota(jnp.int32, (BLK, LANES), 1)
    keep = (lane // HEAD_DIM) == pl.program_id(1)
    lo = lane < HEAD_DIM
    hi = jnp.logical_not(lo)
    zeros = jnp.zeros((PAD, LANES), BF16)
    kb[pl.ds(0, PAD), :] = zeros
    v0b[pl.ds(0, PAD), :] = zeros
    v1b[pl.ds(0, PAD), :] = zeros
    gq = gq_ref[...] * SCALE
    gq = jnp.concatenate([gq, gq], axis=1)

    def prep(c, carry):
        r = pl.ds(c * BLK, BLK)
        ro = pl.ds(PAD + c * BLK, BLK)
        for pp in range(0, npair, 2):
            qq = jnp.concatenate([q_refs[pp][r, :], q_refs[pp + 1][r, :]], axis=1).astype(F32)
            qn = _head_rms(qq, gq).astype(BF16)
            qb[pp, r, :] = qn[:, :LANES]
            qb[pp + 1, r, :] = qn[:, LANES:]
        kn = _head_rms(k_ref[r, :].astype(F32), gk_ref[...])
        kb[ro, :] = jnp.where(keep, kn, pltpu.roll(kn, HEAD_DIM, 1)).astype(BF16)
        v = v_ref[r, :].astype(F32)
        vd = jnp.where(keep, v, pltpu.roll(v, HEAD_DIM, 1))
        v0b[ro, :] = jnp.where(lo, vd, 1.0).astype(BF16)
        v1b[ro, :] = jnp.where(lo, 1.0, vd).astype(BF16)
        return carry
    lax.fori_loop(0, nblk, prep, 0, unroll=2)

    def stage1(n):
        slot = _and(n, 1)
        first = jnp.where(n == 0, 1, 0)
        q_rows = pl.ds(n * BLK, BLK)
        kk = kb[pl.ds(n * BLK, 2 * BLK), :]
        for pp in range(npair):
            q = qb[pp, q_rows, :]
            ms = []
            for hh, sel in enumerate((lo, hi)):
                m, p = _scores_softmax(jnp.where(sel, q, jnp.zeros_like(q)), kk, bias_ref[2 * pp + hh, first])
                p_scr[slot, pp, hh] = p
                ms.append(jnp.broadcast_to(m, (BLK, LANES)))
            m_scr[slot, pp] = jnp.where(lo, ms[0], ms[1])

    def stage2(n):
        slot = _and(n, 1)
        r = pl.ds(n * BLK, BLK)
        rk = pl.ds(n * BLK, 2 * BLK)
        for pp in range(npair):
            ul0 = jnp.dot(p_scr[slot, pp, 0], v0b[rk, :], preferred_element_type=F32)
            ul1 = jnp.dot(p_scr[slot, pp, 1], v1b[rk, :], preferred_element_type=F32)
            u = jnp.where(lo, ul0, ul1)
            l = pltpu.roll(jnp.where(lo, ul1, ul0), HEAD_DIM, 1)
            m = m_scr[slot, pp]
            snk = snk_ref[pp:pp + 1, :]
            mx = jnp.maximum(m, snk)
            w = jnp.exp(m - mx)
            o = u * (w / (l * w + jnp.exp(snk - mx)))
            o_ref[r, pl.ds(pp * LANES, LANES)] = (o * _silu(g_refs[pp][r, :].astype(F32))).astype(o_ref.dtype)

    _pipelined(nblk, stage1, stage2)


def _attn_a(z3, gq2, gk2, snk3, tabs):
    bsz, seq, _ = z3.shape
    npair = A_HEADS // A_KV_HEADS // 2

    def col(off, pp):
        return pl.BlockSpec((None, seq, LANES), lambda b, gi: (b, 0, off + npair * gi + pp))

    def fixed(off):
        return pl.BlockSpec((None, seq, LANES), lambda b, gi: (b, 0, off))

    return pl.pallas_call(
        functools.partial(_attn_a_kernel, seq=seq),
        grid=(bsz, A_KV_HEADS),
        in_specs=[col(_QA, 0), col(_QA, 1), col(_QA, 2), col(_QA, 3), fixed(_KA), fixed(_VA),
                  col(_GA, 0), col(_GA, 1), col(_GA, 2), col(_GA, 3),
                  pl.BlockSpec((1, LANES), lambda b, gi: (0, 0)),
                  pl.BlockSpec((1, LANES), lambda b, gi: (0, 0)),
                  pl.BlockSpec((None, npair, LANES), lambda b, gi: (gi, 0, 0)),
                  pl.BlockSpec((None, 2 * npair, 2, BLK, 2 * BLK), lambda b, gi: (N_B_PATTERNS, gi, 0, 0, 0))],
        out_specs=pl.BlockSpec((None, seq, npair * LANES), lambda b, gi: (b, 0, gi)),
        out_shape=jax.ShapeDtypeStruct((bsz, seq, A_WIDTH), BF16),
        scratch_shapes=[pltpu.VMEM((npair, seq, LANES), BF16),
                        pltpu.VMEM((PAD + seq, LANES), BF16),
                        pltpu.VMEM((PAD + seq, LANES), BF16),
                        pltpu.VMEM((PAD + seq, LANES), BF16),
                        pltpu.VMEM((2, npair, 2, BLK, 2 * BLK), BF16),
                        pltpu.VMEM((2, npair, BLK, LANES), F32)],
        compiler_params=_params(("arbitrary", "arbitrary")),
        name="attn_a",
    )(z3, z3, z3, z3, z3, z3, z3, z3, z3, z3, gq2, gk2, snk3, tabs)


def _attn_b_kernel(q_ref, k_ref, v_ref, g_ref, gq_ref, gk_ref, bias_ref, o_ref,
                   qf, kf, vf, q4f, k4f, v4f, qb, kb, v0b, v1b, p_scr, m_scr, l_scr, u_scr, *, seq):
    lo = _lane_lo(BLK)
    hi = jnp.logical_not(lo)
    nblk = seq // BLK
    zeros = jnp.zeros((PAD, LANES), BF16)
    for pi in range(2):
        kb[pi, pl.ds(0, PAD), :] = zeros
        v0b[pi, pl.ds(0, PAD), :] = zeros
        v1b[pi, pl.ds(0, PAD), :] = zeros

    def put(pi, dst, q, k, v):
        qb[pi, dst, :] = q.astype(BF16)
        kb[pi, dst, :] = k.astype(BF16)
        v0b[pi, dst, :] = jnp.where(lo, v, 1.0).astype(BF16)
        v1b[pi, dst, :] = jnp.where(lo, 1.0, v).astype(BF16)

    def strided(t):
        return pl.ds(_and(t, 3) * (4 * BLK) + _shr(t, 2), BLK, stride=4)

    gains = jnp.concatenate([gq_ref[...] * SCALE, gk_ref[...]], axis=1)

    def prep(c, carry):
        r = pl.ds(c * BLK, BLK)
        qk = jnp.concatenate([q_ref[r, :], k_ref[r, :]], axis=1).astype(F32)
        n = _head_rms(qk, gains)
        q, k, v = n[:, :LANES], n[:, LANES:], v_ref[r, :].astype(F32)
        qf[r, :] = q
        kf[r, :] = k
        vf[r, :] = v
        put(0, pl.ds(PAD + c * BLK, BLK), q, k, v)
        return carry
    lax.fori_loop(0, nblk, prep, 0, unroll=2)

    def deint4(t, carry):
        src = strided(t)
        dst = pl.ds(t * BLK, BLK)
        q, k, v = qf[src, :], kf[src, :], vf[src, :]
        q4f[dst, :] = q
        k4f[dst, :] = k
        v4f[dst, :] = v
        put(1, pl.ds(PAD + t * BLK, BLK), q, k, v)
        return carry
    lax.fori_loop(0, nblk, deint4, 0)

    def deint16(t, carry):
        src = strided(t)
        put(2, pl.ds(PAD + t * BLK, BLK), q4f[src, :], k4f[src, :], v4f[src, :])
        return carry
    lax.fori_loop(0, nblk, deint16, 0)

    for pi in range(N_B_PATTERNS):
        single = pi == 2
        keys = BLK if single else 2 * BLK
        koff = PAD if single else PAD - BLK

        def out_rows(b, pi=pi):
            return pl.ds(b * BLK, BLK) if pi == 0 else strided(b)

        def biases(b, pi=pi, single=single):
            if single:
                return [bias_ref[pi, hh, 0, :, pl.ds(BLK, BLK)] for hh in range(2)]
            first = jnp.where(_and(b, nblk - 1 if pi == 0 else 3) == 0, 1, 0)
            return [bias_ref[pi, hh, first] for hh in range(2)]

        def stage1(g, pi=pi, keys=keys, koff=koff, out_rows=out_rows, biases=biases):
            for uu in range(UNROLL):
                b = g * UNROLL + uu
                q = qb[pi, pl.ds(PAD + b * BLK, BLK), :]
                kk = kb[pi, pl.ds(koff + b * BLK, keys), :]
                bias = biases(b)
                ms = []
                for hh, sel in enumerate((lo, hi)):
                    m, p = _scores_softmax(jnp.where(sel, q, jnp.zeros_like(q)), kk, bias[hh])
                    p_scr[b, hh, :, pl.ds(0, keys)] = p
                    ms.append(jnp.broadcast_to(m, (BLK, LANES)))
                m_scr[pi, out_rows(b), :] = jnp.where(lo, ms[0], ms[1])

        def stage2(g, pi=pi, keys=keys, koff=koff, out_rows=out_rows):
            for uu in range(UNROLL):
                b = g * UNROLL + uu
                rk = pl.ds(koff + b * BLK, keys)
                ul0 = jnp.dot(p_scr[b, 0, :, pl.ds(0, keys)], v0b[pi, rk, :], preferred_element_type=F32)
                ul1 = jnp.dot(p_scr[b, 1, :, pl.ds(0, keys)], v1b[pi, rk, :], preferred_element_type=F32)
                u_scr[pi, out_rows(b), :] = jnp.where(lo, ul0, ul1)
                l_scr[pi, out_rows(b), :] = pltpu.roll(jnp.where(lo, ul1, ul0), HEAD_DIM, 1)

        _pipelined(nblk // UNROLL, stage1, stage2)

    def renat(t, carry):
        for ref in (m_scr, l_scr, u_scr):
            ref[3, strided(t), :] = ref[2, pl.ds(t * BLK, BLK), :]
        return carry
    lax.fori_loop(0, nblk, renat, 0)

    def combine(c, carry):
        r = pl.ds(c * BLK, BLK)
        slots = (0, 1, 3)
        ms = [m_scr[s, r, :] for s in slots]
        mx = jnp.maximum(jnp.maximum(ms[0], ms[1]), ms[2])
        num = jnp.zeros((BLK, LANES), F32)
        den = jnp.zeros((BLK, LANES), F32)
        for s, m in zip(slots, ms):
            w = jnp.exp(m - mx)
            num = num + w * u_scr[s, r, :]
            den = den + w * l_scr[s, r, :]
        o_ref[r, :] = ((num / den) * _silu(g_ref[r, :].astype(F32))).astype(o_ref.dtype)
        return carry
    lax.fori_loop(0, nblk, combine, 0, unroll=2)


def _attn_b(z3, gq2, gk2, tabs):
    bsz, seq, _ = z3.shape
    col = lambda off: pl.BlockSpec((None, seq, LANES), lambda b, p: (b, 0, off + p))
    f32_rows = pltpu.VMEM((seq, LANES), F32)
    bf16_ops = pltpu.VMEM((N_B_PATTERNS, PAD + seq, LANES), BF16)
    stats = pltpu.VMEM((N_B_PATTERNS + 1, seq, LANES), F32)
    return pl.pallas_call(
        functools.partial(_attn_b_kernel, seq=seq),
        grid=(bsz, B_HEADS // 2),
        in_specs=[col(_QB), col(_KB), col(_VB), col(_GB),
                  pl.BlockSpec((1, LANES), lambda b, p: (0, 0)),
                  pl.BlockSpec((1, LANES), lambda b, p: (0, 0)),
                  pl.BlockSpec((N_B_PATTERNS, 2, 2, BLK, 2 * BLK), lambda b, p: (0, p, 0, 0, 0))],
        out_specs=pl.BlockSpec((None, seq, LANES), lambda b, p: (b, 0, p)),
        out_shape=jax.ShapeDtypeStruct((bsz, seq, B_WIDTH), BF16),
        scratch_shapes=[f32_rows] * 6 + [bf16_ops] * 4
                       + [pltpu.VMEM((seq // BLK, 2, BLK, 2 * BLK), BF16)] + [stats] * 3,
        compiler_params=_params(("arbitrary", "arbitrary")),
        name="attn_b",
    )(z3, z3, z3, z3, gq2, gk2, tabs)


def _even_out_kernel(ya_ref, yb_ref, wa_ref, wb_ref, x_ref, o_ref):
    acc = jnp.dot(ya_ref[...], wa_ref[...], preferred_element_type=F32)
    acc = acc + jnp.dot(yb_ref[...], wb_ref[...], preferred_element_type=F32)
    o_ref[...] = x_ref[...] + acc


def _even_out(ya, yb, w, x2, *, tm=1024, tn=1024):
    m, d = x2.shape
    ka, kb = ya.shape[1], yb.shape[1]
    assert ka == kb
    return pl.pallas_call(
        _even_out_kernel,
        grid=(m // tm, d // tn),
        in_specs=[pl.BlockSpec((tm, ka), lambda i, j: (i, 0)),
                  pl.BlockSpec((tm, kb), lambda i, j: (i, 0)),
                  pl.BlockSpec((ka, tn), lambda i, j: (0, j)),
                  pl.BlockSpec((kb, tn), lambda i, j: (1, j)),
                  pl.BlockSpec((tm, tn), lambda i, j: (i, j))],
        out_specs=pl.BlockSpec((tm, tn), lambda i, j: (i, j)),
        out_shape=jax.ShapeDtypeStruct((m, d), F32),
        compiler_params=_params(("arbitrary", "arbitrary")),
        name="even_out",
    )(ya, yb, w, w, x2)


def _odd_out_kernel(ug_ref, v_ref, ssq_ref, vg_ref, ws_ref, bs_ref, w_ref, x_ref, o_ref, y_scr, *, tm):
    @pl.when(pl.program_id(1) == 0)
    def _():
        tril = (lax.broadcasted_iota(jnp.int32, (C_CHUNK, C_CHUNK), 0)
                >= lax.broadcasted_iota(jnp.int32, (C_CHUNK, C_CHUNK), 1))

        def chunk(c, carry):
            r = pl.ds(c * C_CHUNK, C_CHUNK)
            inv = lax.rsqrt(ssq_ref[r, :][:, :1] * (1.0 / C_WIDTH) + EPS)
            for g in range(C_GROUPS):
                cols = pl.ds(g * C_GROUP_DIM, C_GROUP_DIM)
                vn = (v_ref[r, cols].astype(F32) * inv * vg_ref[:, cols]).astype(BF16)
                wt = jnp.where(tril, ws_ref[g], 0.0).astype(BF16)
                s = jnp.dot(wt, vn, preferred_element_type=F32) + bs_ref[:, g:g + 1]
                y_scr[r, cols] = (ug_ref[r, cols].astype(F32) * s).astype(BF16)
            return carry
        lax.fori_loop(0, tm // C_CHUNK, chunk, 0)

    o_ref[...] = x_ref[...] + jnp.dot(y_scr[...], w_ref[...], preferred_element_type=F32)


def _odd_out(ug, v, ssq, vg, ws, bs_t, w, x2, *, tm=512, tn=512):
    m, d = x2.shape
    return pl.pallas_call(
        functools.partial(_odd_out_kernel, tm=tm),
        grid=(m // tm, d // tn),
        in_specs=[pl.BlockSpec((tm, C_WIDTH), lambda i, j: (i, 0)),
                  pl.BlockSpec((tm, C_WIDTH), lambda i, j: (i, 0)),
                  pl.BlockSpec((tm, LANES), lambda i, j: (i, 0)),
                  pl.BlockSpec((1, C_WIDTH), lambda i, j: (0, 0)),
                  pl.BlockSpec((C_GROUPS, C_CHUNK, C_CHUNK), lambda i, j: (0, 0, 0)),
                  pl.BlockSpec((C_CHUNK, C_GROUPS), lambda i, j: (0, 0)),
                  pl.BlockSpec((C_WIDTH, tn), lambda i, j: (0, j)),
                  pl.BlockSpec((tm, tn), lambda i, j: (i, j))],
        out_specs=pl.BlockSpec((tm, tn), lambda i, j: (i, j)),
        out_shape=jax.ShapeDtypeStruct((m, d), F32),
        scratch_shapes=[pltpu.VMEM((tm, C_WIDTH), BF16)],
        compiler_params=_params(("arbitrary", "arbitrary")),
        name="odd_out",
    )(ug, v, ssq, vg, ws, bs_t, w, x2)


def kernel(x, ev_ln_g, ev_w_in, ev_qk_g, ev_sinks, ev_w_out, od_ln_g, od_w_in, od_v_g, od_w_s,
           od_b_s, od_w_out, rel_bias):
    bsz, seq, d = x.shape
    depth = ev_ln_g.shape[0] + od_ln_g.shape[0]
    tabs = _bias_tables(rel_bias)
    x2 = x.reshape(bsz * seq, d)
    for i in range(depth):
        j = i // 2
        if i % 2 == 0:
            z = _even_in(x2, ev_ln_g[j].reshape(1, d), ev_w_in[j].astype(BF16))
            z3 = z.reshape(bsz, seq, EVEN_IN)
            gains = jnp.tile(ev_qk_g[j].astype(F32), (1, 2))
            snk3 = jnp.repeat(ev_sinks[j].astype(F32), HEAD_DIM).reshape(A_KV_HEADS, -1, LANES)
            ya = _attn_a(z3, gains[0:1], gains[1:2], snk3, tabs)
            yb = _attn_b(z3, gains[2:3], gains[3:4], tabs)
            x2 = _even_out(ya.reshape(bsz * seq, A_WIDTH), yb.reshape(bsz * seq, B_WIDTH),
                           ev_w_out[j].astype(BF16), x2)
        else:
            ug, v, ssq = _odd_in(x2, od_ln_g[j].reshape(1, d), od_w_in[j].astype(BF16))
            x2 = _odd_out(ug, v, ssq, od_v_g[j].astype(F32).reshape(1, C_WIDTH),
                          od_w_s[j].astype(F32), od_b_s[j].astype(F32).T,
                          od_w_out[j].astype(BF16), x2)
    return x2.reshape(bsz, seq, d)
```

```python
import functools
import math

import numpy as np
import jax
import jax.numpy as jnp
from jax import lax
from jax.experimental import pallas as pl
from jax.experimental.pallas import tpu as pltpu

F32 = jnp.float32
BF16 = jnp.bfloat16

D_MODEL = 2048
HEAD_DIM = 64
A_HEADS = 16
A_KV_HEADS = 2
B_HEADS = 16
BLK = 128
NUM_BUCKETS = 32
REL_MAX_DISTANCE = 2048
A_WIDTH = A_HEADS * HEAD_DIM
B_WIDTH = B_HEADS * HEAD_DIM
EVEN_IN = 6400
C_WIDTH = 2 * D_MODEL
C_GROUPS = 16
C_GROUP_DIM = C_WIDTH // C_GROUPS
C_CHUNK = 128
EPS = 1e-6
NEG = -1e30
SCALE = HEAD_DIM ** -0.5
PATTERNS = ((1, 128), (4, 128), (16, 128), (1, 127))
N_B_PATTERNS = 3

LANES = 128
VMEM_LIMIT = 56 * 1024 * 1024

_QA, _KA, _VA, _GA = 0, 8, 9, 10
_QB, _KB, _VB, _GB = 18, 26, 34, 42


def _params(sem):
    return pltpu.CompilerParams(dimension_semantics=sem, vmem_limit_bytes=VMEM_LIMIT)


def _bucket_tables():
    a = np.arange(BLK)[:, None]
    b = np.arange(2 * BLK)[None, :]
    dist = BLK + a - b
    max_exact = NUM_BUCKETS // 2
    out = []
    for dil, max_dist in PATTERNS:
        n = np.maximum(dist * dil, 0)
        large = max_exact + (np.log(np.maximum(n, 1) / max_exact)
                             / np.log(REL_MAX_DISTANCE / max_exact)
                             * (NUM_BUCKETS - max_exact)).astype(np.int32)
        large = np.minimum(large, NUM_BUCKETS - 1)
        bucket = np.where(n < max_exact, n, large).astype(np.int32)
        valid = (dist >= 0) & (dist <= max_dist)
        out.append(np.where(valid, bucket, -1).astype(np.int32))
    return np.stack(out)


def _bias_kernel(tbl_ref, bucket_ref, out_ref):
    t = pl.program_id(0)
    h = pl.program_id(1)
    col = h + jnp.where(t < N_B_PATTERNS, A_HEADS, 0)
    bk = bucket_ref[...]
    acc = jnp.full(bk.shape, NEG, F32)
    for b in range(NUM_BUCKETS):
        acc = jnp.where(bk == b, tbl_ref[b, col], acc)
    out_ref[0] = acc
    prev_cols = lax.broadcasted_iota(jnp.int32, bk.shape, 1) < BLK
    out_ref[1] = jnp.where(prev_cols, NEG, acc)


def _bias_tables(rel_bias):
    buckets = jnp.asarray(_bucket_tables())
    return pl.pallas_call(
        _bias_kernel,
        grid=(len(PATTERNS), 16),
        in_specs=[pl.BlockSpec(memory_space=pltpu.SMEM),
                  pl.BlockSpec((None, BLK, 2 * BLK), lambda t, h: (t, 0, 0))],
        out_specs=pl.BlockSpec((None, None, 2, BLK, 2 * BLK), lambda t, h: (t, h, 0, 0, 0)),
        out_shape=jax.ShapeDtypeStruct((len(PATTERNS), 16, 2, BLK, 2 * BLK), F32),
        compiler_params=_params(("arbitrary", "arbitrary")),
        name="bias_tables",
    )(rel_bias.astype(F32), buckets)


def _norm_rows(x_ref, g_ref, h_scr, tm):
    def body(c, carry):
        r = pl.ds(c * 128, 128)
        x = x_ref[r, :]
        ms = jnp.mean(x * x, axis=-1, keepdims=True)
        h_scr[r, :] = (x * lax.rsqrt(ms + EPS) * g_ref[...]).astype(BF16)
        return carry
    lax.fori_loop(0, tm // 128, body, 0)


def _even_in_kernel(x_ref, g_ref, w_ref, o_ref, h_scr, *, tm):
    @pl.when(pl.program_id(1) == 0)
    def _():
        _norm_rows(x_ref, g_ref, h_scr, tm)
    o_ref[...] = jnp.dot(h_scr[...], w_ref[...], preferred_element_type=F32).astype(o_ref.dtype)


def _even_in(x2, g, w, layer, *, tm=1024, tn=1280):
    m, d = x2.shape
    n = w.shape[2]
    return pl.pallas_call(
        functools.partial(_even_in_kernel, tm=tm),
        grid=(m // tm, n // tn),
        in_specs=[pl.BlockSpec((tm, d), lambda i, j: (i, 0)),
                  pl.BlockSpec((1, d), lambda i, j: (0, 0)),
                  pl.BlockSpec((None, d, tn), lambda i, j: (layer, 0, j))],
        out_specs=pl.BlockSpec((tm, tn), lambda i, j: (i, j)),
        out_shape=jax.ShapeDtypeStruct((m, n), BF16),
        scratch_shapes=[pltpu.VMEM((tm, d), BF16)],
        compiler_params=_params(("arbitrary", "arbitrary")),
        name="even_in",
    )(x2, g, w)


def _gelu(x):
    return 0.5 * x * (1.0 + lax.erf(x * (1.0 / math.sqrt(2.0))))


def _silu(x):
    return x * jax.nn.sigmoid(x)


def _odd_in_kernel(x_ref, g_ref, wv_ref, wu_ref, wg_ref, vg_ref, ws_ref, bs_ref, y_ref,
                   h_scr, v_scr, ssq_scr, *, tm, tn):
    j = pl.program_id(1)
    nb = C_WIDTH // tn

    @pl.when(j == 0)
    def _():
        _norm_rows(x_ref, g_ref, h_scr, tm)
        ssq_scr[...] = jnp.zeros_like(ssq_scr)

    @pl.when(j < nb)
    def _():
        v = _gelu(jnp.dot(h_scr[...], wv_ref[...], preferred_element_type=F32))
        v_scr[j] = v.astype(v_scr.dtype)
        ssq_scr[...] += jnp.sum(v * v, axis=-1, keepdims=True)

    @pl.when(j >= nb)
    def _():
        jj = j - nb
        h = h_scr[...]
        ug = (_gelu(jnp.dot(h, wu_ref[...], preferred_element_type=F32))
              * _silu(jnp.dot(h, wg_ref[...], preferred_element_type=F32)))
        tril = (lax.broadcasted_iota(jnp.int32, (C_CHUNK, C_CHUNK), 0)
                >= lax.broadcasted_iota(jnp.int32, (C_CHUNK, C_CHUNK), 1))
        groups_per_tile = tn // C_GROUP_DIM
        vg = vg_ref[jj]
        for gg in range(groups_per_tile):
            grp = jj * groups_per_tile + gg
            wt = jnp.where(tril, ws_ref[grp], 0.0).astype(BF16)
            cols = slice(gg * C_GROUP_DIM, (gg + 1) * C_GROUP_DIM)
            for c in range(tm // C_CHUNK):
                rows = slice(c * C_CHUNK, (c + 1) * C_CHUNK)
                inv = lax.rsqrt(ssq_scr[rows, :][:, :1] * (1.0 / C_WIDTH) + EPS)
                vn = (v_scr[jj, rows, cols].astype(F32) * inv * vg[:, cols]).astype(BF16)
                s = jnp.dot(wt, vn, preferred_element_type=F32) + bs_ref[grp]
                y_ref[rows, cols] = (ug[rows, cols] * s).astype(y_ref.dtype)


def _odd_in(x2, g, w, layer, vg, ws, bs, *, tm=1024, tn=512):
    m, d = x2.shape
    nb = C_WIDTH // tn
    return pl.pallas_call(
        functools.partial(_odd_in_kernel, tm=tm, tn=tn),
        grid=(m // tm, 2 * nb),
        in_specs=[pl.BlockSpec((tm, d), lambda i, j: (i, 0)),
                  pl.BlockSpec((1, d), lambda i, j: (0, 0)),
                  pl.BlockSpec((None, d, tn), lambda i, j: (layer, 0, nb + jnp.minimum(j, nb - 1))),
                  pl.BlockSpec((None, d, tn), lambda i, j: (layer, 0, jnp.maximum(j - nb, 0))),
                  pl.BlockSpec((None, d, tn), lambda i, j: (layer, 0, 2 * nb + jnp.maximum(j - nb, 0))),
                  pl.BlockSpec((nb, 1, tn), lambda i, j: (0, 0, 0)),
                  pl.BlockSpec((C_GROUPS, C_CHUNK, C_CHUNK), lambda i, j: (0, 0, 0)),
                  pl.BlockSpec((C_GROUPS, C_CHUNK, 1), lambda i, j: (0, 0, 0))],
        out_specs=pl.BlockSpec((tm, tn), lambda i, j: (i, jnp.maximum(j - nb, 0))),
        out_shape=jax.ShapeDtypeStruct((m, C_WIDTH), BF16),
        scratch_shapes=[pltpu.VMEM((tm, d), BF16),
                        pltpu.VMEM((nb, tm, tn), BF16),
                        pltpu.VMEM((tm, LANES), F32)],
        compiler_params=_params(("arbitrary", "arbitrary")),
        name="odd_in",
    )(x2, g, w, w, w, vg.reshape(nb, 1, tn), ws, bs.reshape(C_GROUPS, C_CHUNK, 1))


PAD = BLK
UNROLL = 4


def _and(b, mask):
    return b & mask if isinstance(b, int) else lax.bitwise_and(b, mask)


def _shr(b, s):
    return b >> s if isinstance(b, int) else lax.shift_right_logical(b, s)


def _lane_lo(rows):
    return lax.broadcasted_iota(jnp.int32, (rows, LANES), 1) < HEAD_DIM


def _head_rms(x, gain):
    w = x.shape[1]
    r = lax.broadcasted_iota(jnp.int32, (w, w), 0) // HEAD_DIM
    c = lax.broadcasted_iota(jnp.int32, (w, w), 1) // HEAD_DIM
    ones_bd = (r == c).astype(BF16)
    x2 = x * x
    hi = x2.astype(BF16)
    lo = (x2 - hi.astype(F32)).astype(BF16)
    ssq = (jnp.dot(hi, ones_bd, preferred_element_type=F32)
           + jnp.dot(lo, ones_bd, preferred_element_type=F32))
    return x * lax.rsqrt(ssq * (1.0 / HEAD_DIM) + EPS) * gain


def _scores_softmax(qm, kk, bias):
    s = lax.dot_general(qm, kk, (((1,), (1,)), ((), ())), preferred_element_type=F32) + bias
    m = jnp.max(s, axis=-1, keepdims=True)
    return m, jnp.exp(s - m).astype(BF16)


def _pipelined(n_groups, stage1, stage2):
    stage1(0)

    def body(g, carry):
        stage2(g - 1)
        stage1(g)
        return carry
    lax.fori_loop(1, n_groups, body, 0)
    stage2(n_groups - 1)


def _attn_a_kernel(q0_ref, q1_ref, q2_ref, q3_ref, k_ref, v_ref, g0_ref, g1_ref, g2_ref, g3_ref,
                   gq_ref, gk_ref, snk_ref, bias_ref, o_ref, qb, kb, v0b, v1b, p_scr, m_scr, *, seq):
    q_refs = (q0_ref, q1_ref, q2_ref, q3_ref)
    g_refs = (g0_ref, g1_ref, g2_ref, g3_ref)
    npair = len(q_refs)
    nblk = seq // BLK
    lane = lax.broadcasted_iota(jnp.int32, (BLK, LANES), 1)
    keep = (lane // HEAD_DIM) == pl.program_id(1)
    lo = lane < HEAD_DIM
    hi = jnp.logical_not(lo)
    zeros = jnp.zeros((PAD, LANES), BF16)
    kb[pl.ds(0, PAD), :] = zeros
    v0b[pl.ds(0, PAD), :] = zeros
    v1b[pl.ds(0, PAD), :] = zeros
    gq = gq_ref[...] * SCALE
    gq = jnp.concatenate([gq, gq], axis=1)

    def prep(c, carry):
        r = pl.ds(c * BLK, BLK)
        ro = pl.ds(PAD + c * BLK, BLK)
        for pp in range(0, npair, 2):
            qq = jnp.concatenate([q_refs[pp][r, :], q_refs[pp + 1][r, :]], axis=1).astype(F32)
            qn = _head_rms(qq, gq).astype(BF16)
            qb[pp, r, :] = qn[:, :LANES]
            qb[pp + 1, r, :] = qn[:, LANES:]
        kn = _head_rms(k_ref[r, :].astype(F32), gk_ref[...])
        kb[ro, :] = jnp.where(keep, kn, pltpu.roll(kn, HEAD_DIM, 1)).astype(BF16)
        v = v_ref[r, :].astype(F32)
        vd = jnp.where(keep, v, pltpu.roll(v, HEAD_DIM, 1))
        v0b[ro, :] = jnp.where(lo, vd, 1.0).astype(BF16)
        v1b[ro, :] = jnp.where(lo, 1.0, vd).astype(BF16)
        return carry
    lax.fori_loop(0, nblk, prep, 0, unroll=2)

    def stage1(n):
        slot = _and(n, 1)
        first = jnp.where(n == 0, 1, 0)
        q_rows = pl.ds(n * BLK, BLK)
        kk = kb[pl.ds(n * BLK, 2 * BLK), :]
        for pp in range(npair):
            q = qb[pp, q_rows, :]
            ms = []
            for hh, sel in enumerate((lo, hi)):
                m, p = _scores_softmax(jnp.where(sel, q, jnp.zeros_like(q)), kk, bias_ref[2 * pp + hh, first])
                p_scr[slot, pp, hh] = p
                ms.append(jnp.broadcast_to(m, (BLK, LANES)))
            m_scr[slot, pp] = jnp.where(lo, ms[0], ms[1])

    def stage2(n):
        slot = _and(n, 1)
        r = pl.ds(n * BLK, BLK)
        rk = pl.ds(n * BLK, 2 * BLK)
        for pp in range(npair):
            ul0 = jnp.dot(p_scr[slot, pp, 0], v0b[rk, :], preferred_element_type=F32)
            ul1 = jnp.dot(p_scr[slot, pp, 1], v1b[rk, :], preferred_element_type=F32)
            u = jnp.where(lo, ul0, ul1)
            l = pltpu.roll(jnp.where(lo, ul1, ul0), HEAD_DIM, 1)
            m = m_scr[slot, pp]
            snk = snk_ref[pp:pp + 1, :]
            mx = jnp.maximum(m, snk)
            w = jnp.exp(m - mx)
            o = u * (w / (l * w + jnp.exp(snk - mx)))
            o_ref[r, pl.ds(pp * LANES, LANES)] = (o * _silu(g_refs[pp][r, :].astype(F32))).astype(o_ref.dtype)

    _pipelined(nblk, stage1, stage2)


def _attn_a(z3, gq2, gk2, snk3, tabs):
    bsz, seq, _ = z3.shape
    npair = A_HEADS // A_KV_HEADS // 2

    def col(off, pp):
        return pl.BlockSpec((None, seq, LANES), lambda b, gi: (b, 0, off + npair * gi + pp))

    def fixed(off):
        return pl.BlockSpec((None, seq, LANES), lambda b, gi: (b, 0, off))

    return pl.pallas_call(
        functools.partial(_attn_a_kernel, seq=seq),
        grid=(bsz, A_KV_HEADS),
        in_specs=[col(_QA, 0), col(_QA, 1), col(_QA, 2), col(_QA, 3), fixed(_KA), fixed(_VA),
                  col(_GA, 0), col(_GA, 1), col(_GA, 2), col(_GA, 3),
                  pl.BlockSpec((1, LANES), lambda b, gi: (0, 0)),
                  pl.BlockSpec((1, LANES), lambda b, gi: (0, 0)),
                  pl.BlockSpec((None, npair, LANES), lambda b, gi: (gi, 0, 0)),
                  pl.BlockSpec((None, 2 * npair, 2, BLK, 2 * BLK), lambda b, gi: (N_B_PATTERNS, gi, 0, 0, 0))],
        out_specs=pl.BlockSpec((None, seq, npair * LANES), lambda b, gi: (b, 0, gi)),
        out_shape=jax.ShapeDtypeStruct((bsz, seq, A_WIDTH), BF16),
        scratch_shapes=[pltpu.VMEM((npair, seq, LANES), BF16),
                        pltpu.VMEM((PAD + seq, LANES), BF16),
                        pltpu.VMEM((PAD + seq, LANES), BF16),
                        pltpu.VMEM((PAD + seq, LANES), BF16),
                        pltpu.VMEM((2, npair, 2, BLK, 2 * BLK), BF16),
                        pltpu.VMEM((2, npair, BLK, LANES), F32)],
        compiler_params=_params(("arbitrary", "arbitrary")),
        name="attn_a",
    )(z3, z3, z3, z3, z3, z3, z3, z3, z3, z3, gq2, gk2, snk3, tabs)


def _attn_b_kernel(q_ref, k_ref, v_ref, g_ref, gq_ref, gk_ref, bias_ref, o_ref,
                   qf, kf, vf, q4f, k4f, v4f, qb, kb, v0b, v1b, p_scr, m_scr, l_scr, u_scr, *, seq):
    lo = _lane_lo(BLK)
    hi = jnp.logical_not(lo)
    nblk = seq // BLK
    zeros = jnp.zeros((PAD, LANES), BF16)
    for pi in range(2):
        kb[pi, pl.ds(0, PAD), :] = zeros
        v0b[pi, pl.ds(0, PAD), :] = zeros
        v1b[pi, pl.ds(0, PAD), :] = zeros

    def put(pi, dst, q, k, v):
        qb[pi, dst, :] = q.astype(BF16)
        kb[pi, dst, :] = k.astype(BF16)
        v0b[pi, dst, :] = jnp.where(lo, v, 1.0).astype(BF16)
        v1b[pi, dst, :] = jnp.where(lo, 1.0, v).astype(BF16)

    def strided(t):
        return pl.ds(_and(t, 3) * (4 * BLK) + _shr(t, 2), BLK, stride=4)

    gains = jnp.concatenate([gq_ref[...] * SCALE, gk_ref[...]], axis=1)

    def prep(c, carry):
        r = pl.ds(c * BLK, BLK)
        qk = jnp.concatenate([q_ref[r, :], k_ref[r, :]], axis=1).astype(F32)
        n = _head_rms(qk, gains)
        q, k, v = n[:, :LANES], n[:, LANES:], v_ref[r, :].astype(F32)
        qf[r, :] = q
        kf[r, :] = k
        vf[r, :] = v
        put(0, pl.ds(PAD + c * BLK, BLK), q, k, v)
        return carry
    lax.fori_loop(0, nblk, prep, 0, unroll=2)

    def deint4(t, carry):
        src = strided(t)
        dst = pl.ds(t * BLK, BLK)
        q, k, v = qf[src, :], kf[src, :], vf[src, :]
        q4f[dst, :] = q
        k4f[dst, :] = k
        v4f[dst, :] = v
        put(1, pl.ds(PAD + t * BLK, BLK), q, k, v)
        return carry
    lax.fori_loop(0, nblk, deint4, 0)

    def deint16(t, carry):
        src = strided(t)
        put(2, pl.ds(PAD + t * BLK, BLK), q4f[src, :], k4f[src, :], v4f[src, :])
        return carry
    lax.fori_loop(0, nblk, deint16, 0)

    for pi in range(N_B_PATTERNS):
        single = pi == 2
        keys = BLK if single else 2 * BLK
        koff = PAD if single else PAD - BLK

        def out_rows(b, pi=pi):
            return pl.ds(b * BLK, BLK) if pi == 0 else strided(b)

        def biases(b, pi=pi, single=single):
            if single:
                return [bias_ref[pi, hh, 0, :, pl.ds(BLK, BLK)] for hh in range(2)]
            first = jnp.where(_and(b, nblk - 1 if pi == 0 else 3) == 0, 1, 0)
            return [bias_ref[pi, hh, first] for hh in range(2)]

        def stage1(g, pi=pi, keys=keys, koff=koff, out_rows=out_rows, biases=biases):
            for uu in range(UNROLL):
                b = g * UNROLL + uu
                q = qb[pi, pl.ds(PAD + b * BLK, BLK), :]
                kk = kb[pi, pl.ds(koff + b * BLK, keys), :]
                bias = biases(b)
                ms = []
                for hh, sel in enumerate((lo, hi)):
                    m, p = _scores_softmax(jnp.where(sel, q, jnp.zeros_like(q)), kk, bias[hh])
                    p_scr[b, hh, :, pl.ds(0, keys)] = p
                    ms.append(jnp.broadcast_to(m, (BLK, LANES)))
                m_scr[pi, out_rows(b), :] = jnp.where(lo, ms[0], ms[1])

        def stage2(g, pi=pi, keys=keys, koff=koff, out_rows=out_rows):
            for uu in range(UNROLL):
                b = g * UNROLL + uu
                rk = pl.ds(koff + b * BLK, keys)
                ul0 = jnp.dot(p_scr[b, 0, :, pl.ds(0, keys)], v0b[pi, rk, :], preferred_element_type=F32)
                ul1 = jnp.dot(p_scr[b, 1, :, pl.ds(0, keys)], v1b[pi, rk, :], preferred_element_type=F32)
                u_scr[pi, out_rows(b), :] = jnp.where(lo, ul0, ul1)
                l_scr[pi, out_rows(b), :] = pltpu.roll(jnp.where(lo, ul1, ul0), HEAD_DIM, 1)

        _pipelined(nblk // UNROLL, stage1, stage2)

    def renat(t, carry):
        for ref in (m_scr, l_scr, u_scr):
            ref[3, strided(t), :] = ref[2, pl.ds(t * BLK, BLK), :]
        return carry
    lax.fori_loop(0, nblk, renat, 0)

    def combine(c, carry):
        r = pl.ds(c * BLK, BLK)
        slots = (0, 1, 3)
        ms = [m_scr[s, r, :] for s in slots]
        mx = jnp.maximum(jnp.maximum(ms[0], ms[1]), ms[2])
        num = jnp.zeros((BLK, LANES), F32)
        den = jnp.zeros((BLK, LANES), F32)
        for s, m in zip(slots, ms):
            w = jnp.exp(m - mx)
            num = num + w * u_scr[s, r, :]
            den = den + w * l_scr[s, r, :]
        o_ref[r, :] = ((num / den) * _silu(g_ref[r, :].astype(F32))).astype(o_ref.dtype)
        return carry
    lax.fori_loop(0, nblk, combine, 0, unroll=2)


def _attn_b(z3, gq2, gk2, tabs):
    bsz, seq, _ = z3.shape
    col = lambda off: pl.BlockSpec((None, seq, LANES), lambda b, p: (b, 0, off + p))
    f32_rows = pltpu.VMEM((seq, LANES), F32)
    bf16_ops = pltpu.VMEM((N_B_PATTERNS, PAD + seq, LANES), BF16)
    stats = pltpu.VMEM((N_B_PATTERNS + 1, seq, LANES), F32)
    return pl.pallas_call(
        functools.partial(_attn_b_kernel, seq=seq),
        grid=(bsz, B_HEADS // 2),
        in_specs=[col(_QB), col(_KB), col(_VB), col(_GB),
                  pl.BlockSpec((1, LANES), lambda b, p: (0, 0)),
                  pl.BlockSpec((1, LANES), lambda b, p: (0, 0)),
                  pl.BlockSpec((N_B_PATTERNS, 2, 2, BLK, 2 * BLK), lambda b, p: (0, p, 0, 0, 0))],
        out_specs=pl.BlockSpec((None, seq, LANES), lambda b, p: (b, 0, p)),
        out_shape=jax.ShapeDtypeStruct((bsz, seq, B_WIDTH), BF16),
        scratch_shapes=[f32_rows] * 6 + [bf16_ops] * 4
                       + [pltpu.VMEM((seq // BLK, 2, BLK, 2 * BLK), BF16)] + [stats] * 3,
        compiler_params=_params(("arbitrary", "arbitrary")),
        name="attn_b",
    )(z3, z3, z3, z3, gq2, gk2, tabs)


def _even_out_kernel(ya_ref, yb_ref, wa_ref, wb_ref, x_ref, o_ref):
    acc = jnp.dot(ya_ref[...], wa_ref[...], preferred_element_type=F32)
    acc = acc + jnp.dot(yb_ref[...], wb_ref[...], preferred_element_type=F32)
    o_ref[...] = x_ref[...] + acc


def _even_out(ya, yb, w, layer, x2, *, tm=2048, tn=512):
    m, d = x2.shape
    ka, kb = ya.shape[1], yb.shape[1]
    assert ka == kb
    return pl.pallas_call(
        _even_out_kernel,
        grid=(m // tm, d // tn),
        in_specs=[pl.BlockSpec((tm, ka), lambda i, j: (i, 0)),
                  pl.BlockSpec((tm, kb), lambda i, j: (i, 0)),
                  pl.BlockSpec((None, ka, tn), lambda i, j: (layer, 0, j)),
                  pl.BlockSpec((None, kb, tn), lambda i, j: (layer, 1, j)),
                  pl.BlockSpec((tm, tn), lambda i, j: (i, j))],
        out_specs=pl.BlockSpec((tm, tn), lambda i, j: (i, j)),
        out_shape=jax.ShapeDtypeStruct((m, d), F32),
        compiler_params=_params(("arbitrary", "arbitrary")),
        name="even_out",
    )(ya, yb, w, w, x2)


def _odd_out_kernel(y_ref, w_ref, x_ref, o_ref):
    o_ref[...] = x_ref[...] + jnp.dot(y_ref[...], w_ref[...], preferred_element_type=F32)


def _odd_out(y, w, layer, x2, *, tm=1024, tn=512):
    m, d = x2.shape
    k = y.shape[1]
    return pl.pallas_call(
        _odd_out_kernel,
        grid=(m // tm, d // tn),
        in_specs=[pl.BlockSpec((tm, k), lambda i, j: (i, 0)),
                  pl.BlockSpec((None, k, tn), lambda i, j: (layer, 0, j)),
                  pl.BlockSpec((tm, tn), lambda i, j: (i, j))],
        out_specs=pl.BlockSpec((tm, tn), lambda i, j: (i, j)),
        out_shape=jax.ShapeDtypeStruct((m, d), F32),
        compiler_params=_params(("arbitrary", "arbitrary")),
        name="odd_out",
    )(y, w, x2)


def kernel(x, ev_ln_g, ev_w_in, ev_qk_g, ev_sinks, ev_w_out, od_ln_g, od_w_in, od_v_g, od_w_s,
           od_b_s, od_w_out, rel_bias):
    bsz, seq, d = x.shape
    depth = ev_ln_g.shape[0] + od_ln_g.shape[0]
    tabs = _bias_tables(rel_bias)
    ev_w_in, ev_w_out, od_w_in, od_w_out = (w.astype(BF16) for w in (ev_w_in, ev_w_out, od_w_in, od_w_out))
    x2 = x.reshape(bsz * seq, d)
    for i in range(depth):
        j = i // 2
        if i % 2 == 0:
            z = _even_in(x2, ev_ln_g[j].reshape(1, d), ev_w_in, j)
            z3 = z.reshape(bsz, seq, EVEN_IN)
            gains = jnp.tile(ev_qk_g[j].astype(F32), (1, 2))
            snk3 = jnp.repeat(ev_sinks[j].astype(F32), HEAD_DIM).reshape(A_KV_HEADS, -1, LANES)
            ya = _attn_a(z3, gains[0:1], gains[1:2], snk3, tabs)
            yb = _attn_b(z3, gains[2:3], gains[3:4], tabs)
            x2 = _even_out(ya.reshape(bsz * seq, A_WIDTH), yb.reshape(bsz * seq, B_WIDTH),
                           ev_w_out, j, x2)
        else:
            y = _odd_in(x2, od_ln_g[j].reshape(1, d), od_w_in, j, od_v_g[j].astype(F32),
                        od_w_s[j].astype(F32), od_b_s[j].astype(F32))
            x2 = _odd_out(y, od_w_out, j, x2)
    return x2.reshape(bsz, seq, d)
```

```python
import functools
import math
from typing import NamedTuple

import numpy as np
import jax
import jax.numpy as jnp
from jax import lax
from jax.experimental import pallas as pl
from jax.experimental.pallas import tpu as pltpu

F32 = jnp.float32
BF16 = jnp.bfloat16

D_MODEL = 2048
HEAD_DIM = 64
A_HEADS = 16
A_KV_HEADS = 2
B_HEADS = 16
BLK = 128
NUM_BUCKETS = 32
REL_MAX_DISTANCE = 2048
A_WIDTH = A_HEADS * HEAD_DIM
B_WIDTH = B_HEADS * HEAD_DIM
EVEN_IN = 6400
C_WIDTH = 2 * D_MODEL
C_GROUPS = 16
C_GROUP_DIM = C_WIDTH // C_GROUPS
C_CHUNK = 128
EPS = 1e-6
NEG = -1e30
SCALE = HEAD_DIM ** -0.5
PATTERNS = ((1, 128), (4, 128), (16, 128), (1, 127))
N_B_PATTERNS = 3

LANES = 128
VMEM_LIMIT = 56 * 1024 * 1024

_QA, _KA, _VA, _GA = 0, 8, 9, 10
_QB, _KB, _VB, _GB = 18, 26, 34, 42


def _params(sem):
    return pltpu.CompilerParams(dimension_semantics=sem, vmem_limit_bytes=VMEM_LIMIT)


class _Cast(NamedTuple):
    src: jax.Array
    first_row: int
    n_rows: int


def _call_with_casts(body, *, grid, in_specs, out_specs, out_shape, scratch_shapes, name, args, casts=()):
    n_in, n_cast = len(in_specs), len(casts)
    steps = grid[0] * grid[1]
    cast_in, cast_out, cast_shape = [], [], []
    for c in casts:
        cols = c.src.shape[1]
        tile = c.n_rows // steps
        first = c.first_row // tile
        assert tile * steps == c.n_rows and first * tile == c.first_row and tile % 16 == 0
        cast_in.append(pl.BlockSpec((tile, cols), lambda i, j, first=first: (first + i * grid[1] + j, 0)))
        cast_out.append(pl.BlockSpec((tile, cols), lambda i, j: (i * grid[1] + j, 0)))
        cast_shape.append(jax.ShapeDtypeStruct((c.n_rows, cols), BF16))

    def kernel(*refs):
        ins, srcs = refs[:n_in], refs[n_in:n_in + n_cast]
        out, dsts = refs[n_in + n_cast], refs[n_in + n_cast + 1:n_in + 2 * n_cast + 1]
        for src, dst in zip(srcs, dsts):
            dst[...] = src[...].astype(dst.dtype)
        body(*ins, out, *refs[n_in + 2 * n_cast + 1:])

    res = pl.pallas_call(
        kernel,
        grid=grid,
        in_specs=list(in_specs) + cast_in,
        out_specs=[out_specs] + cast_out,
        out_shape=[out_shape] + cast_shape,
        scratch_shapes=scratch_shapes,
        compiler_params=_params(("arbitrary", "arbitrary")),
        name=name,
    )(*args, *[c.src for c in casts])
    return res[0], list(res[1:])


def _bucket_tables():
    a = np.arange(BLK)[:, None]
    b = np.arange(2 * BLK)[None, :]
    dist = BLK + a - b
    max_exact = NUM_BUCKETS // 2
    out = []
    for dil, max_dist in PATTERNS:
        n = np.maximum(dist * dil, 0)
        large = max_exact + (np.log(np.maximum(n, 1) / max_exact)
                             / np.log(REL_MAX_DISTANCE / max_exact)
                             * (NUM_BUCKETS - max_exact)).astype(np.int32)
        large = np.minimum(large, NUM_BUCKETS - 1)
        bucket = np.where(n < max_exact, n, large).astype(np.int32)
        valid = (dist >= 0) & (dist <= max_dist)
        out.append(np.where(valid, bucket, -1).astype(np.int32))
    return np.stack(out)


def _bias_kernel(tbl_ref, bucket_ref, out_ref):
    t = pl.program_id(0)
    h = pl.program_id(1)
    col = h + jnp.where(t < N_B_PATTERNS, A_HEADS, 0)
    bk = bucket_ref[...]
    acc = jnp.full(bk.shape, NEG, F32)
    for b in range(NUM_BUCKETS):
        acc = jnp.where(bk == b, tbl_ref[b, col], acc)
    out_ref[0] = acc
    prev_cols = lax.broadcasted_iota(jnp.int32, bk.shape, 1) < BLK
    out_ref[1] = jnp.where(prev_cols, NEG, acc)


def _bias_tables(rel_bias):
    buckets = jnp.asarray(_bucket_tables())
    return pl.pallas_call(
        _bias_kernel,
        grid=(len(PATTERNS), 16),
        in_specs=[pl.BlockSpec(memory_space=pltpu.SMEM),
                  pl.BlockSpec((None, BLK, 2 * BLK), lambda t, h: (t, 0, 0))],
        out_specs=pl.BlockSpec((None, None, 2, BLK, 2 * BLK), lambda t, h: (t, h, 0, 0, 0)),
        out_shape=jax.ShapeDtypeStruct((len(PATTERNS), 16, 2, BLK, 2 * BLK), F32),
        compiler_params=_params(("arbitrary", "arbitrary")),
        name="bias_tables",
    )(rel_bias.astype(F32), buckets)


def _norm_rows(x_ref, g_ref, h_scr, tm):
    def body(c, carry):
        r = pl.ds(c * 128, 128)
        x = x_ref[r, :]
        ms = jnp.mean(x * x, axis=-1, keepdims=True)
        h_scr[r, :] = (x * lax.rsqrt(ms + EPS) * g_ref[...]).astype(BF16)
        return carry
    lax.fori_loop(0, tm // 128, body, 0)


def _even_in_kernel(x_ref, g_ref, w_ref, o_ref, h_scr, *, tm):
    @pl.when(pl.program_id(1) == 0)
    def _():
        _norm_rows(x_ref, g_ref, h_scr, tm)
    o_ref[...] = jnp.dot(h_scr[...], w_ref[...], preferred_element_type=F32).astype(o_ref.dtype)


def _even_in(x2, g, w, layer, *, tm=1024, tn=1280):
    m, d = x2.shape
    n = w.shape[2]
    return pl.pallas_call(
        functools.partial(_even_in_kernel, tm=tm),
        grid=(m // tm, n // tn),
        in_specs=[pl.BlockSpec((tm, d), lambda i, j: (i, 0)),
                  pl.BlockSpec((1, d), lambda i, j: (0, 0)),
                  pl.BlockSpec((None, d, tn), lambda i, j: (layer, 0, j))],
        out_specs=pl.BlockSpec((tm, tn), lambda i, j: (i, j)),
        out_shape=jax.ShapeDtypeStruct((m, n), BF16),
        scratch_shapes=[pltpu.VMEM((tm, d), BF16)],
        compiler_params=_params(("arbitrary", "arbitrary")),
        name="even_in",
    )(x2, g, w)


def _gelu(x):
    return 0.5 * x * (1.0 + lax.erf(x * (1.0 / math.sqrt(2.0))))


def _silu(x):
    return x * jax.nn.sigmoid(x)


def _odd_in_kernel(x_ref, g_ref, wv_ref, wu_ref, wg_ref, vg_ref, ws_ref, bs_ref, y_ref,
                   h_scr, v_scr, ssq_scr, *, tm, tn):
    j = pl.program_id(1)
    nb = C_WIDTH // tn

    @pl.when(j == 0)
    def _():
        _norm_rows(x_ref, g_ref, h_scr, tm)
        ssq_scr[...] = jnp.zeros_like(ssq_scr)

    @pl.when(j < nb)
    def _():
        v = _gelu(jnp.dot(h_scr[...], wv_ref[...], preferred_element_type=F32))
        v_scr[j] = v.astype(v_scr.dtype)
        ssq_scr[...] += jnp.sum(v * v, axis=-1, keepdims=True)

    @pl.when(j >= nb)
    def _():
        jj = j - nb
        h = h_scr[...]
        ug = (_gelu(jnp.dot(h, wu_ref[...], preferred_element_type=F32))
              * _silu(jnp.dot(h, wg_ref[...], preferred_element_type=F32)))
        tril = (lax.broadcasted_iota(jnp.int32, (C_CHUNK, C_CHUNK), 0)
                >= lax.broadcasted_iota(jnp.int32, (C_CHUNK, C_CHUNK), 1))
        groups_per_tile = tn // C_GROUP_DIM
        vg = vg_ref[jj]
        for gg in range(groups_per_tile):
            grp = jj * groups_per_tile + gg
            wt = jnp.where(tril, ws_ref[grp], 0.0).astype(BF16)
            cols = slice(gg * C_GROUP_DIM, (gg + 1) * C_GROUP_DIM)
            for c in range(tm // C_CHUNK):
                rows = slice(c * C_CHUNK, (c + 1) * C_CHUNK)
                inv = lax.rsqrt(ssq_scr[rows, :][:, :1] * (1.0 / C_WIDTH) + EPS)
                vn = (v_scr[jj, rows, cols].astype(F32) * inv * vg[:, cols]).astype(BF16)
                s = jnp.dot(wt, vn, preferred_element_type=F32) + bs_ref[grp]
                y_ref[rows, cols] = (ug[rows, cols] * s).astype(y_ref.dtype)


def _odd_in(x2, g, w, layer, vg, ws, bs, casts=(), *, tm=1024, tn=512):
    m, d = x2.shape
    nb = C_WIDTH // tn
    return _call_with_casts(
        functools.partial(_odd_in_kernel, tm=tm, tn=tn),
        grid=(m // tm, 2 * nb),
        in_specs=[pl.BlockSpec((tm, d), lambda i, j: (i, 0)),
                  pl.BlockSpec((1, d), lambda i, j: (0, 0)),
                  pl.BlockSpec((None, d, tn), lambda i, j: (layer, 0, nb + jnp.minimum(j, nb - 1))),
                  pl.BlockSpec((None, d, tn), lambda i, j: (layer, 0, jnp.maximum(j - nb, 0))),
                  pl.BlockSpec((None, d, tn), lambda i, j: (layer, 0, 2 * nb + jnp.maximum(j - nb, 0))),
                  pl.BlockSpec((nb, 1, tn), lambda i, j: (0, 0, 0)),
                  pl.BlockSpec((C_GROUPS, C_CHUNK, C_CHUNK), lambda i, j: (0, 0, 0)),
                  pl.BlockSpec((C_GROUPS, C_CHUNK, 1), lambda i, j: (0, 0, 0))],
        out_specs=pl.BlockSpec((tm, tn), lambda i, j: (i, jnp.maximum(j - nb, 0))),
        out_shape=jax.ShapeDtypeStruct((m, C_WIDTH), BF16),
        scratch_shapes=[pltpu.VMEM((tm, d), BF16),
                        pltpu.VMEM((nb, tm, tn), BF16),
                        pltpu.VMEM((tm, LANES), F32)],
        name="odd_in",
        args=(x2, g, w, w, w, vg.reshape(nb, 1, tn), ws, bs.reshape(C_GROUPS, C_CHUNK, 1)),
        casts=casts)


PAD = BLK
UNROLL = 4


def _and(b, mask):
    return b & mask if isinstance(b, int) else lax.bitwise_and(b, mask)


def _shr(b, s):
    return b >> s if isinstance(b, int) else lax.shift_right_logical(b, s)


def _lane_lo(rows):
    return lax.broadcasted_iota(jnp.int32, (rows, LANES), 1) < HEAD_DIM


def _head_rms(x, gain):
    w = x.shape[1]
    r = lax.broadcasted_iota(jnp.int32, (w, w), 0) // HEAD_DIM
    c = lax.broadcasted_iota(jnp.int32, (w, w), 1) // HEAD_DIM
    ones_bd = (r == c).astype(BF16)
    x2 = x * x
    hi = x2.astype(BF16)
    lo = (x2 - hi.astype(F32)).astype(BF16)
    ssq = (jnp.dot(hi, ones_bd, preferred_element_type=F32)
           + jnp.dot(lo, ones_bd, preferred_element_type=F32))
    return x * lax.rsqrt(ssq * (1.0 / HEAD_DIM) + EPS) * gain


def _scores_softmax(qm, kk, bias):
    s = lax.dot_general(qm, kk, (((1,), (1,)), ((), ())), preferred_element_type=F32) + bias
    m = jnp.max(s, axis=-1, keepdims=True)
    return m, jnp.exp(s - m).astype(BF16)


def _pipelined(n_groups, stage1, stage2):
    stage1(0)

    def body(g, carry):
        stage2(g - 1)
        stage1(g)
        return carry
    lax.fori_loop(1, n_groups, body, 0)
    stage2(n_groups - 1)


def _attn_a_kernel(q0_ref, q1_ref, q2_ref, q3_ref, k_ref, v_ref, g0_ref, g1_ref, g2_ref, g3_ref,
                   gq_ref, gk_ref, snk_ref, bias_ref, o_ref, qb, kb, v0b, v1b, p_scr, m_scr, *, seq):
    q_refs = (q0_ref, q1_ref, q2_ref, q3_ref)
    g_refs = (g0_ref, g1_ref, g2_ref, g3_ref)
    npair = len(q_refs)
    nblk = seq // BLK
    lane = lax.broadcasted_iota(jnp.int32, (BLK, LANES), 1)
    keep = (lane // HEAD_DIM) == pl.program_id(1)
    lo = lane < HEAD_DIM
    hi = jnp.logical_not(lo)
    zeros = jnp.zeros((PAD, LANES), BF16)
    kb[pl.ds(0, PAD), :] = zeros
    v0b[pl.ds(0, PAD), :] = zeros
    v1b[pl.ds(0, PAD), :] = zeros
    gq = gq_ref[...] * SCALE
    gq = jnp.concatenate([gq, gq], axis=1)

    def prep(c, carry):
        r = pl.ds(c * BLK, BLK)
        ro = pl.ds(PAD + c * BLK, BLK)
        for pp in range(0, npair, 2):
            qq = jnp.concatenate([q_refs[pp][r, :], q_refs[pp + 1][r, :]], axis=1).astype(F32)
            qn = _head_rms(qq, gq).astype(BF16)
            qb[pp, r, :] = qn[:, :LANES]
            qb[pp + 1, r, :] = qn[:, LANES:]
        kn = _head_rms(k_ref[r, :].astype(F32), gk_ref[...])
        kb[ro, :] = jnp.where(keep, kn, pltpu.roll(kn, HEAD_DIM, 1)).astype(BF16)
        v = v_ref[r, :].astype(F32)
        vd = jnp.where(keep, v, pltpu.roll(v, HEAD_DIM, 1))
        v0b[ro, :] = jnp.where(lo, vd, 1.0).astype(BF16)
        v1b[ro, :] = jnp.where(lo, 1.0, vd).astype(BF16)
        return carry
    lax.fori_loop(0, nblk, prep, 0, unroll=4)

    def stage1(n):
        slot = _and(n, 1)
        first = jnp.where(n == 0, 1, 0)
        q_rows = pl.ds(n * BLK, BLK)
        kk = kb[pl.ds(n * BLK, 2 * BLK), :]
        for pp in range(npair):
            q = qb[pp, q_rows, :]
            ms = []
            for hh, sel in enumerate((lo, hi)):
                m, p = _scores_softmax(jnp.where(sel, q, jnp.zeros_like(q)), kk, bias_ref[2 * pp + hh, first])
                p_scr[slot, pp, hh] = p
                ms.append(jnp.broadcast_to(m, (BLK, LANES)))
            m_scr[slot, pp] = jnp.where(lo, ms[0], ms[1])

    def stage2(n):
        slot = _and(n, 1)
        r = pl.ds(n * BLK, BLK)
        rk = pl.ds(n * BLK, 2 * BLK)
        for pp in range(npair):
            ul0 = jnp.dot(p_scr[slot, pp, 0], v0b[rk, :], preferred_element_type=F32)
            ul1 = jnp.dot(p_scr[slot, pp, 1], v1b[rk, :], preferred_element_type=F32)
            u = jnp.where(lo, ul0, ul1)
            l = pltpu.roll(jnp.where(lo, ul1, ul0), HEAD_DIM, 1)
            m = m_scr[slot, pp]
            snk = snk_ref[pp:pp + 1, :]
            mx = jnp.maximum(m, snk)
            w = jnp.exp(m - mx)
            o = u * (w / (l * w + jnp.exp(snk - mx)))
            o_ref[r, pl.ds(pp * LANES, LANES)] = (o * _silu(g_refs[pp][r, :].astype(F32))).astype(o_ref.dtype)

    _pipelined(nblk, stage1, stage2)


def _attn_a(z3, gq2, gk2, snk3, tabs, casts=()):
    bsz, seq, _ = z3.shape
    npair = A_HEADS // A_KV_HEADS // 2

    def col(off, pp):
        return pl.BlockSpec((None, seq, LANES), lambda b, gi: (b, 0, off + npair * gi + pp))

    def fixed(off):
        return pl.BlockSpec((None, seq, LANES), lambda b, gi: (b, 0, off))

    return _call_with_casts(
        functools.partial(_attn_a_kernel, seq=seq),
        grid=(bsz, A_KV_HEADS),
        in_specs=[col(_QA, 0), col(_QA, 1), col(_QA, 2), col(_QA, 3), fixed(_KA), fixed(_VA),
                  col(_GA, 0), col(_GA, 1), col(_GA, 2), col(_GA, 3),
                  pl.BlockSpec((1, LANES), lambda b, gi: (0, 0)),
                  pl.BlockSpec((1, LANES), lambda b, gi: (0, 0)),
                  pl.BlockSpec((None, npair, LANES), lambda b, gi: (gi, 0, 0)),
                  pl.BlockSpec((None, 2 * npair, 2, BLK, 2 * BLK), lambda b, gi: (N_B_PATTERNS, gi, 0, 0, 0))],
        out_specs=pl.BlockSpec((None, seq, npair * LANES), lambda b, gi: (b, 0, gi)),
        out_shape=jax.ShapeDtypeStruct((bsz, seq, A_WIDTH), BF16),
        scratch_shapes=[pltpu.VMEM((npair, seq, LANES), BF16),
                        pltpu.VMEM((PAD + seq, LANES), BF16),
                        pltpu.VMEM((PAD + seq, LANES), BF16),
                        pltpu.VMEM((PAD + seq, LANES), BF16),
                        pltpu.VMEM((2, npair, 2, BLK, 2 * BLK), BF16),
                        pltpu.VMEM((2, npair, BLK, LANES), F32)],
        name="attn_a",
        args=(z3, z3, z3, z3, z3, z3, z3, z3, z3, z3, gq2, gk2, snk3, tabs),
        casts=casts)


def _attn_b_kernel(q_ref, k_ref, v_ref, g_ref, gq_ref, gk_ref, bias_ref, o_ref,
                   qf, kf, vf, q4f, k4f, v4f, qb, kb, v0b, v1b, p_scr, m_scr, l_scr, u_scr, *, seq):
    lo = _lane_lo(BLK)
    hi = jnp.logical_not(lo)
    nblk = seq // BLK
    zeros = jnp.zeros((PAD, LANES), BF16)
    for pi in range(2):
        kb[pi, pl.ds(0, PAD), :] = zeros
        v0b[pi, pl.ds(0, PAD), :] = zeros
        v1b[pi, pl.ds(0, PAD), :] = zeros

    def put(pi, dst, q, k, v):
        qb[pi, dst, :] = q.astype(BF16)
        kb[pi, dst, :] = k.astype(BF16)
        v0b[pi, dst, :] = jnp.where(lo, v, 1.0).astype(BF16)
        v1b[pi, dst, :] = jnp.where(lo, 1.0, v).astype(BF16)

    def strided(t):
        return pl.ds(_and(t, 3) * (4 * BLK) + _shr(t, 2), BLK, stride=4)

    gains = jnp.concatenate([gq_ref[...] * SCALE, gk_ref[...]], axis=1)

    def prep(c, carry):
        r = pl.ds(c * BLK, BLK)
        qk = jnp.concatenate([q_ref[r, :], k_ref[r, :]], axis=1).astype(F32)
        n = _head_rms(qk, gains)
        q, k, v = n[:, :LANES], n[:, LANES:], v_ref[r, :].astype(F32)
        qf[r, :] = q
        kf[r, :] = k
        vf[r, :] = v
        put(0, pl.ds(PAD + c * BLK, BLK), q, k, v)
        return carry
    lax.fori_loop(0, nblk, prep, 0, unroll=4)

    def deint4(t, carry):
        src = strided(t)
        dst = pl.ds(t * BLK, BLK)
        q, k, v = qf[src, :], kf[src, :], vf[src, :]
        q4f[dst, :] = q
        k4f[dst, :] = k
        v4f[dst, :] = v
        put(1, pl.ds(PAD + t * BLK, BLK), q, k, v)
        return carry
    lax.fori_loop(0, nblk, deint4, 0)

    def deint16(t, carry):
        src = strided(t)
        put(2, pl.ds(PAD + t * BLK, BLK), q4f[src, :], k4f[src, :], v4f[src, :])
        return carry
    lax.fori_loop(0, nblk, deint16, 0)

    for pi in range(N_B_PATTERNS):
        single = pi == 2
        keys = BLK if single else 2 * BLK
        koff = PAD if single else PAD - BLK

        def out_rows(b, pi=pi):
            return pl.ds(b * BLK, BLK) if pi == 0 else strided(b)

        def biases(b, pi=pi, single=single):
            if single:
                return [bias_ref[pi, hh, 0, :, pl.ds(BLK, BLK)] for hh in range(2)]
            first = jnp.where(_and(b, nblk - 1 if pi == 0 else 3) == 0, 1, 0)
            return [bias_ref[pi, hh, first] for hh in range(2)]

        def stage1(g, pi=pi, keys=keys, koff=koff, out_rows=out_rows, biases=biases):
            for uu in range(UNROLL):
                b = g * UNROLL + uu
                q = qb[pi, pl.ds(PAD + b * BLK, BLK), :]
                kk = kb[pi, pl.ds(koff + b * BLK, keys), :]
                bias = biases(b)
                ms = []
                for hh, sel in enumerate((lo, hi)):
                    m, p = _scores_softmax(jnp.where(sel, q, jnp.zeros_like(q)), kk, bias[hh])
                    p_scr[b, hh, :, pl.ds(0, keys)] = p
                    ms.append(jnp.broadcast_to(m, (BLK, LANES)))
                m_scr[pi, out_rows(b), :] = jnp.where(lo, ms[0], ms[1])

        def stage2(g, pi=pi, keys=keys, koff=koff, out_rows=out_rows):
            for uu in range(UNROLL):
                b = g * UNROLL + uu
                rk = pl.ds(koff + b * BLK, keys)
                ul0 = jnp.dot(p_scr[b, 0, :, pl.ds(0, keys)], v0b[pi, rk, :], preferred_element_type=F32)
                ul1 = jnp.dot(p_scr[b, 1, :, pl.ds(0, keys)], v1b[pi, rk, :], preferred_element_type=F32)
                u_scr[pi, out_rows(b), :] = jnp.where(lo, ul0, ul1)
                l_scr[pi, out_rows(b), :] = pltpu.roll(jnp.where(lo, ul1, ul0), HEAD_DIM, 1)

        _pipelined(nblk // UNROLL, stage1, stage2)

    def renat(t, carry):
        for ref in (m_scr, l_scr, u_scr):
            ref[3, strided(t), :] = ref[2, pl.ds(t * BLK, BLK), :]
        return carry
    lax.fori_loop(0, nblk, renat, 0)

    def combine(c, carry):
        r = pl.ds(c * BLK, BLK)
        slots = (0, 1, 3)
        ms = [m_scr[s, r, :] for s in slots]
        mx = jnp.maximum(jnp.maximum(ms[0], ms[1]), ms[2])
        num = jnp.zeros((BLK, LANES), F32)
        den = jnp.zeros((BLK, LANES), F32)
        for s, m in zip(slots, ms):
            w = jnp.exp(m - mx)
            num = num + w * u_scr[s, r, :]
            den = den + w * l_scr[s, r, :]
        o_ref[r, :] = ((num / den) * _silu(g_ref[r, :].astype(F32))).astype(o_ref.dtype)
        return carry
    lax.fori_loop(0, nblk, combine, 0, unroll=2)


def _attn_b(z3, gq2, gk2, tabs, casts=()):
    bsz, seq, _ = z3.shape
    col = lambda off: pl.BlockSpec((None, seq, LANES), lambda b, p: (b, 0, off + p))
    f32_rows = pltpu.VMEM((seq, LANES), F32)
    bf16_ops = pltpu.VMEM((N_B_PATTERNS, PAD + seq, LANES), BF16)
    stats = pltpu.VMEM((N_B_PATTERNS + 1, seq, LANES), F32)
    return _call_with_casts(
        functools.partial(_attn_b_kernel, seq=seq),
        grid=(bsz, B_HEADS // 2),
        in_specs=[col(_QB), col(_KB), col(_VB), col(_GB),
                  pl.BlockSpec((1, LANES), lambda b, p: (0, 0)),
                  pl.BlockSpec((1, LANES), lambda b, p: (0, 0)),
                  pl.BlockSpec((N_B_PATTERNS, 2, 2, BLK, 2 * BLK), lambda b, p: (0, p, 0, 0, 0))],
        out_specs=pl.BlockSpec((None, seq, LANES), lambda b, p: (b, 0, p)),
        out_shape=jax.ShapeDtypeStruct((bsz, seq, B_WIDTH), BF16),
        scratch_shapes=[f32_rows] * 6 + [bf16_ops] * 4
                       + [pltpu.VMEM((seq // BLK, 2, BLK, 2 * BLK), BF16)] + [stats] * 3,
        name="attn_b",
        args=(z3, z3, z3, z3, gq2, gk2, tabs),
        casts=casts)


def _even_out_kernel(ya_ref, yb_ref, wa_ref, wb_ref, x_ref, o_ref):
    acc = jnp.dot(ya_ref[...], wa_ref[...], preferred_element_type=F32)
    acc = acc + jnp.dot(yb_ref[...], wb_ref[...], preferred_element_type=F32)
    o_ref[...] = x_ref[...] + acc


def _even_out(ya, yb, w, layer, x2, *, tm=2048, tn=512):
    m, d = x2.shape
    ka, kb = ya.shape[1], yb.shape[1]
    assert ka == kb
    return pl.pallas_call(
        _even_out_kernel,
        grid=(m // tm, d // tn),
        in_specs=[pl.BlockSpec((tm, ka), lambda i, j: (i, 0)),
                  pl.BlockSpec((tm, kb), lambda i, j: (i, 0)),
                  pl.BlockSpec((None, ka, tn), lambda i, j: (layer, 0, j)),
                  pl.BlockSpec((None, kb, tn), lambda i, j: (layer, 1, j)),
                  pl.BlockSpec((tm, tn), lambda i, j: (i, j))],
        out_specs=pl.BlockSpec((tm, tn), lambda i, j: (i, j)),
        out_shape=jax.ShapeDtypeStruct((m, d), F32),
        compiler_params=_params(("arbitrary", "arbitrary")),
        name="even_out",
    )(ya, yb, w, w, x2)


def _odd_out_kernel(y_ref, w_ref, x_ref, o_ref):
    o_ref[...] = x_ref[...] + jnp.dot(y_ref[...], w_ref[...], preferred_element_type=F32)


def _odd_out(y, w, layer, x2, *, tm=1024, tn=512):
    m, d = x2.shape
    k = y.shape[1]
    return pl.pallas_call(
        _odd_out_kernel,
        grid=(m // tm, d // tn),
        in_specs=[pl.BlockSpec((tm, k), lambda i, j: (i, 0)),
                  pl.BlockSpec((None, k, tn), lambda i, j: (layer, 0, j)),
                  pl.BlockSpec((tm, tn), lambda i, j: (i, j))],
        out_specs=pl.BlockSpec((tm, tn), lambda i, j: (i, j)),
        out_shape=jax.ShapeDtypeStruct((m, d), F32),
        compiler_params=_params(("arbitrary", "arbitrary")),
        name="odd_out",
    )(y, w, x2)


def kernel(x, ev_ln_g, ev_w_in, ev_qk_g, ev_sinks, ev_w_out, od_ln_g, od_w_in, od_v_g, od_w_s,
           od_b_s, od_w_out, rel_bias):
    bsz, seq, d = x.shape
    n_even, n_odd = ev_ln_g.shape[0], od_ln_g.shape[0]
    tabs = _bias_tables(rel_bias)
    flat = lambda w: w.reshape(-1, w.shape[-1])
    whole = lambda w: _Cast(flat(w), 0, w.shape[0] * w.shape[1])
    layer_of = lambda w, j: _Cast(flat(w), j * w.shape[1], w.shape[1])
    ev_in_b = {0: ev_w_in[0:1].astype(BF16)}
    od_in_b, ev_out_b, od_out_b = {}, None, None
    x2 = x.reshape(bsz * seq, d)
    for i in range(n_even + n_odd):
        j = i // 2
        if i % 2 == 0:
            z = _even_in(x2, ev_ln_g[j].reshape(1, d), ev_in_b[j], 0)
            z3 = z.reshape(bsz, seq, EVEN_IN)
            gains = jnp.tile(ev_qk_g[j].astype(F32), (1, 2))
            snk3 = jnp.repeat(ev_sinks[j].astype(F32), HEAD_DIM).reshape(A_KV_HEADS, -1, LANES)
            ya, done = _attn_a(z3, gains[0:1], gains[1:2], snk3, tabs, [whole(ev_w_out)] if j == 0 else [])
            if j == 0:
                ev_out_b = done[0].reshape(ev_w_out.shape)
            yb, done = _attn_b(z3, gains[2:3], gains[3:4], tabs, [layer_of(od_w_in, j)] if j < n_odd else [])
            if j < n_odd:
                od_in_b[j] = done[0][None]
            x2 = _even_out(ya.reshape(bsz * seq, A_WIDTH), yb.reshape(bsz * seq, B_WIDTH), ev_out_b, j, x2)
        else:
            casts = ([whole(od_w_out)] if j == 0 else []) + ([layer_of(ev_w_in, j + 1)] if j + 1 < n_even else [])
            y, done = _odd_in(x2, od_ln_g[j].reshape(1, d), od_in_b[j], 0, od_v_g[j].astype(F32),
                              od_w_s[j].astype(F32), od_b_s[j].astype(F32), casts)
            if j == 0:
                od_out_b = done.pop(0).reshape(od_w_out.shape)
            if j + 1 < n_even:
                ev_in_b[j + 1] = done.pop(0)[None]
            x2 = _odd_out(y, od_out_b, j, x2)
    return x2.reshape(bsz, seq, d)
```

```python
import functools
import math
from typing import NamedTuple

import numpy as np
import jax
import jax.numpy as jnp
from jax import lax
from jax.experimental import pallas as pl
from jax.experimental.pallas import tpu as pltpu

F32 = jnp.float32
BF16 = jnp.bfloat16

D_MODEL = 2048
HEAD_DIM = 64
A_HEADS = 16
A_KV_HEADS = 2
B_HEADS = 16
BLK = 128
NUM_BUCKETS = 32
REL_MAX_DISTANCE = 2048
A_WIDTH = A_HEADS * HEAD_DIM
B_WIDTH = B_HEADS * HEAD_DIM
EVEN_IN = 6400
C_WIDTH = 2 * D_MODEL
C_GROUPS = 16
C_GROUP_DIM = C_WIDTH // C_GROUPS
C_CHUNK = 128
EPS = 1e-6
NEG = -1e30
SCALE = HEAD_DIM ** -0.5
PATTERNS = ((1, 128), (4, 128), (16, 128), (1, 127))
N_B_PATTERNS = 3

LANES = 128
VMEM_LIMIT = 56 * 1024 * 1024

_QA, _KA, _VA, _GA = 0, 8, 9, 10
_QB, _KB, _VB, _GB = 18, 26, 34, 42


def _params(sem):
    return pltpu.CompilerParams(dimension_semantics=sem, vmem_limit_bytes=VMEM_LIMIT)


class _Cast(NamedTuple):
    src: jax.Array
    first_row: int
    n_rows: int


def _call_with_casts(body, *, grid, in_specs, out_specs, out_shape, scratch_shapes, name, args, casts=()):
    n_in, n_cast = len(in_specs), len(casts)
    steps = grid[0] * grid[1]
    cast_in, cast_out, cast_shape = [], [], []
    for c in casts:
        cols = c.src.shape[1]
        tile = c.n_rows // steps
        first = c.first_row // tile
        assert tile * steps == c.n_rows and first * tile == c.first_row and tile % 16 == 0
        cast_in.append(pl.BlockSpec((tile, cols), lambda i, j, first=first: (first + i * grid[1] + j, 0)))
        cast_out.append(pl.BlockSpec((tile, cols), lambda i, j: (i * grid[1] + j, 0)))
        cast_shape.append(jax.ShapeDtypeStruct((c.n_rows, cols), BF16))

    def kernel(*refs):
        ins, srcs = refs[:n_in], refs[n_in:n_in + n_cast]
        out, dsts = refs[n_in + n_cast], refs[n_in + n_cast + 1:n_in + 2 * n_cast + 1]
        for src, dst in zip(srcs, dsts):
            dst[...] = src[...].astype(dst.dtype)
        body(*ins, out, *refs[n_in + 2 * n_cast + 1:])

    res = pl.pallas_call(
        kernel,
        grid=grid,
        in_specs=list(in_specs) + cast_in,
        out_specs=[out_specs] + cast_out,
        out_shape=[out_shape] + cast_shape,
        scratch_shapes=scratch_shapes,
        compiler_params=_params(("arbitrary", "arbitrary")),
        name=name,
    )(*args, *[c.src for c in casts])
    return res[0], list(res[1:])


def _bucket_tables():
    a = np.arange(BLK)[:, None]
    b = np.arange(2 * BLK)[None, :]
    dist = BLK + a - b
    max_exact = NUM_BUCKETS // 2
    out = []
    for dil, max_dist in PATTERNS:
        n = np.maximum(dist * dil, 0)
        large = max_exact + (np.log(np.maximum(n, 1) / max_exact)
                             / np.log(REL_MAX_DISTANCE / max_exact)
                             * (NUM_BUCKETS - max_exact)).astype(np.int32)
        large = np.minimum(large, NUM_BUCKETS - 1)
        bucket = np.where(n < max_exact, n, large).astype(np.int32)
        valid = (dist >= 0) & (dist <= max_dist)
        out.append(np.where(valid, bucket, -1).astype(np.int32))
    return np.stack(out)


def _bias_kernel(tbl_ref, bucket_ref, out_ref):
    t = pl.program_id(0)
    h = pl.program_id(1)
    col = h + jnp.where(t < N_B_PATTERNS, A_HEADS, 0)
    bk = bucket_ref[...]
    acc = jnp.full(bk.shape, NEG, F32)
    for b in range(NUM_BUCKETS):
        acc = jnp.where(bk == b, tbl_ref[b, col], acc)
    out_ref[0] = acc
    prev_cols = lax.broadcasted_iota(jnp.int32, bk.shape, 1) < BLK
    out_ref[1] = jnp.where(prev_cols, NEG, acc)


def _bias_tables(rel_bias):
    buckets = jnp.asarray(_bucket_tables())
    return pl.pallas_call(
        _bias_kernel,
        grid=(len(PATTERNS), 16),
        in_specs=[pl.BlockSpec(memory_space=pltpu.SMEM),
                  pl.BlockSpec((None, BLK, 2 * BLK), lambda t, h: (t, 0, 0))],
        out_specs=pl.BlockSpec((None, None, 2, BLK, 2 * BLK), lambda t, h: (t, h, 0, 0, 0)),
        out_shape=jax.ShapeDtypeStruct((len(PATTERNS), 16, 2, BLK, 2 * BLK), F32),
        compiler_params=_params(("arbitrary", "arbitrary")),
        name="bias_tables",
    )(rel_bias.astype(F32), buckets)


def _norm_rows(x_ref, g_ref, h_scr, tm):
    def body(c, carry):
        r = pl.ds(c * 128, 128)
        x = x_ref[r, :]
        ms = jnp.mean(x * x, axis=-1, keepdims=True)
        h_scr[r, :] = (x * lax.rsqrt(ms + EPS) * g_ref[...]).astype(BF16)
        return carry
    lax.fori_loop(0, tm // 128, body, 0)


def _even_in_kernel(x_ref, g_ref, w_ref, o_ref, h_scr, *, tm):
    @pl.when(pl.program_id(1) == 0)
    def _():
        _norm_rows(x_ref, g_ref, h_scr, tm)
    o_ref[...] = jnp.dot(h_scr[...], w_ref[...], preferred_element_type=F32).astype(o_ref.dtype)


def _even_in(x2, g, w, layer, *, tm=1024, tn=1280):
    m, d = x2.shape
    n = w.shape[2]
    return pl.pallas_call(
        functools.partial(_even_in_kernel, tm=tm),
        grid=(m // tm, n // tn),
        in_specs=[pl.BlockSpec((tm, d), lambda i, j: (i, 0)),
                  pl.BlockSpec((1, d), lambda i, j: (0, 0)),
                  pl.BlockSpec((None, d, tn), lambda i, j: (layer, 0, j))],
        out_specs=pl.BlockSpec((tm, tn), lambda i, j: (i, j)),
        out_shape=jax.ShapeDtypeStruct((m, n), BF16),
        scratch_shapes=[pltpu.VMEM((tm, d), BF16)],
        compiler_params=_params(("arbitrary", "arbitrary")),
        name="even_in",
    )(x2, g, w)


def _gelu(x):
    return 0.5 * x * (1.0 + lax.erf(x * (1.0 / math.sqrt(2.0))))


def _silu(x):
    return x * jax.nn.sigmoid(x)


def _odd_in_kernel(x_ref, g_ref, wv_ref, wu_ref, wg_ref, vg_ref, ws_ref, bs_ref, y_ref,
                   h_scr, v_scr, ssq_scr, *, tm, tn):
    j = pl.program_id(1)
    nb = C_WIDTH // tn

    @pl.when(j == 0)
    def _():
        _norm_rows(x_ref, g_ref, h_scr, tm)
        ssq_scr[...] = jnp.zeros_like(ssq_scr)

    @pl.when(j < nb)
    def _():
        v = _gelu(jnp.dot(h_scr[...], wv_ref[...], preferred_element_type=F32))
        v_scr[j] = v.astype(v_scr.dtype)
        ssq_scr[...] += jnp.sum(v * v, axis=-1, keepdims=True)

    @pl.when(j >= nb)
    def _():
        jj = j - nb
        h = h_scr[...]
        ug = (_gelu(jnp.dot(h, wu_ref[...], preferred_element_type=F32))
              * _silu(jnp.dot(h, wg_ref[...], preferred_element_type=F32)))
        tril = (lax.broadcasted_iota(jnp.int32, (C_CHUNK, C_CHUNK), 0)
                >= lax.broadcasted_iota(jnp.int32, (C_CHUNK, C_CHUNK), 1))
        groups_per_tile = tn // C_GROUP_DIM
        vg = vg_ref[jj]
        for gg in range(groups_per_tile):
            grp = jj * groups_per_tile + gg
            wt = jnp.where(tril, ws_ref[grp], 0.0).astype(BF16)
            cols = slice(gg * C_GROUP_DIM, (gg + 1) * C_GROUP_DIM)
            for c in range(tm // C_CHUNK):
                rows = slice(c * C_CHUNK, (c + 1) * C_CHUNK)
                inv = lax.rsqrt(ssq_scr[rows, :][:, :1] * (1.0 / C_WIDTH) + EPS)
                vn = (v_scr[jj, rows, cols].astype(F32) * inv * vg[:, cols]).astype(BF16)
                s = jnp.dot(wt, vn, preferred_element_type=F32) + bs_ref[grp]
                y_ref[rows, cols] = (ug[rows, cols] * s).astype(y_ref.dtype)


def _odd_in(x2, g, w, layer, vg, ws, bs, casts=(), *, tm=1024, tn=512):
    m, d = x2.shape
    nb = C_WIDTH // tn
    return _call_with_casts(
        functools.partial(_odd_in_kernel, tm=tm, tn=tn),
        grid=(m // tm, 2 * nb),
        in_specs=[pl.BlockSpec((tm, d), lambda i, j: (i, 0)),
                  pl.BlockSpec((1, d), lambda i, j: (0, 0)),
                  pl.BlockSpec((None, d, tn), lambda i, j: (layer, 0, nb + jnp.minimum(j, nb - 1))),
                  pl.BlockSpec((None, d, tn), lambda i, j: (layer, 0, jnp.maximum(j - nb, 0))),
                  pl.BlockSpec((None, d, tn), lambda i, j: (layer, 0, 2 * nb + jnp.maximum(j - nb, 0))),
                  pl.BlockSpec((nb, 1, tn), lambda i, j: (0, 0, 0)),
                  pl.BlockSpec((C_GROUPS, C_CHUNK, C_CHUNK), lambda i, j: (0, 0, 0)),
                  pl.BlockSpec((C_GROUPS, C_CHUNK, 1), lambda i, j: (0, 0, 0))],
        out_specs=pl.BlockSpec((tm, tn), lambda i, j: (i, jnp.maximum(j - nb, 0))),
        out_shape=jax.ShapeDtypeStruct((m, C_WIDTH), BF16),
        scratch_shapes=[pltpu.VMEM((tm, d), BF16),
                        pltpu.VMEM((nb, tm, tn), BF16),
                        pltpu.VMEM((tm, LANES), F32)],
        name="odd_in",
        args=(x2, g, w, w, w, vg.reshape(nb, 1, tn), ws, bs.reshape(C_GROUPS, C_CHUNK, 1)),
        casts=casts)


PAD = BLK
UNROLL = 4


def _and(b, mask):
    return b & mask if isinstance(b, int) else lax.bitwise_and(b, mask)


def _shr(b, s):
    return b >> s if isinstance(b, int) else lax.shift_right_logical(b, s)


def _lane_lo(rows):
    return lax.broadcasted_iota(jnp.int32, (rows, LANES), 1) < HEAD_DIM


def _head_rms(x, gain):
    w = x.shape[1]
    r = lax.broadcasted_iota(jnp.int32, (w, w), 0) // HEAD_DIM
    c = lax.broadcasted_iota(jnp.int32, (w, w), 1) // HEAD_DIM
    ones_bd = (r == c).astype(BF16)
    x2 = x * x
    hi = x2.astype(BF16)
    lo = (x2 - hi.astype(F32)).astype(BF16)
    ssq = (jnp.dot(hi, ones_bd, preferred_element_type=F32)
           + jnp.dot(lo, ones_bd, preferred_element_type=F32))
    return x * lax.rsqrt(ssq * (1.0 / HEAD_DIM) + EPS) * gain


def _attn_a_kernel(q0_ref, q1_ref, q2_ref, q3_ref, k_ref, v_ref, g0_ref, g1_ref, g2_ref, g3_ref,
                   gq_ref, gk_ref, snk_ref, bias_ref, o_ref, qb, kb, v0b, v1b, s_scr, p_scr, m_scr, *, seq):
    q_refs = (q0_ref, q1_ref, q2_ref, q3_ref)
    g_refs = (g0_ref, g1_ref, g2_ref, g3_ref)
    npair = len(q_refs)
    nblk = seq // BLK
    lane = lax.broadcasted_iota(jnp.int32, (BLK, LANES), 1)
    keep = (lane // HEAD_DIM) == pl.program_id(1)
    lo = lane < HEAD_DIM
    hi = jnp.logical_not(lo)
    zeros = jnp.zeros((PAD, LANES), BF16)
    kb[pl.ds(0, PAD), :] = zeros
    v0b[pl.ds(0, PAD), :] = zeros
    v1b[pl.ds(0, PAD), :] = zeros
    gq = gq_ref[...] * SCALE
    gq = jnp.concatenate([gq, gq], axis=1)

    def prep(c, carry):
        r = pl.ds(c * BLK, BLK)
        ro = pl.ds(PAD + c * BLK, BLK)
        for pp in range(0, npair, 2):
            qq = jnp.concatenate([q_refs[pp][r, :], q_refs[pp + 1][r, :]], axis=1).astype(F32)
            qn = _head_rms(qq, gq).astype(BF16)
            qb[pp, r, :] = qn[:, :LANES]
            qb[pp + 1, r, :] = qn[:, LANES:]
        kn = _head_rms(k_ref[r, :].astype(F32), gk_ref[...])
        kb[ro, :] = jnp.where(keep, kn, pltpu.roll(kn, HEAD_DIM, 1)).astype(BF16)
        v = v_ref[r, :].astype(F32)
        vd = jnp.where(keep, v, pltpu.roll(v, HEAD_DIM, 1))
        v0b[ro, :] = jnp.where(lo, vd, 1.0).astype(BF16)
        v1b[ro, :] = jnp.where(lo, 1.0, vd).astype(BF16)
        return carry
    lax.fori_loop(0, nblk, prep, 0, unroll=4)

    def scores(n):
        kk = kb[pl.ds(n * BLK, 2 * BLK), :]
        for pp in range(npair):
            q = qb[pp, pl.ds(n * BLK, BLK), :]
            for hh, sel in enumerate((lo, hi)):
                s_scr[n % 2, pp, hh] = lax.dot_general(
                    jnp.where(sel, q, jnp.zeros_like(q)), kk, (((1,), (1,)), ((), ())),
                    preferred_element_type=F32)

    def softmax(n):
        slot = n % 2
        for pp in range(npair):
            ms = []
            for hh in range(2):
                sc = s_scr[slot, pp, hh] + bias_ref[2 * pp + hh, int(n == 0)]
                m = jnp.max(sc, axis=-1, keepdims=True)
                p_scr[slot, pp, hh] = jnp.exp(sc - m).astype(BF16)
                ms.append(jnp.broadcast_to(m, (BLK, LANES)))
            m_scr[slot, pp] = jnp.where(lo, ms[0], ms[1])

    def weighted_values(n):
        slot = n % 2
        r = pl.ds(n * BLK, BLK)
        rk = pl.ds(n * BLK, 2 * BLK)
        for pp in range(npair):
            ul0 = jnp.dot(p_scr[slot, pp, 0], v0b[rk, :], preferred_element_type=F32)
            ul1 = jnp.dot(p_scr[slot, pp, 1], v1b[rk, :], preferred_element_type=F32)
            u = jnp.where(lo, ul0, ul1)
            l = pltpu.roll(jnp.where(lo, ul1, ul0), HEAD_DIM, 1)
            m = m_scr[slot, pp]
            snk = snk_ref[pp:pp + 1, :]
            mx = jnp.maximum(m, snk)
            w = jnp.exp(m - mx)
            o = u * (w / (l * w + jnp.exp(snk - mx)))
            o_ref[r, pl.ds(pp * LANES, LANES)] = (o * _silu(g_refs[pp][r, :].astype(F32))).astype(o_ref.dtype)

    for n in range(nblk + 2):
        if n - 2 >= 0:
            weighted_values(n - 2)
        if n < nblk:
            scores(n)
        if 0 <= n - 1 < nblk:
            softmax(n - 1)


def _attn_a(z3, gq2, gk2, snk3, tabs, casts=()):
    bsz, seq, _ = z3.shape
    npair = A_HEADS // A_KV_HEADS // 2

    def col(off, pp):
        return pl.BlockSpec((None, seq, LANES), lambda b, gi: (b, 0, off + npair * gi + pp))

    def fixed(off):
        return pl.BlockSpec((None, seq, LANES), lambda b, gi: (b, 0, off))

    return _call_with_casts(
        functools.partial(_attn_a_kernel, seq=seq),
        grid=(bsz, A_KV_HEADS),
        in_specs=[col(_QA, 0), col(_QA, 1), col(_QA, 2), col(_QA, 3), fixed(_KA), fixed(_VA),
                  col(_GA, 0), col(_GA, 1), col(_GA, 2), col(_GA, 3),
                  pl.BlockSpec((1, LANES), lambda b, gi: (0, 0)),
                  pl.BlockSpec((1, LANES), lambda b, gi: (0, 0)),
                  pl.BlockSpec((None, npair, LANES), lambda b, gi: (gi, 0, 0)),
                  pl.BlockSpec((None, 2 * npair, 2, BLK, 2 * BLK), lambda b, gi: (N_B_PATTERNS, gi, 0, 0, 0))],
        out_specs=pl.BlockSpec((None, seq, npair * LANES), lambda b, gi: (b, 0, gi)),
        out_shape=jax.ShapeDtypeStruct((bsz, seq, A_WIDTH), BF16),
        scratch_shapes=[pltpu.VMEM((npair, seq, LANES), BF16),
                        pltpu.VMEM((PAD + seq, LANES), BF16),
                        pltpu.VMEM((PAD + seq, LANES), BF16),
                        pltpu.VMEM((PAD + seq, LANES), BF16),
                        pltpu.VMEM((2, npair, 2, BLK, 2 * BLK), F32),
                        pltpu.VMEM((2, npair, 2, BLK, 2 * BLK), BF16),
                        pltpu.VMEM((2, npair, BLK, LANES), F32)],
        name="attn_a",
        args=(z3, z3, z3, z3, z3, z3, z3, z3, z3, z3, gq2, gk2, snk3, tabs),
        casts=casts)


def _attn_b_kernel(q_ref, k_ref, v_ref, g_ref, gq_ref, gk_ref, bias_ref, o_ref,
                   qf, kf, vf, q4f, k4f, v4f, qb, kb, v0b, v1b, s_scr, p_scr, m_scr, l_scr, u_scr, *, seq):
    lo = _lane_lo(BLK)
    hi = jnp.logical_not(lo)
    nblk = seq // BLK
    zeros = jnp.zeros((PAD, LANES), BF16)
    for pi in range(2):
        kb[pi, pl.ds(0, PAD), :] = zeros
        v0b[pi, pl.ds(0, PAD), :] = zeros
        v1b[pi, pl.ds(0, PAD), :] = zeros

    def put(pi, dst, q, k, v):
        qb[pi, dst, :] = q.astype(BF16)
        kb[pi, dst, :] = k.astype(BF16)
        v0b[pi, dst, :] = jnp.where(lo, v, 1.0).astype(BF16)
        v1b[pi, dst, :] = jnp.where(lo, 1.0, v).astype(BF16)

    def strided(t):
        return pl.ds(_and(t, 3) * (4 * BLK) + _shr(t, 2), BLK, stride=4)

    gains = jnp.concatenate([gq_ref[...] * SCALE, gk_ref[...]], axis=1)

    def prep(c, carry):
        r = pl.ds(c * BLK, BLK)
        qk = jnp.concatenate([q_ref[r, :], k_ref[r, :]], axis=1).astype(F32)
        n = _head_rms(qk, gains)
        q, k, v = n[:, :LANES], n[:, LANES:], v_ref[r, :].astype(F32)
        qf[r, :] = q
        kf[r, :] = k
        vf[r, :] = v
        put(0, pl.ds(PAD + c * BLK, BLK), q, k, v)
        return carry
    lax.fori_loop(0, nblk, prep, 0, unroll=4)

    def deint4(t, carry):
        src = strided(t)
        dst = pl.ds(t * BLK, BLK)
        q, k, v = qf[src, :], kf[src, :], vf[src, :]
        q4f[dst, :] = q
        k4f[dst, :] = k
        v4f[dst, :] = v
        put(1, pl.ds(PAD + t * BLK, BLK), q, k, v)
        return carry
    lax.fori_loop(0, nblk, deint4, 0)

    def deint16(t, carry):
        src = strided(t)
        put(2, pl.ds(PAD + t * BLK, BLK), q4f[src, :], k4f[src, :], v4f[src, :])
        return carry
    lax.fori_loop(0, nblk, deint16, 0)

    groups = [(pi, g) for pi in range(N_B_PATTERNS) for g in range(nblk // UNROLL)]

    def geometry(pi):
        single = pi == 2
        return single, (BLK if single else 2 * BLK), (PAD if single else PAD - BLK)

    def out_rows(pi, b):
        return pl.ds(b * BLK, BLK) if pi == 0 else strided(b)

    def scores(t):
        pi, g = groups[t]
        _, keys, koff = geometry(pi)
        for uu in range(UNROLL):
            b = g * UNROLL + uu
            q = qb[pi, pl.ds(PAD + b * BLK, BLK), :]
            kk = kb[pi, pl.ds(koff + b * BLK, keys), :]
            for hh, sel in enumerate((lo, hi)):
                s_scr[(t % 2) * UNROLL + uu, hh, :, pl.ds(0, keys)] = lax.dot_general(
                    jnp.where(sel, q, jnp.zeros_like(q)), kk, (((1,), (1,)), ((), ())),
                    preferred_element_type=F32)

    def softmax(t):
        pi, g = groups[t]
        single, keys, _ = geometry(pi)
        for uu in range(UNROLL):
            b = g * UNROLL + uu
            slot = (t % 2) * UNROLL + uu
            first = int(b % (nblk if pi == 0 else 4) == 0)
            ms = []
            for hh in range(2):
                bias = bias_ref[pi, hh, 0, :, pl.ds(BLK, BLK)] if single else bias_ref[pi, hh, first]
                sc = s_scr[slot, hh, :, pl.ds(0, keys)] + bias
                m = jnp.max(sc, axis=-1, keepdims=True)
                p_scr[slot, hh, :, pl.ds(0, keys)] = jnp.exp(sc - m).astype(BF16)
                ms.append(jnp.broadcast_to(m, (BLK, LANES)))
            m_scr[pi, out_rows(pi, b), :] = jnp.where(lo, ms[0], ms[1])

    def weighted_values(t):
        pi, g = groups[t]
        _, keys, koff = geometry(pi)
        for uu in range(UNROLL):
            b = g * UNROLL + uu
            slot = (t % 2) * UNROLL + uu
            rk = pl.ds(koff + b * BLK, keys)
            ul0 = jnp.dot(p_scr[slot, 0, :, pl.ds(0, keys)], v0b[pi, rk, :], preferred_element_type=F32)
            ul1 = jnp.dot(p_scr[slot, 1, :, pl.ds(0, keys)], v1b[pi, rk, :], preferred_element_type=F32)
            u_scr[pi, out_rows(pi, b), :] = jnp.where(lo, ul0, ul1)
            l_scr[pi, out_rows(pi, b), :] = pltpu.roll(jnp.where(lo, ul1, ul0), HEAD_DIM, 1)

    for t in range(len(groups) + 2):
        if t - 2 >= 0:
            weighted_values(t - 2)
        if t < len(groups):
            scores(t)
        if 0 <= t - 1 < len(groups):
            softmax(t - 1)

    def renat(t, carry):
        for ref in (m_scr, l_scr, u_scr):
            ref[3, strided(t), :] = ref[2, pl.ds(t * BLK, BLK), :]
        return carry
    lax.fori_loop(0, nblk, renat, 0)

    def combine(c, carry):
        r = pl.ds(c * BLK, BLK)
        slots = (0, 1, 3)
        ms = [m_scr[s, r, :] for s in slots]
        mx = jnp.maximum(jnp.maximum(ms[0], ms[1]), ms[2])
        num = jnp.zeros((BLK, LANES), F32)
        den = jnp.zeros((BLK, LANES), F32)
        for s, m in zip(slots, ms):
            w = jnp.exp(m - mx)
            num = num + w * u_scr[s, r, :]
            den = den + w * l_scr[s, r, :]
        o_ref[r, :] = ((num / den) * _silu(g_ref[r, :].astype(F32))).astype(o_ref.dtype)
        return carry
    lax.fori_loop(0, nblk, combine, 0, unroll=2)


def _attn_b(z3, gq2, gk2, tabs, casts=()):
    bsz, seq, _ = z3.shape
    col = lambda off: pl.BlockSpec((None, seq, LANES), lambda b, p: (b, 0, off + p))
    f32_rows = pltpu.VMEM((seq, LANES), F32)
    bf16_ops = pltpu.VMEM((N_B_PATTERNS, PAD + seq, LANES), BF16)
    stats = pltpu.VMEM((N_B_PATTERNS + 1, seq, LANES), F32)
    return _call_with_casts(
        functools.partial(_attn_b_kernel, seq=seq),
        grid=(bsz, B_HEADS // 2),
        in_specs=[col(_QB), col(_KB), col(_VB), col(_GB),
                  pl.BlockSpec((1, LANES), lambda b, p: (0, 0)),
                  pl.BlockSpec((1, LANES), lambda b, p: (0, 0)),
                  pl.BlockSpec((N_B_PATTERNS, 2, 2, BLK, 2 * BLK), lambda b, p: (0, p, 0, 0, 0))],
        out_specs=pl.BlockSpec((None, seq, LANES), lambda b, p: (b, 0, p)),
        out_shape=jax.ShapeDtypeStruct((bsz, seq, B_WIDTH), BF16),
        scratch_shapes=[f32_rows] * 6 + [bf16_ops] * 4
                       + [pltpu.VMEM((2 * UNROLL, 2, BLK, 2 * BLK), F32),
                          pltpu.VMEM((2 * UNROLL, 2, BLK, 2 * BLK), BF16)] + [stats] * 3,
        name="attn_b",
        args=(z3, z3, z3, z3, gq2, gk2, tabs),
        casts=casts)


def _even_out_kernel(ya_ref, yb_ref, wa_ref, wb_ref, x_ref, o_ref):
    acc = jnp.dot(ya_ref[...], wa_ref[...], preferred_element_type=F32)
    acc = acc + jnp.dot(yb_ref[...], wb_ref[...], preferred_element_type=F32)
    o_ref[...] = x_ref[...] + acc


def _even_out(ya, yb, w, layer, x2, *, tm=2048, tn=512):
    m, d = x2.shape
    ka, kb = ya.shape[1], yb.shape[1]
    assert ka == kb
    return pl.pallas_call(
        _even_out_kernel,
        grid=(m // tm, d // tn),
        in_specs=[pl.BlockSpec((tm, ka), lambda i, j: (i, 0)),
                  pl.BlockSpec((tm, kb), lambda i, j: (i, 0)),
                  pl.BlockSpec((None, ka, tn), lambda i, j: (layer, 0, j)),
                  pl.BlockSpec((None, kb, tn), lambda i, j: (layer, 1, j)),
                  pl.BlockSpec((tm, tn), lambda i, j: (i, j))],
        out_specs=pl.BlockSpec((tm, tn), lambda i, j: (i, j)),
        out_shape=jax.ShapeDtypeStruct((m, d), F32),
        compiler_params=_params(("arbitrary", "arbitrary")),
        name="even_out",
    )(ya, yb, w, w, x2)


def _odd_out_kernel(y_ref, w_ref, x_ref, o_ref):
    o_ref[...] = x_ref[...] + jnp.dot(y_ref[...], w_ref[...], preferred_element_type=F32)


def _odd_out(y, w, layer, x2, *, tm=1024, tn=512):
    m, d = x2.shape
    k = y.shape[1]
    return pl.pallas_call(
        _odd_out_kernel,
        grid=(m // tm, d // tn),
        in_specs=[pl.BlockSpec((tm, k), lambda i, j: (i, 0)),
                  pl.BlockSpec((None, k, tn), lambda i, j: (layer, 0, j)),
                  pl.BlockSpec((tm, tn), lambda i, j: (i, j))],
        out_specs=pl.BlockSpec((tm, tn), lambda i, j: (i, j)),
        out_shape=jax.ShapeDtypeStruct((m, d), F32),
        compiler_params=_params(("arbitrary", "arbitrary")),
        name="odd_out",
    )(y, w, x2)


def kernel(x, ev_ln_g, ev_w_in, ev_qk_g, ev_sinks, ev_w_out, od_ln_g, od_w_in, od_v_g, od_w_s,
           od_b_s, od_w_out, rel_bias):
    bsz, seq, d = x.shape
    n_even, n_odd = ev_ln_g.shape[0], od_ln_g.shape[0]
    tabs = _bias_tables(rel_bias)
    flat = lambda w: w.reshape(-1, w.shape[-1])
    whole = lambda w: _Cast(flat(w), 0, w.shape[0] * w.shape[1])
    layer_of = lambda w, j: _Cast(flat(w), j * w.shape[1], w.shape[1])
    ev_in_b = {0: ev_w_in[0:1].astype(BF16)}
    od_in_b, ev_out_b, od_out_b = {}, None, None
    x2 = x.reshape(bsz * seq, d)
    for i in range(n_even + n_odd):
        j = i // 2
        if i % 2 == 0:
            z = _even_in(x2, ev_ln_g[j].reshape(1, d), ev_in_b[j], 0)
            z3 = z.reshape(bsz, seq, EVEN_IN)
            gains = jnp.tile(ev_qk_g[j].astype(F32), (1, 2))
            snk3 = jnp.repeat(ev_sinks[j].astype(F32), HEAD_DIM).reshape(A_KV_HEADS, -1, LANES)
            ya, done = _attn_a(z3, gains[0:1], gains[1:2], snk3, tabs, [whole(ev_w_out)] if j == 0 else [])
            if j == 0:
                ev_out_b = done[0].reshape(ev_w_out.shape)
            yb, done = _attn_b(z3, gains[2:3], gains[3:4], tabs, [layer_of(od_w_in, j)] if j < n_odd else [])
            if j < n_odd:
                od_in_b[j] = done[0][None]
            x2 = _even_out(ya.reshape(bsz * seq, A_WIDTH), yb.reshape(bsz * seq, B_WIDTH), ev_out_b, j, x2)
        else:
            casts = ([whole(od_w_out)] if j == 0 else []) + ([layer_of(ev_w_in, j + 1)] if j + 1 < n_even else [])
            y, done = _odd_in(x2, od_ln_g[j].reshape(1, d), od_in_b[j], 0, od_v_g[j].astype(F32),
                              od_w_s[j].astype(F32), od_b_s[j].astype(F32), casts)
            if j == 0:
                od_out_b = done.pop(0).reshape(od_w_out.shape)
            if j + 1 < n_even:
                ev_in_b[j + 1] = done.pop(0)[None]
            x2 = _odd_out(y, od_out_b, j, x2)
    return x2.reshape(bsz, seq, d)
```

```python
import functools
import math
from typing import NamedTuple

import numpy as np
import jax
import jax.numpy as jnp
from jax import lax
from jax.experimental import pallas as pl
from jax.experimental.pallas import tpu as pltpu

F32 = jnp.float32
BF16 = jnp.bfloat16

D_MODEL = 2048
HEAD_DIM = 64
A_HEADS = 16
A_KV_HEADS = 2
B_HEADS = 16
BLK = 128
NUM_BUCKETS = 32
REL_MAX_DISTANCE = 2048
A_WIDTH = A_HEADS * HEAD_DIM
B_WIDTH = B_HEADS * HEAD_DIM
EVEN_IN = 6400
C_WIDTH = 2 * D_MODEL
C_GROUPS = 16
C_GROUP_DIM = C_WIDTH // C_GROUPS
C_CHUNK = 128
EPS = 1e-6
NEG = -1e30
SCALE = HEAD_DIM ** -0.5
PATTERNS = ((1, 128), (4, 128), (16, 128), (1, 127))
N_B_PATTERNS = 3

LANES = 128
VMEM_LIMIT = 56 * 1024 * 1024

_QA, _KA, _VA, _GA = 0, 8, 9, 10
_QB, _KB, _VB, _GB = 18, 26, 34, 42


def _params(sem):
    return pltpu.CompilerParams(dimension_semantics=sem, vmem_limit_bytes=VMEM_LIMIT)


class _Cast(NamedTuple):
    src: jax.Array
    first_row: int
    n_rows: int


def _call_with_casts(body, *, grid, in_specs, out_specs, out_shape, scratch_shapes, name, args, casts=()):
    n_in, n_cast = len(in_specs), len(casts)
    steps = grid[0] * grid[1]
    cast_in, cast_out, cast_shape = [], [], []
    for c in casts:
        cols = c.src.shape[1]
        tile = c.n_rows // steps
        first = c.first_row // tile
        assert tile * steps == c.n_rows and first * tile == c.first_row and tile % 16 == 0
        cast_in.append(pl.BlockSpec((tile, cols), lambda i, j, first=first: (first + i * grid[1] + j, 0)))
        cast_out.append(pl.BlockSpec((tile, cols), lambda i, j: (i * grid[1] + j, 0)))
        cast_shape.append(jax.ShapeDtypeStruct((c.n_rows, cols), BF16))

    def kernel(*refs):
        ins, srcs = refs[:n_in], refs[n_in:n_in + n_cast]
        out, dsts = refs[n_in + n_cast], refs[n_in + n_cast + 1:n_in + 2 * n_cast + 1]
        for src, dst in zip(srcs, dsts):
            dst[...] = src[...].astype(dst.dtype)
        body(*ins, out, *refs[n_in + 2 * n_cast + 1:])

    res = pl.pallas_call(
        kernel,
        grid=grid,
        in_specs=list(in_specs) + cast_in,
        out_specs=[out_specs] + cast_out,
        out_shape=[out_shape] + cast_shape,
        scratch_shapes=scratch_shapes,
        compiler_params=_params(("arbitrary", "arbitrary")),
        name=name,
    )(*args, *[c.src for c in casts])
    return res[0], list(res[1:])


def _bucket_tables():
    a = np.arange(BLK)[:, None]
    b = np.arange(2 * BLK)[None, :]
    dist = BLK + a - b
    max_exact = NUM_BUCKETS // 2
    out = []
    for dil, max_dist in PATTERNS:
        n = np.maximum(dist * dil, 0)
        large = max_exact + (np.log(np.maximum(n, 1) / max_exact)
                             / np.log(REL_MAX_DISTANCE / max_exact)
                             * (NUM_BUCKETS - max_exact)).astype(np.int32)
        large = np.minimum(large, NUM_BUCKETS - 1)
        bucket = np.where(n < max_exact, n, large).astype(np.int32)
        valid = (dist >= 0) & (dist <= max_dist)
        out.append(np.where(valid, bucket, -1).astype(np.int32))
    return np.stack(out)


def _bias_kernel(tbl_ref, bucket_ref, out_ref):
    t = pl.program_id(0)
    h = pl.program_id(1)
    col = h + jnp.where(t < N_B_PATTERNS, A_HEADS, 0)
    bk = bucket_ref[...]
    acc = jnp.full(bk.shape, NEG, F32)
    for b in range(NUM_BUCKETS):
        acc = jnp.where(bk == b, tbl_ref[b, col], acc)
    out_ref[0] = acc
    prev_cols = lax.broadcasted_iota(jnp.int32, bk.shape, 1) < BLK
    out_ref[1] = jnp.where(prev_cols, NEG, acc)


def _bias_tables(rel_bias):
    buckets = jnp.asarray(_bucket_tables())
    return pl.pallas_call(
        _bias_kernel,
        grid=(len(PATTERNS), 16),
        in_specs=[pl.BlockSpec(memory_space=pltpu.SMEM),
                  pl.BlockSpec((None, BLK, 2 * BLK), lambda t, h: (t, 0, 0))],
        out_specs=pl.BlockSpec((None, None, 2, BLK, 2 * BLK), lambda t, h: (t, h, 0, 0, 0)),
        out_shape=jax.ShapeDtypeStruct((len(PATTERNS), 16, 2, BLK, 2 * BLK), F32),
        compiler_params=_params(("arbitrary", "arbitrary")),
        name="bias_tables",
    )(rel_bias.astype(F32), buckets)


def _norm_rows(x_ref, g_ref, h_scr, tm):
    def body(c, carry):
        r = pl.ds(c * 128, 128)
        x = x_ref[r, :]
        ms = jnp.mean(x * x, axis=-1, keepdims=True)
        h_scr[r, :] = (x * lax.rsqrt(ms + EPS) * g_ref[...]).astype(BF16)
        return carry
    lax.fori_loop(0, tm // 128, body, 0)


def _even_in_kernel(x_ref, g_ref, w_ref, o_ref, h_scr, *, tm):
    @pl.when(pl.program_id(1) == 0)
    def _():
        _norm_rows(x_ref, g_ref, h_scr, tm)
    o_ref[...] = jnp.dot(h_scr[...], w_ref[...], preferred_element_type=F32).astype(o_ref.dtype)


def _even_in(x2, g, w, layer, *, tm=1024, tn=1280):
    m, d = x2.shape
    n = w.shape[2]
    return pl.pallas_call(
        functools.partial(_even_in_kernel, tm=tm),
        grid=(m // tm, n // tn),
        in_specs=[pl.BlockSpec((tm, d), lambda i, j: (i, 0)),
                  pl.BlockSpec((1, d), lambda i, j: (0, 0)),
                  pl.BlockSpec((None, d, tn), lambda i, j: (layer, 0, j))],
        out_specs=pl.BlockSpec((tm, tn), lambda i, j: (i, j)),
        out_shape=jax.ShapeDtypeStruct((m, n), BF16),
        scratch_shapes=[pltpu.VMEM((tm, d), BF16)],
        compiler_params=_params(("arbitrary", "arbitrary")),
        name="even_in",
    )(x2, g, w)


def _gelu(x):
    return 0.5 * x * (1.0 + lax.erf(x * (1.0 / math.sqrt(2.0))))


def _silu(x):
    return x * jax.nn.sigmoid(x)


def _odd_in_kernel(x_ref, g_ref, wv_ref, wu_ref, wg_ref, vg_ref, ws_ref, bs_ref, y_ref,
                   h_scr, v_scr, ssq_scr, *, tm, tn):
    j = pl.program_id(1)
    nb = C_WIDTH // tn

    @pl.when(j == 0)
    def _():
        _norm_rows(x_ref, g_ref, h_scr, tm)
        ssq_scr[...] = jnp.zeros_like(ssq_scr)

    @pl.when(j < nb)
    def _():
        v = _gelu(jnp.dot(h_scr[...], wv_ref[...], preferred_element_type=F32))
        v_scr[j] = v.astype(v_scr.dtype)
        ssq_scr[...] += jnp.sum(v * v, axis=-1, keepdims=True)

    @pl.when(j >= nb)
    def _():
        jj = j - nb
        h = h_scr[...]
        ug = (_gelu(jnp.dot(h, wu_ref[...], preferred_element_type=F32))
              * _silu(jnp.dot(h, wg_ref[...], preferred_element_type=F32)))
        tril = (lax.broadcasted_iota(jnp.int32, (C_CHUNK, C_CHUNK), 0)
                >= lax.broadcasted_iota(jnp.int32, (C_CHUNK, C_CHUNK), 1))
        groups_per_tile = tn // C_GROUP_DIM
        vg = vg_ref[jj]
        for gg in range(groups_per_tile):
            grp = jj * groups_per_tile + gg
            wt = jnp.where(tril, ws_ref[grp], 0.0).astype(BF16)
            cols = slice(gg * C_GROUP_DIM, (gg + 1) * C_GROUP_DIM)
            for c in range(tm // C_CHUNK):
                rows = slice(c * C_CHUNK, (c + 1) * C_CHUNK)
                inv = lax.rsqrt(ssq_scr[rows, :][:, :1] * (1.0 / C_WIDTH) + EPS)
                vn = (v_scr[jj, rows, cols].astype(F32) * inv * vg[:, cols]).astype(BF16)
                s = jnp.dot(wt, vn, preferred_element_type=F32) + bs_ref[grp]
                y_ref[rows, cols] = (ug[rows, cols] * s).astype(y_ref.dtype)


def _odd_in(x2, g, w, layer, vg, ws, bs, casts=(), *, tm=1024, tn=512):
    m, d = x2.shape
    nb = C_WIDTH // tn
    return _call_with_casts(
        functools.partial(_odd_in_kernel, tm=tm, tn=tn),
        grid=(m // tm, 2 * nb),
        in_specs=[pl.BlockSpec((tm, d), lambda i, j: (i, 0)),
                  pl.BlockSpec((1, d), lambda i, j: (0, 0)),
                  pl.BlockSpec((None, d, tn), lambda i, j: (layer, 0, nb + jnp.minimum(j, nb - 1))),
                  pl.BlockSpec((None, d, tn), lambda i, j: (layer, 0, jnp.maximum(j - nb, 0))),
                  pl.BlockSpec((None, d, tn), lambda i, j: (layer, 0, 2 * nb + jnp.maximum(j - nb, 0))),
                  pl.BlockSpec((nb, 1, tn), lambda i, j: (0, 0, 0)),
                  pl.BlockSpec((C_GROUPS, C_CHUNK, C_CHUNK), lambda i, j: (0, 0, 0)),
                  pl.BlockSpec((C_GROUPS, C_CHUNK, 1), lambda i, j: (0, 0, 0))],
        out_specs=pl.BlockSpec((tm, tn), lambda i, j: (i, jnp.maximum(j - nb, 0))),
        out_shape=jax.ShapeDtypeStruct((m, C_WIDTH), BF16),
        scratch_shapes=[pltpu.VMEM((tm, d), BF16),
                        pltpu.VMEM((nb, tm, tn), BF16),
                        pltpu.VMEM((tm, LANES), F32)],
        name="odd_in",
        args=(x2, g, w, w, w, vg.reshape(nb, 1, tn), ws, bs.reshape(C_GROUPS, C_CHUNK, 1)),
        casts=casts)


PAD = BLK
UNROLL = 4
NSLOT = 3


def _and(b, mask):
    return b & mask if isinstance(b, int) else lax.bitwise_and(b, mask)


def _shr(b, s):
    return b >> s if isinstance(b, int) else lax.shift_right_logical(b, s)


def _lane_lo(rows):
    return lax.broadcasted_iota(jnp.int32, (rows, LANES), 1) < HEAD_DIM


def _head_rms(x, gain):
    w = x.shape[1]
    r = lax.broadcasted_iota(jnp.int32, (w, w), 0) // HEAD_DIM
    c = lax.broadcasted_iota(jnp.int32, (w, w), 1) // HEAD_DIM
    ones_bd = (r == c).astype(BF16)
    x2 = x * x
    hi = x2.astype(BF16)
    lo = (x2 - hi.astype(F32)).astype(BF16)
    ssq = (jnp.dot(hi, ones_bd, preferred_element_type=F32)
           + jnp.dot(lo, ones_bd, preferred_element_type=F32))
    return x * lax.rsqrt(ssq * (1.0 / HEAD_DIM) + EPS) * gain


def _attn_a_kernel(q0_ref, q1_ref, q2_ref, q3_ref, k_ref, v_ref, g0_ref, g1_ref, g2_ref, g3_ref,
                   gq_ref, gk_ref, snk_ref, bias_ref, o_ref, qb, kb, v0b, v1b, s_scr, mb_scr, p_scr, m_scr, *, seq):
    q_refs = (q0_ref, q1_ref, q2_ref, q3_ref)
    g_refs = (g0_ref, g1_ref, g2_ref, g3_ref)
    npair = len(q_refs)
    nblk = seq // BLK
    lane = lax.broadcasted_iota(jnp.int32, (BLK, LANES), 1)
    keep = (lane // HEAD_DIM) == pl.program_id(1)
    lo = lane < HEAD_DIM
    hi = jnp.logical_not(lo)
    zeros = jnp.zeros((PAD, LANES), BF16)
    kb[pl.ds(0, PAD), :] = zeros
    v0b[pl.ds(0, PAD), :] = zeros
    v1b[pl.ds(0, PAD), :] = zeros
    gq = gq_ref[...] * SCALE
    gq = jnp.concatenate([gq, gq], axis=1)

    def prep(c, carry):
        r = pl.ds(c * BLK, BLK)
        ro = pl.ds(PAD + c * BLK, BLK)
        for pp in range(0, npair, 2):
            qq = jnp.concatenate([q_refs[pp][r, :], q_refs[pp + 1][r, :]], axis=1).astype(F32)
            qn = _head_rms(qq, gq).astype(BF16)
            qb[pp, r, :] = qn[:, :LANES]
            qb[pp + 1, r, :] = qn[:, LANES:]
        kn = _head_rms(k_ref[r, :].astype(F32), gk_ref[...])
        kb[ro, :] = jnp.where(keep, kn, pltpu.roll(kn, HEAD_DIM, 1)).astype(BF16)
        v = v_ref[r, :].astype(F32)
        vd = jnp.where(keep, v, pltpu.roll(v, HEAD_DIM, 1))
        v0b[ro, :] = jnp.where(lo, vd, 1.0).astype(BF16)
        v1b[ro, :] = jnp.where(lo, 1.0, vd).astype(BF16)
        return carry
    lax.fori_loop(0, nblk, prep, 0, unroll=4)

    def scores(n):
        kk = kb[pl.ds(n * BLK, 2 * BLK), :]
        for pp in range(npair):
            q = qb[pp, pl.ds(n * BLK, BLK), :]
            for hh, sel in enumerate((lo, hi)):
                s_scr[n % NSLOT, pp, hh] = lax.dot_general(
                    jnp.where(sel, q, jnp.zeros_like(q)), kk, (((1,), (1,)), ((), ())),
                    preferred_element_type=F32) + bias_ref[2 * pp + hh, int(n == 0)]

    def row_max(n):
        slot = n % NSLOT
        for pp in range(npair):
            for hh in range(2):
                m = jnp.max(s_scr[slot, pp, hh], axis=-1, keepdims=True)
                mb_scr[slot, pp, hh] = jnp.broadcast_to(m, (BLK, LANES))
            m_scr[slot, pp] = jnp.where(lo, mb_scr[slot, pp, 0], mb_scr[slot, pp, 1])

    def exponentials(n):
        slot = n % NSLOT
        for pp in range(npair):
            for hh in range(2):
                mb = mb_scr[slot, pp, hh]
                p_scr[slot, pp, hh] = jnp.exp(s_scr[slot, pp, hh] - jnp.concatenate([mb, mb], axis=1)).astype(BF16)

    def weighted_values(n):
        slot = n % NSLOT
        r = pl.ds(n * BLK, BLK)
        rk = pl.ds(n * BLK, 2 * BLK)
        for pp in range(npair):
            ul0 = jnp.dot(p_scr[slot, pp, 0], v0b[rk, :], preferred_element_type=F32)
            ul1 = jnp.dot(p_scr[slot, pp, 1], v1b[rk, :], preferred_element_type=F32)
            u = jnp.where(lo, ul0, ul1)
            l = pltpu.roll(jnp.where(lo, ul1, ul0), HEAD_DIM, 1)
            m = m_scr[slot, pp]
            snk = snk_ref[pp:pp + 1, :]
            mx = jnp.maximum(m, snk)
            w = jnp.exp(m - mx)
            o = u * (w / (l * w + jnp.exp(snk - mx)))
            o_ref[r, pl.ds(pp * LANES, LANES)] = (o * _silu(g_refs[pp][r, :].astype(F32))).astype(o_ref.dtype)

    for n in range(nblk + 3):
        if n - 3 >= 0:
            weighted_values(n - 3)
        if n < nblk:
            scores(n)
        if 0 <= n - 1 < nblk:
            row_max(n - 1)
        if 0 <= n - 2 < nblk:
            exponentials(n - 2)


def _attn_a(z3, gq2, gk2, snk3, tabs, casts=()):
    bsz, seq, _ = z3.shape
    npair = A_HEADS // A_KV_HEADS // 2

    def col(off, pp):
        return pl.BlockSpec((None, seq, LANES), lambda b, gi: (b, 0, off + npair * gi + pp))

    def fixed(off):
        return pl.BlockSpec((None, seq, LANES), lambda b, gi: (b, 0, off))

    return _call_with_casts(
        functools.partial(_attn_a_kernel, seq=seq),
        grid=(bsz, A_KV_HEADS),
        in_specs=[col(_QA, 0), col(_QA, 1), col(_QA, 2), col(_QA, 3), fixed(_KA), fixed(_VA),
                  col(_GA, 0), col(_GA, 1), col(_GA, 2), col(_GA, 3),
                  pl.BlockSpec((1, LANES), lambda b, gi: (0, 0)),
                  pl.BlockSpec((1, LANES), lambda b, gi: (0, 0)),
                  pl.BlockSpec((None, npair, LANES), lambda b, gi: (gi, 0, 0)),
                  pl.BlockSpec((None, 2 * npair, 2, BLK, 2 * BLK), lambda b, gi: (N_B_PATTERNS, gi, 0, 0, 0))],
        out_specs=pl.BlockSpec((None, seq, npair * LANES), lambda b, gi: (b, 0, gi)),
        out_shape=jax.ShapeDtypeStruct((bsz, seq, A_WIDTH), BF16),
        scratch_shapes=[pltpu.VMEM((npair, seq, LANES), BF16),
                        pltpu.VMEM((PAD + seq, LANES), BF16),
                        pltpu.VMEM((PAD + seq, LANES), BF16),
                        pltpu.VMEM((PAD + seq, LANES), BF16),
                        pltpu.VMEM((NSLOT, npair, 2, BLK, 2 * BLK), F32),
                        pltpu.VMEM((NSLOT, npair, 2, BLK, LANES), F32),
                        pltpu.VMEM((NSLOT, npair, 2, BLK, 2 * BLK), BF16),
                        pltpu.VMEM((NSLOT, npair, BLK, LANES), F32)],
        name="attn_a",
        args=(z3, z3, z3, z3, z3, z3, z3, z3, z3, z3, gq2, gk2, snk3, tabs),
        casts=casts)


def _attn_b_kernel(q_ref, k_ref, v_ref, g_ref, gq_ref, gk_ref, bias_ref, o_ref,
                   qf, kf, vf, q4f, k4f, v4f, qb, kb, v0b, v1b, s_scr, mb_scr, p_scr, m_scr, l_scr, u_scr, *, seq):
    lo = _lane_lo(BLK)
    hi = jnp.logical_not(lo)
    nblk = seq // BLK
    zeros = jnp.zeros((PAD, LANES), BF16)
    for pi in range(2):
        kb[pi, pl.ds(0, PAD), :] = zeros
        v0b[pi, pl.ds(0, PAD), :] = zeros
        v1b[pi, pl.ds(0, PAD), :] = zeros

    def put(pi, dst, q, k, v):
        qb[pi, dst, :] = q.astype(BF16)
        kb[pi, dst, :] = k.astype(BF16)
        v0b[pi, dst, :] = jnp.where(lo, v, 1.0).astype(BF16)
        v1b[pi, dst, :] = jnp.where(lo, 1.0, v).astype(BF16)

    def strided(t):
        return pl.ds(_and(t, 3) * (4 * BLK) + _shr(t, 2), BLK, stride=4)

    gains = jnp.concatenate([gq_ref[...] * SCALE, gk_ref[...]], axis=1)

    def prep(c, carry):
        r = pl.ds(c * BLK, BLK)
        qk = jnp.concatenate([q_ref[r, :], k_ref[r, :]], axis=1).astype(F32)
        n = _head_rms(qk, gains)
        q, k, v = n[:, :LANES], n[:, LANES:], v_ref[r, :].astype(F32)
        qf[r, :] = q
        kf[r, :] = k
        vf[r, :] = v
        put(0, pl.ds(PAD + c * BLK, BLK), q, k, v)
        return carry
    for c in range(nblk):
        prep(c, 0)

    def deint4(t, carry):
        src = strided(t)
        dst = pl.ds(t * BLK, BLK)
        q, k, v = qf[src, :], kf[src, :], vf[src, :]
        q4f[dst, :] = q
        k4f[dst, :] = k
        v4f[dst, :] = v
        put(1, pl.ds(PAD + t * BLK, BLK), q, k, v)
        return carry
    for t in range(nblk):
        deint4(t, 0)

    def deint16(t, carry):
        src = strided(t)
        put(2, pl.ds(PAD + t * BLK, BLK), q4f[src, :], k4f[src, :], v4f[src, :])
        return carry
    for t in range(nblk):
        deint16(t, 0)

    groups = [(pi, g) for pi in range(N_B_PATTERNS) for g in range(nblk // UNROLL)]

    def geometry(pi):
        single = pi == 2
        return single, (BLK if single else 2 * BLK), (PAD if single else PAD - BLK)

    def out_rows(pi, b):
        return pl.ds(b * BLK, BLK) if pi == 0 else strided(b)

    def scores(t):
        pi, g = groups[t]
        single, keys, koff = geometry(pi)
        for uu in range(UNROLL):
            b = g * UNROLL + uu
            first = int(b % (nblk if pi == 0 else 4) == 0)
            q = qb[pi, pl.ds(PAD + b * BLK, BLK), :]
            kk = kb[pi, pl.ds(koff + b * BLK, keys), :]
            for hh, sel in enumerate((lo, hi)):
                bias = bias_ref[pi, hh, 0, :, pl.ds(BLK, BLK)] if single else bias_ref[pi, hh, first]
                s_scr[(t % NSLOT) * UNROLL + uu, hh, :, pl.ds(0, keys)] = lax.dot_general(
                    jnp.where(sel, q, jnp.zeros_like(q)), kk, (((1,), (1,)), ((), ())),
                    preferred_element_type=F32) + bias

    def row_max(t):
        pi, g = groups[t]
        _, keys, _ = geometry(pi)
        for uu in range(UNROLL):
            b = g * UNROLL + uu
            slot = (t % NSLOT) * UNROLL + uu
            for hh in range(2):
                m = jnp.max(s_scr[slot, hh, :, pl.ds(0, keys)], axis=-1, keepdims=True)
                mb_scr[slot, hh] = jnp.broadcast_to(m, (BLK, LANES))
            m_scr[pi, out_rows(pi, b), :] = jnp.where(lo, mb_scr[slot, 0], mb_scr[slot, 1])

    def exponentials(t):
        pi, g = groups[t]
        _, keys, _ = geometry(pi)
        for uu in range(UNROLL):
            slot = (t % NSLOT) * UNROLL + uu
            for hh in range(2):
                mb = mb_scr[slot, hh]
                mb = mb if keys == BLK else jnp.concatenate([mb, mb], axis=1)
                p_scr[slot, hh, :, pl.ds(0, keys)] = jnp.exp(s_scr[slot, hh, :, pl.ds(0, keys)] - mb).astype(BF16)

    def weighted_values(t):
        pi, g = groups[t]
        _, keys, koff = geometry(pi)
        for uu in range(UNROLL):
            b = g * UNROLL + uu
            slot = (t % NSLOT) * UNROLL + uu
            rk = pl.ds(koff + b * BLK, keys)
            ul0 = jnp.dot(p_scr[slot, 0, :, pl.ds(0, keys)], v0b[pi, rk, :], preferred_element_type=F32)
            ul1 = jnp.dot(p_scr[slot, 1, :, pl.ds(0, keys)], v1b[pi, rk, :], preferred_element_type=F32)
            u_scr[pi, out_rows(pi, b), :] = jnp.where(lo, ul0, ul1)
            l_scr[pi, out_rows(pi, b), :] = pltpu.roll(jnp.where(lo, ul1, ul0), HEAD_DIM, 1)

    for t in range(len(groups) + 3):
        if t - 3 >= 0:
            weighted_values(t - 3)
        if t < len(groups):
            scores(t)
        if 0 <= t - 1 < len(groups):
            row_max(t - 1)
        if 0 <= t - 2 < len(groups):
            exponentials(t - 2)

    def renat(t, carry):
        for ref in (m_scr, l_scr, u_scr):
            ref[3, strided(t), :] = ref[2, pl.ds(t * BLK, BLK), :]
        return carry
    lax.fori_loop(0, nblk, renat, 0)

    def combine(c, carry):
        r = pl.ds(c * BLK, BLK)
        slots = (0, 1, 3)
        ms = [m_scr[s, r, :] for s in slots]
        mx = jnp.maximum(jnp.maximum(ms[0], ms[1]), ms[2])
        num = jnp.zeros((BLK, LANES), F32)
        den = jnp.zeros((BLK, LANES), F32)
        for s, m in zip(slots, ms):
            w = jnp.exp(m - mx)
            num = num + w * u_scr[s, r, :]
            den = den + w * l_scr[s, r, :]
        o_ref[r, :] = ((num / den) * _silu(g_ref[r, :].astype(F32))).astype(o_ref.dtype)
        return carry
    lax.fori_loop(0, nblk, combine, 0, unroll=2)


def _attn_b(z3, gq2, gk2, tabs, casts=()):
    bsz, seq, _ = z3.shape
    col = lambda off: pl.BlockSpec((None, seq, LANES), lambda b, p: (b, 0, off + p))
    f32_rows = pltpu.VMEM((seq, LANES), F32)
    bf16_ops = pltpu.VMEM((N_B_PATTERNS, PAD + seq, LANES), BF16)
    stats = pltpu.VMEM((N_B_PATTERNS + 1, seq, LANES), F32)
    return _call_with_casts(
        functools.partial(_attn_b_kernel, seq=seq),
        grid=(bsz, B_HEADS // 2),
        in_specs=[col(_QB), col(_KB), col(_VB), col(_GB),
                  pl.BlockSpec((1, LANES), lambda b, p: (0, 0)),
                  pl.BlockSpec((1, LANES), lambda b, p: (0, 0)),
                  pl.BlockSpec((N_B_PATTERNS, 2, 2, BLK, 2 * BLK), lambda b, p: (0, p, 0, 0, 0))],
        out_specs=pl.BlockSpec((None, seq, LANES), lambda b, p: (b, 0, p)),
        out_shape=jax.ShapeDtypeStruct((bsz, seq, B_WIDTH), BF16),
        scratch_shapes=[f32_rows] * 6 + [bf16_ops] * 4
                       + [pltpu.VMEM((NSLOT * UNROLL, 2, BLK, 2 * BLK), F32),
                          pltpu.VMEM((NSLOT * UNROLL, 2, BLK, LANES), F32),
                          pltpu.VMEM((NSLOT * UNROLL, 2, BLK, 2 * BLK), BF16)] + [stats] * 3,
        name="attn_b",
        args=(z3, z3, z3, z3, gq2, gk2, tabs),
        casts=casts)


def _even_out_kernel(ya_ref, yb_ref, wa_ref, wb_ref, x_ref, o_ref):
    acc = jnp.dot(ya_ref[...], wa_ref[...], preferred_element_type=F32)
    acc = acc + jnp.dot(yb_ref[...], wb_ref[...], preferred_element_type=F32)
    o_ref[...] = x_ref[...] + acc


def _even_out(ya, yb, w, layer, x2, *, tm=2048, tn=512):
    m, d = x2.shape
    ka, kb = ya.shape[1], yb.shape[1]
    assert ka == kb
    return pl.pallas_call(
        _even_out_kernel,
        grid=(m // tm, d // tn),
        in_specs=[pl.BlockSpec((tm, ka), lambda i, j: (i, 0)),
                  pl.BlockSpec((tm, kb), lambda i, j: (i, 0)),
                  pl.BlockSpec((None, ka, tn), lambda i, j: (layer, 0, j)),
                  pl.BlockSpec((None, kb, tn), lambda i, j: (layer, 1, j)),
                  pl.BlockSpec((tm, tn), lambda i, j: (i, j))],
        out_specs=pl.BlockSpec((tm, tn), lambda i, j: (i, j)),
        out_shape=jax.ShapeDtypeStruct((m, d), F32),
        compiler_params=_params(("arbitrary", "arbitrary")),
        name="even_out",
    )(ya, yb, w, w, x2)


def _odd_out_kernel(y_ref, w_ref, x_ref, o_ref):
    o_ref[...] = x_ref[...] + jnp.dot(y_ref[...], w_ref[...], preferred_element_type=F32)


def _odd_out(y, w, layer, x2, *, tm=1024, tn=512):
    m, d = x2.shape
    k = y.shape[1]
    return pl.pallas_call(
        _odd_out_kernel,
        grid=(m // tm, d // tn),
        in_specs=[pl.BlockSpec((tm, k), lambda i, j: (i, 0)),
                  pl.BlockSpec((None, k, tn), lambda i, j: (layer, 0, j)),
                  pl.BlockSpec((tm, tn), lambda i, j: (i, j))],
        out_specs=pl.BlockSpec((tm, tn), lambda i, j: (i, j)),
        out_shape=jax.ShapeDtypeStruct((m, d), F32),
        compiler_params=_params(("arbitrary", "arbitrary")),
        name="odd_out",
    )(y, w, x2)


def kernel(x, ev_ln_g, ev_w_in, ev_qk_g, ev_sinks, ev_w_out, od_ln_g, od_w_in, od_v_g, od_w_s,
           od_b_s, od_w_out, rel_bias):
    bsz, seq, d = x.shape
    n_even, n_odd = ev_ln_g.shape[0], od_ln_g.shape[0]
    tabs = _bias_tables(rel_bias)
    flat = lambda w: w.reshape(-1, w.shape[-1])
    whole = lambda w: _Cast(flat(w), 0, w.shape[0] * w.shape[1])
    layer_of = lambda w, j: _Cast(flat(w), j * w.shape[1], w.shape[1])
    ev_in_b = {0: ev_w_in[0:1].astype(BF16)}
    od_in_b, ev_out_b, od_out_b = {}, None, None
    x2 = x.reshape(bsz * seq, d)
    for i in range(n_even + n_odd):
        j = i // 2
        if i % 2 == 0:
            z = _even_in(x2, ev_ln_g[j].reshape(1, d), ev_in_b[j], 0)
            z3 = z.reshape(bsz, seq, EVEN_IN)
            gains = jnp.tile(ev_qk_g[j].astype(F32), (1, 2))
            snk3 = jnp.repeat(ev_sinks[j].astype(F32), HEAD_DIM).reshape(A_KV_HEADS, -1, LANES)
            ya, done = _attn_a(z3, gains[0:1], gains[1:2], snk3, tabs, [whole(ev_w_out)] if j == 0 else [])
            if j == 0:
                ev_out_b = done[0].reshape(ev_w_out.shape)
            yb, done = _attn_b(z3, gains[2:3], gains[3:4], tabs, [layer_of(od_w_in, j)] if j < n_odd else [])
            if j < n_odd:
                od_in_b[j] = done[0][None]
            x2 = _even_out(ya.reshape(bsz * seq, A_WIDTH), yb.reshape(bsz * seq, B_WIDTH), ev_out_b, j, x2)
        else:
            casts = ([whole(od_w_out)] if j == 0 else []) + ([layer_of(ev_w_in, j + 1)] if j + 1 < n_even else [])
            y, done = _odd_in(x2, od_ln_g[j].reshape(1, d), od_in_b[j], 0, od_v_g[j].astype(F32),
                              od_w_s[j].astype(F32), od_b_s[j].astype(F32), casts)
            if j == 0:
                od_out_b = done.pop(0).reshape(od_w_out.shape)
            if j + 1 < n_even:
                ev_in_b[j + 1] = done.pop(0)[None]
            x2 = _odd_out(y, od_out_b, j, x2)
    return x2.reshape(bsz, seq, d)
```

```python
import functools
import math
from typing import NamedTuple

import numpy as np
import jax
import jax.numpy as jnp
from jax import lax
from jax.experimental import pallas as pl
from jax.experimental.pallas import tpu as pltpu

F32 = jnp.float32
BF16 = jnp.bfloat16

D_MODEL = 2048
HEAD_DIM = 64
A_HEADS = 16
A_KV_HEADS = 2
B_HEADS = 16
BLK = 128
NUM_BUCKETS = 32
REL_MAX_DISTANCE = 2048
A_WIDTH = A_HEADS * HEAD_DIM
B_WIDTH = B_HEADS * HEAD_DIM
EVEN_IN = 6400
C_WIDTH = 2 * D_MODEL
C_GROUPS = 16
C_GROUP_DIM = C_WIDTH // C_GROUPS
C_CHUNK = 128
EPS = 1e-6
NEG = -1e30
SCALE = HEAD_DIM ** -0.5
LOG2E = 1.4426950408889634
PATTERNS = ((1, 128), (4, 128), (16, 128), (1, 127))
N_B_PATTERNS = 3

LANES = 128
VMEM_LIMIT = 56 * 1024 * 1024

_QA, _KA, _VA, _GA = 0, 8, 9, 10
_QB, _KB, _VB, _GB = 18, 26, 34, 42


def _params(sem):
    return pltpu.CompilerParams(dimension_semantics=sem, vmem_limit_bytes=VMEM_LIMIT)


class _Cast(NamedTuple):
    src: jax.Array
    first_row: int
    n_rows: int


def _call_with_casts(body, *, grid, in_specs, out_specs, out_shape, scratch_shapes, name, args, casts=()):
    n_in, n_cast = len(in_specs), len(casts)
    steps = grid[0] * grid[1]
    cast_in, cast_out, cast_shape = [], [], []
    for c in casts:
        cols = c.src.shape[1]
        tile = c.n_rows // steps
        first = c.first_row // tile
        assert tile * steps == c.n_rows and first * tile == c.first_row and tile % 16 == 0
        cast_in.append(pl.BlockSpec((tile, cols), lambda i, j, first=first: (first + i * grid[1] + j, 0)))
        cast_out.append(pl.BlockSpec((tile, cols), lambda i, j: (i * grid[1] + j, 0)))
        cast_shape.append(jax.ShapeDtypeStruct((c.n_rows, cols), BF16))

    def kernel(*refs):
        ins, srcs = refs[:n_in], refs[n_in:n_in + n_cast]
        out, dsts = refs[n_in + n_cast], refs[n_in + n_cast + 1:n_in + 2 * n_cast + 1]
        for src, dst in zip(srcs, dsts):
            dst[...] = src[...].astype(dst.dtype)
        body(*ins, out, *refs[n_in + 2 * n_cast + 1:])

    res = pl.pallas_call(
        kernel,
        grid=grid,
        in_specs=list(in_specs) + cast_in,
        out_specs=[out_specs] + cast_out,
        out_shape=[out_shape] + cast_shape,
        scratch_shapes=scratch_shapes,
        compiler_params=_params(("arbitrary", "arbitrary")),
        name=name,
    )(*args, *[c.src for c in casts])
    return res[0], list(res[1:])


def _bucket_tables():
    a = np.arange(BLK)[:, None]
    b = np.arange(2 * BLK)[None, :]
    dist = BLK + a - b
    max_exact = NUM_BUCKETS // 2
    out = []
    for dil, max_dist in PATTERNS:
        n = np.maximum(dist * dil, 0)
        large = max_exact + (np.log(np.maximum(n, 1) / max_exact)
                             / np.log(REL_MAX_DISTANCE / max_exact)
                             * (NUM_BUCKETS - max_exact)).astype(np.int32)
        large = np.minimum(large, NUM_BUCKETS - 1)
        bucket = np.where(n < max_exact, n, large).astype(np.int32)
        valid = (dist >= 0) & (dist <= max_dist)
        out.append(np.where(valid, bucket, -1).astype(np.int32))
    return np.stack(out)


def _bias_kernel(tbl_ref, bucket_ref, out_ref):
    col0 = jnp.where(pl.program_id(0) < N_B_PATTERNS, A_HEADS, 0)
    bk = bucket_ref[...]
    prev_cols = lax.broadcasted_iota(jnp.int32, bk.shape, 1) < BLK

    def head(h, carry):
        acc = jnp.full(bk.shape, NEG, F32)
        for b in range(NUM_BUCKETS):
            acc = jnp.where(bk == b, tbl_ref[b, col0 + h] * LOG2E, acc)
        out_ref[h, 0] = acc
        out_ref[h, 1] = jnp.where(prev_cols, NEG, acc)
        return carry
    lax.fori_loop(0, out_ref.shape[0], head, 0)


def _bias_tables(rel_bias):
    buckets = jnp.asarray(_bucket_tables())
    heads = max(A_HEADS, B_HEADS)
    return pl.pallas_call(
        _bias_kernel,
        grid=(len(PATTERNS),),
        in_specs=[pl.BlockSpec(memory_space=pltpu.SMEM),
                  pl.BlockSpec((None, BLK, 2 * BLK), lambda t: (t, 0, 0))],
        out_specs=pl.BlockSpec((None, heads, 2, BLK, 2 * BLK), lambda t: (t, 0, 0, 0, 0)),
        out_shape=jax.ShapeDtypeStruct((len(PATTERNS), heads, 2, BLK, 2 * BLK), F32),
        compiler_params=_params(("arbitrary",)),
        name="bias_tables",
    )(rel_bias.astype(F32), buckets)


def _norm_rows(x_ref, g_ref, h_scr, tm):
    def body(c, carry):
        r = pl.ds(c * 128, 128)
        x = x_ref[r, :]
        ms = jnp.mean(x * x, axis=-1, keepdims=True)
        h_scr[r, :] = (x * lax.rsqrt(ms + EPS) * g_ref[...]).astype(BF16)
        return carry
    lax.fori_loop(0, tm // 128, body, 0, unroll=2)


def _even_in_kernel(x_ref, g_ref, w_ref, o_ref, h_scr, *, tm):
    @pl.when(pl.program_id(1) == 0)
    def _():
        _norm_rows(x_ref, g_ref, h_scr, tm)
    o_ref[...] = jnp.dot(h_scr[...], w_ref[...], preferred_element_type=F32).astype(o_ref.dtype)


def _even_in(x2, g, w, layer, *, tm=1024, tn=1280):
    m, d = x2.shape
    n = w.shape[2]
    return pl.pallas_call(
        functools.partial(_even_in_kernel, tm=tm),
        grid=(m // tm, n // tn),
        in_specs=[pl.BlockSpec((tm, d), lambda i, j: (i, 0)),
                  pl.BlockSpec((1, d), lambda i, j: (0, 0)),
                  pl.BlockSpec((None, d, tn), lambda i, j: (layer, 0, j))],
        out_specs=pl.BlockSpec((tm, tn), lambda i, j: (i, j)),
        out_shape=jax.ShapeDtypeStruct((m, n), BF16),
        scratch_shapes=[pltpu.VMEM((tm, d), BF16)],
        compiler_params=_params(("arbitrary", "arbitrary")),
        name="even_in",
    )(x2, g, w)


def _gelu(x):
    return 0.5 * x * (1.0 + lax.erf(x * (1.0 / math.sqrt(2.0))))


def _silu(x):
    return (0.5 * x) * (1.0 + jnp.tanh(0.5 * x))


def _odd_in_kernel(x_ref, g_ref, wv_ref, wu_ref, wg_ref, vg_ref, ws_ref, bs_ref, y_ref,
                   h_scr, v_scr, ssq_scr, *, tm, tn):
    j = pl.program_id(1)
    nb = C_WIDTH // tn

    @pl.when(j == 0)
    def _():
        _norm_rows(x_ref, g_ref, h_scr, tm)
        ssq_scr[...] = jnp.zeros_like(ssq_scr)

    @pl.when(j < nb)
    def _():
        v = _gelu(jnp.dot(h_scr[...], wv_ref[...], preferred_element_type=F32))
        v_scr[j] = v.astype(v_scr.dtype)
        ssq_scr[...] += jnp.sum(v * v, axis=-1, keepdims=True)

    @pl.when(j >= nb)
    def _():
        jj = j - nb
        h = h_scr[...]
        ug = (_gelu(jnp.dot(h, wu_ref[...], preferred_element_type=F32))
              * _silu(jnp.dot(h, wg_ref[...], preferred_element_type=F32)))
        tril = (lax.broadcasted_iota(jnp.int32, (C_CHUNK, C_CHUNK), 0)
                >= lax.broadcasted_iota(jnp.int32, (C_CHUNK, C_CHUNK), 1))
        groups_per_tile = tn // C_GROUP_DIM
        vg = vg_ref[jj]
        for gg in range(groups_per_tile):
            grp = jj * groups_per_tile + gg
            wt = jnp.where(tril, ws_ref[grp], 0.0).astype(BF16)
            cols = slice(gg * C_GROUP_DIM, (gg + 1) * C_GROUP_DIM)
            for c in range(tm // C_CHUNK):
                rows = slice(c * C_CHUNK, (c + 1) * C_CHUNK)
                inv = lax.rsqrt(ssq_scr[rows, :][:, :1] * (1.0 / C_WIDTH) + EPS)
                vn = (v_scr[jj, rows, cols].astype(F32) * inv * vg[:, cols]).astype(BF16)
                s = jnp.dot(wt, vn, preferred_element_type=F32) + bs_ref[grp]
                y_ref[rows, cols] = (ug[rows, cols] * s).astype(y_ref.dtype)


def _odd_in(x2, g, w, layer, vg, ws, bs, casts=(), *, tm=1024, tn=512):
    m, d = x2.shape
    nb = C_WIDTH // tn
    return _call_with_casts(
        functools.partial(_odd_in_kernel, tm=tm, tn=tn),
        grid=(m // tm, 2 * nb),
        in_specs=[pl.BlockSpec((tm, d), lambda i, j: (i, 0)),
                  pl.BlockSpec((1, d), lambda i, j: (0, 0)),
                  pl.BlockSpec((None, d, tn), lambda i, j: (layer, 0, nb + jnp.minimum(j, nb - 1))),
                  pl.BlockSpec((None, d, tn), lambda i, j: (layer, 0, jnp.maximum(j - nb, 0))),
                  pl.BlockSpec((None, d, tn), lambda i, j: (layer, 0, 2 * nb + jnp.maximum(j - nb, 0))),
                  pl.BlockSpec((nb, 1, tn), lambda i, j: (0, 0, 0)),
                  pl.BlockSpec((C_GROUPS, C_CHUNK, C_CHUNK), lambda i, j: (0, 0, 0)),
                  pl.BlockSpec((C_GROUPS, C_CHUNK, 1), lambda i, j: (0, 0, 0))],
        out_specs=pl.BlockSpec((tm, tn), lambda i, j: (i, jnp.maximum(j - nb, 0))),
        out_shape=jax.ShapeDtypeStruct((m, C_WIDTH), BF16),
        scratch_shapes=[pltpu.VMEM((tm, d), BF16),
                        pltpu.VMEM((nb, tm, tn), BF16),
                        pltpu.VMEM((tm, LANES), F32)],
        name="odd_in",
        args=(x2, g, w, w, w, vg.reshape(nb, 1, tn), ws, bs.reshape(C_GROUPS, C_CHUNK, 1)),
        casts=casts)


PAD = BLK
UNROLL = 4
NSLOT = 3


def _and(b, mask):
    return b & mask if isinstance(b, int) else lax.bitwise_and(b, mask)


def _shr(b, s):
    return b >> s if isinstance(b, int) else lax.shift_right_logical(b, s)


def _lane_lo(rows):
    return lax.broadcasted_iota(jnp.int32, (rows, LANES), 1) < HEAD_DIM


def _head_rms(x, gain):
    w = x.shape[1]
    r = lax.broadcasted_iota(jnp.int32, (w, w), 0) // HEAD_DIM
    c = lax.broadcasted_iota(jnp.int32, (w, w), 1) // HEAD_DIM
    ones_bd = (r == c).astype(BF16)
    x2 = x * x
    hi = x2.astype(BF16)
    lo = (x2 - hi.astype(F32)).astype(BF16)
    ssq = (jnp.dot(hi, ones_bd, preferred_element_type=F32)
           + jnp.dot(lo, ones_bd, preferred_element_type=F32))
    return x * lax.rsqrt(ssq * (1.0 / HEAD_DIM) + EPS) * gain


def _attn_a_kernel(q0_ref, q1_ref, q2_ref, q3_ref, k_ref, v_ref, g0_ref, g1_ref, g2_ref, g3_ref,
                   gq_ref, gk_ref, snk_ref, bias_ref, o_ref, qb, kb, v0b, v1b, s_scr, mb_scr, p_scr, m_scr, *, seq):
    q_refs = (q0_ref, q1_ref, q2_ref, q3_ref)
    g_refs = (g0_ref, g1_ref, g2_ref, g3_ref)
    npair = len(q_refs)
    nblk = seq // BLK
    lane = lax.broadcasted_iota(jnp.int32, (BLK, LANES), 1)
    keep = (lane // HEAD_DIM) == pl.program_id(1)
    lo = lane < HEAD_DIM
    hi = jnp.logical_not(lo)
    zeros = jnp.zeros((PAD, LANES), BF16)
    kb[pl.ds(0, PAD), :] = zeros
    v0b[pl.ds(0, PAD), :] = zeros
    v1b[pl.ds(0, PAD), :] = zeros
    gq = gq_ref[...] * (SCALE * LOG2E)
    gq = jnp.concatenate([gq, gq], axis=1)

    def prep(c, carry):
        r = pl.ds(c * BLK, BLK)
        ro = pl.ds(PAD + c * BLK, BLK)
        for pp in range(0, npair, 2):
            qq = jnp.concatenate([q_refs[pp][r, :], q_refs[pp + 1][r, :]], axis=1).astype(F32)
            qn = _head_rms(qq, gq).astype(BF16)
            qb[pp, r, :] = qn[:, :LANES]
            qb[pp + 1, r, :] = qn[:, LANES:]
        kn = _head_rms(k_ref[r, :].astype(F32), gk_ref[...])
        kb[ro, :] = jnp.where(keep, kn, pltpu.roll(kn, HEAD_DIM, 1)).astype(BF16)
        v = v_ref[r, :].astype(F32)
        vd = jnp.where(keep, v, pltpu.roll(v, HEAD_DIM, 1))
        v0b[ro, :] = jnp.where(lo, vd, 1.0).astype(BF16)
        v1b[ro, :] = jnp.where(lo, 1.0, vd).astype(BF16)
        return carry
    lax.fori_loop(0, nblk, prep, 0, unroll=4)

    def scores(n):
        kk = kb[pl.ds(n * BLK, 2 * BLK), :]
        for pp in range(npair):
            q = qb[pp, pl.ds(n * BLK, BLK), :]
            for hh, sel in enumerate((lo, hi)):
                s_scr[n % NSLOT, pp, hh] = lax.dot_general(
                    jnp.where(sel, q, jnp.zeros_like(q)), kk, (((1,), (1,)), ((), ())),
                    preferred_element_type=F32) + bias_ref[2 * pp + hh, int(n == 0)]

    def row_max(n):
        slot = n % NSLOT
        for pp in range(npair):
            for hh in range(2):
                m = jnp.max(s_scr[slot, pp, hh], axis=-1, keepdims=True)
                mb_scr[slot, pp, hh] = jnp.broadcast_to(m, (BLK, LANES))
            m_scr[slot, pp] = jnp.where(lo, mb_scr[slot, pp, 0], mb_scr[slot, pp, 1])

    def exponentials(n):
        slot = n % NSLOT
        for pp in range(npair):
            for hh in range(2):
                mb = mb_scr[slot, pp, hh]
                p_scr[slot, pp, hh] = jnp.exp2(s_scr[slot, pp, hh] - jnp.concatenate([mb, mb], axis=1)).astype(BF16)

    def weighted_values(n):
        slot = n % NSLOT
        r = pl.ds(n * BLK, BLK)
        rk = pl.ds(n * BLK, 2 * BLK)
        for pp in range(npair):
            ul0 = jnp.dot(p_scr[slot, pp, 0], v0b[rk, :], preferred_element_type=F32)
            ul1 = jnp.dot(p_scr[slot, pp, 1], v1b[rk, :], preferred_element_type=F32)
            u = jnp.where(lo, ul0, ul1)
            l = pltpu.roll(jnp.where(lo, ul1, ul0), HEAD_DIM, 1)
            m = m_scr[slot, pp]
            snk = snk_ref[pp:pp + 1, :] * LOG2E
            mx = jnp.maximum(m, snk)
            w = jnp.exp2(m - mx)
            o = u * (w / (l * w + jnp.exp2(snk - mx)))
            o_ref[r, pl.ds(pp * LANES, LANES)] = (o * _silu(g_refs[pp][r, :].astype(F32))).astype(o_ref.dtype)

    for n in range(nblk + 3):
        if n - 3 >= 0:
            weighted_values(n - 3)
        if n < nblk:
            scores(n)
        if 0 <= n - 1 < nblk:
            row_max(n - 1)
        if 0 <= n - 2 < nblk:
            exponentials(n - 2)


def _attn_a(z3, gq2, gk2, snk3, tabs, casts=()):
    bsz, seq, _ = z3.shape
    npair = A_HEADS // A_KV_HEADS // 2

    def col(off, pp):
        return pl.BlockSpec((None, seq, LANES), lambda b, gi: (b, 0, off + npair * gi + pp))

    def fixed(off):
        return pl.BlockSpec((None, seq, LANES), lambda b, gi: (b, 0, off))

    return _call_with_casts(
        functools.partial(_attn_a_kernel, seq=seq),
        grid=(bsz, A_KV_HEADS),
        in_specs=[col(_QA, 0), col(_QA, 1), col(_QA, 2), col(_QA, 3), fixed(_KA), fixed(_VA),
                  col(_GA, 0), col(_GA, 1), col(_GA, 2), col(_GA, 3),
                  pl.BlockSpec((1, LANES), lambda b, gi: (0, 0)),
                  pl.BlockSpec((1, LANES), lambda b, gi: (0, 0)),
                  pl.BlockSpec((None, npair, LANES), lambda b, gi: (gi, 0, 0)),
                  pl.BlockSpec((None, 2 * npair, 2, BLK, 2 * BLK), lambda b, gi: (N_B_PATTERNS, gi, 0, 0, 0))],
        out_specs=pl.BlockSpec((None, seq, npair * LANES), lambda b, gi: (b, 0, gi)),
        out_shape=jax.ShapeDtypeStruct((bsz, seq, A_WIDTH), BF16),
        scratch_shapes=[pltpu.VMEM((npair, seq, LANES), BF16),
                        pltpu.VMEM((PAD + seq, LANES), BF16),
                        pltpu.VMEM((PAD + seq, LANES), BF16),
                        pltpu.VMEM((PAD + seq, LANES), BF16),
                        pltpu.VMEM((NSLOT, npair, 2, BLK, 2 * BLK), F32),
                        pltpu.VMEM((NSLOT, npair, 2, BLK, LANES), F32),
                        pltpu.VMEM((NSLOT, npair, 2, BLK, 2 * BLK), BF16),
                        pltpu.VMEM((NSLOT, npair, BLK, LANES), F32)],
        name="attn_a",
        args=(z3, z3, z3, z3, z3, z3, z3, z3, z3, z3, gq2, gk2, snk3, tabs),
        casts=casts)


def _attn_b_kernel(q_ref, k_ref, v_ref, g_ref, gq_ref, gk_ref, bias_ref, o_ref,
                   qf, kf, vf, q4f, k4f, v4f, qb, kb, v0b, v1b, s_scr, mb_scr, p_scr, m_scr, l_scr, u_scr, *, seq):
    lo = _lane_lo(BLK)
    hi = jnp.logical_not(lo)
    nblk = seq // BLK
    zeros = jnp.zeros((PAD, LANES), BF16)
    for pi in range(2):
        kb[pi, pl.ds(0, PAD), :] = zeros
        v0b[pi, pl.ds(0, PAD), :] = zeros
        v1b[pi, pl.ds(0, PAD), :] = zeros

    def put(pi, dst, q, k, v):
        qb[pi, dst, :] = q.astype(BF16)
        kb[pi, dst, :] = k.astype(BF16)
        v0b[pi, dst, :] = jnp.where(lo, v, 1.0).astype(BF16)
        v1b[pi, dst, :] = jnp.where(lo, 1.0, v).astype(BF16)

    def strided(t):
        return pl.ds(_and(t, 3) * (4 * BLK) + _shr(t, 2), BLK, stride=4)

    gains = jnp.concatenate([gq_ref[...] * (SCALE * LOG2E), gk_ref[...]], axis=1)

    def prep(c, carry):
        r = pl.ds(c * BLK, BLK)
        qk = jnp.concatenate([q_ref[r, :], k_ref[r, :]], axis=1).astype(F32)
        n = _head_rms(qk, gains)
        q, k, v = n[:, :LANES], n[:, LANES:], v_ref[r, :].astype(F32)
        qf[r, :] = q
        kf[r, :] = k
        vf[r, :] = v
        put(0, pl.ds(PAD + c * BLK, BLK), q, k, v)
        return carry
    for c in range(nblk):
        prep(c, 0)

    def deint4(t, carry):
        src = strided(t)
        dst = pl.ds(t * BLK, BLK)
        q, k, v = qf[src, :], kf[src, :], vf[src, :]
        q4f[dst, :] = q
        k4f[dst, :] = k
        v4f[dst, :] = v
        put(1, pl.ds(PAD + t * BLK, BLK), q, k, v)
        return carry
    for t in range(nblk):
        deint4(t, 0)

    def deint16(t, carry):
        src = strided(t)
        put(2, pl.ds(PAD + t * BLK, BLK), q4f[src, :], k4f[src, :], v4f[src, :])
        return carry
    for t in range(nblk):
        deint16(t, 0)

    groups = [(pi, g) for pi in range(N_B_PATTERNS) for g in range(nblk // UNROLL)]

    def geometry(pi):
        single = pi == 2
        return single, (BLK if single else 2 * BLK), (PAD if single else PAD - BLK)

    def out_rows(pi, b):
        return pl.ds(b * BLK, BLK) if pi == 0 else strided(b)

    def scores(t):
        pi, g = groups[t]
        single, keys, koff = geometry(pi)
        for uu in range(UNROLL):
            b = g * UNROLL + uu
            first = int(b % (nblk if pi == 0 else 4) == 0)
            q = qb[pi, pl.ds(PAD + b * BLK, BLK), :]
            kk = kb[pi, pl.ds(koff + b * BLK, keys), :]
            for hh, sel in enumerate((lo, hi)):
                bias = bias_ref[pi, hh, 0, :, pl.ds(BLK, BLK)] if single else bias_ref[pi, hh, first]
                s_scr[(t % NSLOT) * UNROLL + uu, hh, :, pl.ds(0, keys)] = lax.dot_general(
                    jnp.where(sel, q, jnp.zeros_like(q)), kk, (((1,), (1,)), ((), ())),
                    preferred_element_type=F32) + bias

    def row_max(t):
        pi, g = groups[t]
        _, keys, _ = geometry(pi)
        for uu in range(UNROLL):
            b = g * UNROLL + uu
            slot = (t % NSLOT) * UNROLL + uu
            for hh in range(2):
                m = jnp.max(s_scr[slot, hh, :, pl.ds(0, keys)], axis=-1, keepdims=True)
                mb_scr[slot, hh] = jnp.broadcast_to(m, (BLK, LANES))
            m_scr[pi, out_rows(pi, b), :] = jnp.where(lo, mb_scr[slot, 0], mb_scr[slot, 1])

    def exponentials(t):
        pi, g = groups[t]
        _, keys, _ = geometry(pi)
        for uu in range(UNROLL):
            slot = (t % NSLOT) * UNROLL + uu
            for hh in range(2):
                mb = mb_scr[slot, hh]
                mb = mb if keys == BLK else jnp.concatenate([mb, mb], axis=1)
                p_scr[slot, hh, :, pl.ds(0, keys)] = jnp.exp2(s_scr[slot, hh, :, pl.ds(0, keys)] - mb).astype(BF16)

    def weighted_values(t):
        pi, g = groups[t]
        _, keys, koff = geometry(pi)
        for uu in range(UNROLL):
            b = g * UNROLL + uu
            slot = (t % NSLOT) * UNROLL + uu
            rk = pl.ds(koff + b * BLK, keys)
            ul0 = jnp.dot(p_scr[slot, 0, :, pl.ds(0, keys)], v0b[pi, rk, :], preferred_element_type=F32)
            ul1 = jnp.dot(p_scr[slot, 1, :, pl.ds(0, keys)], v1b[pi, rk, :], preferred_element_type=F32)
            u_scr[pi, out_rows(pi, b), :] = jnp.where(lo, ul0, ul1)
            l_scr[pi, out_rows(pi, b), :] = pltpu.roll(jnp.where(lo, ul1, ul0), HEAD_DIM, 1)

    for t in range(len(groups) + 3):
        if t - 3 >= 0:
            weighted_values(t - 3)
        if t < len(groups):
            scores(t)
        if 0 <= t - 1 < len(groups):
            row_max(t - 1)
        if 0 <= t - 2 < len(groups):
            exponentials(t - 2)

    def renat(t, carry):
        for ref in (m_scr, l_scr, u_scr):
            ref[3, strided(t), :] = ref[2, pl.ds(t * BLK, BLK), :]
        return carry
    lax.fori_loop(0, nblk, renat, 0)

    def combine(c, carry):
        r = pl.ds(c * BLK, BLK)
        slots = (0, 1, 3)
        ms = [m_scr[s, r, :] for s in slots]
        mx = jnp.maximum(jnp.maximum(ms[0], ms[1]), ms[2])
        num = jnp.zeros((BLK, LANES), F32)
        den = jnp.zeros((BLK, LANES), F32)
        for s, m in zip(slots, ms):
            w = jnp.exp2(m - mx)
            num = num + w * u_scr[s, r, :]
            den = den + w * l_scr[s, r, :]
        o_ref[r, :] = ((num / den) * _silu(g_ref[r, :].astype(F32))).astype(o_ref.dtype)
        return carry
    lax.fori_loop(0, nblk, combine, 0, unroll=2)


def _attn_b(z3, gq2, gk2, tabs, casts=()):
    bsz, seq, _ = z3.shape
    col = lambda off: pl.BlockSpec((None, seq, LANES), lambda b, p: (b, 0, off + p))
    f32_rows = pltpu.VMEM((seq, LANES), F32)
    bf16_ops = pltpu.VMEM((N_B_PATTERNS, PAD + seq, LANES), BF16)
    stats = pltpu.VMEM((N_B_PATTERNS + 1, seq, LANES), F32)
    return _call_with_casts(
        functools.partial(_attn_b_kernel, seq=seq),
        grid=(bsz, B_HEADS // 2),
        in_specs=[col(_QB), col(_KB), col(_VB), col(_GB),
                  pl.BlockSpec((1, LANES), lambda b, p: (0, 0)),
                  pl.BlockSpec((1, LANES), lambda b, p: (0, 0)),
                  pl.BlockSpec((N_B_PATTERNS, 2, 2, BLK, 2 * BLK), lambda b, p: (0, p, 0, 0, 0))],
        out_specs=pl.BlockSpec((None, seq, LANES), lambda b, p: (b, 0, p)),
        out_shape=jax.ShapeDtypeStruct((bsz, seq, B_WIDTH), BF16),
        scratch_shapes=[f32_rows] * 6 + [bf16_ops] * 4
                       + [pltpu.VMEM((NSLOT * UNROLL, 2, BLK, 2 * BLK), F32),
                          pltpu.VMEM((NSLOT * UNROLL, 2, BLK, LANES), F32),
                          pltpu.VMEM((NSLOT * UNROLL, 2, BLK, 2 * BLK), BF16)] + [stats] * 3,
        name="attn_b",
        args=(z3, z3, z3, z3, gq2, gk2, tabs),
        casts=casts)


def _even_out_kernel(ya_ref, yb_ref, wa_ref, wb_ref, x_ref, o_ref):
    acc = jnp.dot(ya_ref[...], wa_ref[...], preferred_element_type=F32)
    acc = acc + jnp.dot(yb_ref[...], wb_ref[...], preferred_element_type=F32)
    o_ref[...] = x_ref[...] + acc


def _even_out(ya, yb, w, layer, x2, *, tm=2048, tn=512):
    m, d = x2.shape
    ka, kb = ya.shape[1], yb.shape[1]
    assert ka == kb
    return pl.pallas_call(
        _even_out_kernel,
        grid=(m // tm, d // tn),
        in_specs=[pl.BlockSpec((tm, ka), lambda i, j: (i, 0)),
                  pl.BlockSpec((tm, kb), lambda i, j: (i, 0)),
                  pl.BlockSpec((None, ka, tn), lambda i, j: (layer, 0, j)),
                  pl.BlockSpec((None, kb, tn), lambda i, j: (layer, 1, j)),
                  pl.BlockSpec((tm, tn), lambda i, j: (i, j))],
        out_specs=pl.BlockSpec((tm, tn), lambda i, j: (i, j)),
        out_shape=jax.ShapeDtypeStruct((m, d), F32),
        compiler_params=_params(("arbitrary", "arbitrary")),
        name="even_out",
    )(ya, yb, w, w, x2)


def _odd_out_kernel(y_ref, w_ref, x_ref, o_ref):
    o_ref[...] = x_ref[...] + jnp.dot(y_ref[...], w_ref[...], preferred_element_type=F32)


def _odd_out(y, w, layer, x2, *, tm=1024, tn=1024):
    m, d = x2.shape
    k = y.shape[1]
    return pl.pallas_call(
        _odd_out_kernel,
        grid=(m // tm, d // tn),
        in_specs=[pl.BlockSpec((tm, k), lambda i, j: (i, 0)),
                  pl.BlockSpec((None, k, tn), lambda i, j: (layer, 0, j)),
                  pl.BlockSpec((tm, tn), lambda i, j: (i, j))],
        out_specs=pl.BlockSpec((tm, tn), lambda i, j: (i, j)),
        out_shape=jax.ShapeDtypeStruct((m, d), F32),
        compiler_params=_params(("arbitrary", "arbitrary")),
        name="odd_out",
    )(y, w, x2)


def kernel(x, ev_ln_g, ev_w_in, ev_qk_g, ev_sinks, ev_w_out, od_ln_g, od_w_in, od_v_g, od_w_s,
           od_b_s, od_w_out, rel_bias):
    bsz, seq, d = x.shape
    n_even, n_odd = ev_ln_g.shape[0], od_ln_g.shape[0]
    tabs = _bias_tables(rel_bias)
    flat = lambda w: w.reshape(-1, w.shape[-1])
    whole = lambda w: _Cast(flat(w), 0, w.shape[0] * w.shape[1])
    layer_of = lambda w, j: _Cast(flat(w), j * w.shape[1], w.shape[1])
    ev_in_b = {0: ev_w_in[0:1].astype(BF16)}
    od_in_b, ev_out_b, od_out_b = {}, None, None
    x2 = x.reshape(bsz * seq, d)
    for i in range(n_even + n_odd):
        j = i // 2
        if i % 2 == 0:
            z = _even_in(x2, ev_ln_g[j].reshape(1, d), ev_in_b[j], 0)
            z3 = z.reshape(bsz, seq, EVEN_IN)
            gains = jnp.tile(ev_qk_g[j].astype(F32), (1, 2))
            snk3 = jnp.repeat(ev_sinks[j].astype(F32), HEAD_DIM).reshape(A_KV_HEADS, -1, LANES)
            ya, done = _attn_a(z3, gains[0:1], gains[1:2], snk3, tabs, [whole(ev_w_out)] if j == 0 else [])
            if j == 0:
                ev_out_b = done[0].reshape(ev_w_out.shape)
            yb, done = _attn_b(z3, gains[2:3], gains[3:4], tabs, [layer_of(od_w_in, j)] if j < n_odd else [])
            if j < n_odd:
                od_in_b[j] = done[0][None]
            x2 = _even_out(ya.reshape(bsz * seq, A_WIDTH), yb.reshape(bsz * seq, B_WIDTH), ev_out_b, j, x2)
        else:
            casts = ([whole(od_w_out)] if j == 0 else []) + ([layer_of(ev_w_in, j + 1)] if j + 1 < n_even else [])
            y, done = _odd_in(x2, od_ln_g[j].reshape(1, d), od_in_b[j], 0, od_v_g[j].astype(F32),
                              od_w_s[j].astype(F32), od_b_s[j].astype(F32), casts)
            if j == 0:
                od_out_b = done.pop(0).reshape(od_w_out.shape)
            if j + 1 < n_even:
                ev_in_b[j + 1] = done.pop(0)[None]
            x2 = _odd_out(y, od_out_b, j, x2)
    return x2.reshape(bsz, seq, d)
```

```python
import functools
import math
from typing import NamedTuple

import numpy as np
import jax
import jax.numpy as jnp
from jax import lax
from jax.experimental import pallas as pl
from jax.experimental.pallas import tpu as pltpu

F32 = jnp.float32
BF16 = jnp.bfloat16

D_MODEL = 2048
HEAD_DIM = 64
A_HEADS = 16
A_KV_HEADS = 2
B_HEADS = 16
BLK = 128
NUM_BUCKETS = 32
REL_MAX_DISTANCE = 2048
A_WIDTH = A_HEADS * HEAD_DIM
B_WIDTH = B_HEADS * HEAD_DIM
EVEN_IN = 6400
C_WIDTH = 2 * D_MODEL
C_GROUPS = 16
C_GROUP_DIM = C_WIDTH // C_GROUPS
C_CHUNK = 128
EPS = 1e-6
NEG = -1e30
SCALE = HEAD_DIM ** -0.5
LOG2E = 1.4426950408889634
PATTERNS = ((1, 128), (4, 128), (16, 128), (1, 127))
N_B_PATTERNS = 3

LANES = 128
VMEM_LIMIT = 56 * 1024 * 1024

_QA, _KA, _VA, _GA = 0, 8, 9, 10
_QB, _KB, _VB, _GB = 18, 26, 34, 42


def _params(sem):
    return pltpu.CompilerParams(dimension_semantics=sem, vmem_limit_bytes=VMEM_LIMIT)


class _Cast(NamedTuple):
    src: jax.Array
    first_row: int
    n_rows: int


def _call_with_casts(body, *, grid, in_specs, out_specs, out_shape, scratch_shapes, name, args, casts=()):
    n_in, n_cast = len(in_specs), len(casts)
    steps = grid[0] * grid[1]
    cast_in, cast_out, cast_shape = [], [], []
    for c in casts:
        cols = c.src.shape[1]
        tile = c.n_rows // steps
        first = c.first_row // tile
        assert tile * steps == c.n_rows and first * tile == c.first_row and tile % 16 == 0
        cast_in.append(pl.BlockSpec((tile, cols), lambda i, j, first=first: (first + i * grid[1] + j, 0)))
        cast_out.append(pl.BlockSpec((tile, cols), lambda i, j: (i * grid[1] + j, 0)))
        cast_shape.append(jax.ShapeDtypeStruct((c.n_rows, cols), BF16))

    def kernel(*refs):
        ins, srcs = refs[:n_in], refs[n_in:n_in + n_cast]
        out, dsts = refs[n_in + n_cast], refs[n_in + n_cast + 1:n_in + 2 * n_cast + 1]
        for src, dst in zip(srcs, dsts):
            dst[...] = src[...].astype(dst.dtype)
        body(*ins, out, *refs[n_in + 2 * n_cast + 1:])

    res = pl.pallas_call(
        kernel,
        grid=grid,
        in_specs=list(in_specs) + cast_in,
        out_specs=[out_specs] + cast_out,
        out_shape=[out_shape] + cast_shape,
        scratch_shapes=scratch_shapes,
        compiler_params=_params(("arbitrary", "arbitrary")),
        name=name,
    )(*args, *[c.src for c in casts])
    return res[0], list(res[1:])


def _bucket_tables():
    a = np.arange(BLK)[:, None]
    b = np.arange(2 * BLK)[None, :]
    dist = BLK + a - b
    max_exact = NUM_BUCKETS // 2
    out = []
    for dil, max_dist in PATTERNS:
        n = np.maximum(dist * dil, 0)
        large = max_exact + (np.log(np.maximum(n, 1) / max_exact)
                             / np.log(REL_MAX_DISTANCE / max_exact)
                             * (NUM_BUCKETS - max_exact)).astype(np.int32)
        large = np.minimum(large, NUM_BUCKETS - 1)
        bucket = np.where(n < max_exact, n, large).astype(np.int32)
        valid = (dist >= 0) & (dist <= max_dist)
        out.append(np.where(valid, bucket, -1).astype(np.int32))
    return np.stack(out)


def _bias_kernel(tbl_ref, bucket_ref, out_ref):
    col0 = jnp.where(pl.program_id(0) < N_B_PATTERNS, A_HEADS, 0)
    bk = bucket_ref[...]
    prev_cols = lax.broadcasted_iota(jnp.int32, bk.shape, 1) < BLK

    def head(h, carry):
        acc = jnp.full(bk.shape, NEG, F32)
        for b in range(NUM_BUCKETS):
            acc = jnp.where(bk == b, tbl_ref[b, col0 + h] * LOG2E, acc)
        out_ref[h, 0] = acc
        out_ref[h, 1] = jnp.where(prev_cols, NEG, acc)
        return carry
    lax.fori_loop(0, out_ref.shape[0], head, 0)


def _bias_tables(rel_bias, casts=()):
    buckets = jnp.asarray(_bucket_tables())
    heads = max(A_HEADS, B_HEADS)
    return _call_with_casts(
        _bias_kernel,
        grid=(len(PATTERNS), 1),
        in_specs=[pl.BlockSpec(memory_space=pltpu.SMEM),
                  pl.BlockSpec((None, BLK, 2 * BLK), lambda t, _: (t, 0, 0))],
        out_specs=pl.BlockSpec((None, heads, 2, BLK, 2 * BLK), lambda t, _: (t, 0, 0, 0, 0)),
        out_shape=jax.ShapeDtypeStruct((len(PATTERNS), heads, 2, BLK, 2 * BLK), F32),
        scratch_shapes=[],
        name="bias_tables",
        args=(rel_bias.astype(F32), buckets),
        casts=casts)


def _norm_rows(x_ref, g_ref, h_scr, tm):
    def body(c, carry):
        r = pl.ds(c * 128, 128)
        x = x_ref[r, :]
        ms = jnp.mean(x * x, axis=-1, keepdims=True)
        h_scr[r, :] = (x * lax.rsqrt(ms + EPS) * g_ref[...]).astype(BF16)
        return carry
    lax.fori_loop(0, tm // 128, body, 0, unroll=2)


def _even_in_kernel(x_ref, g_ref, w_ref, o_ref, h_scr, *, tm):
    @pl.when(pl.program_id(1) == 0)
    def _():
        _norm_rows(x_ref, g_ref, h_scr, tm)
    o_ref[...] = jnp.dot(h_scr[...], w_ref[...], preferred_element_type=F32).astype(o_ref.dtype)


def _even_in(x2, g, w, layer, *, tm=1024, tn=1280):
    m, d = x2.shape
    n = w.shape[2]
    return pl.pallas_call(
        functools.partial(_even_in_kernel, tm=tm),
        grid=(m // tm, n // tn),
        in_specs=[pl.BlockSpec((tm, d), lambda i, j: (i, 0)),
                  pl.BlockSpec((1, d), lambda i, j: (0, 0)),
                  pl.BlockSpec((None, d, tn), lambda i, j: (layer, 0, j))],
        out_specs=pl.BlockSpec((tm, tn), lambda i, j: (i, j)),
        out_shape=jax.ShapeDtypeStruct((m, n), BF16),
        scratch_shapes=[pltpu.VMEM((tm, d), BF16)],
        compiler_params=_params(("arbitrary", "arbitrary")),
        name="even_in",
    )(x2, g, w)


def _gelu(x):
    return 0.5 * x * (1.0 + lax.erf(x * (1.0 / math.sqrt(2.0))))


def _silu(x):
    return (0.5 * x) * (1.0 + jnp.tanh(0.5 * x))


def _odd_in_kernel(x_ref, g_ref, wv_ref, wu_ref, wg_ref, vg_ref, ws_ref, bs_ref, y_ref,
                   h_scr, v_scr, ssq_scr, *, tm, tn):
    j = pl.program_id(1)
    nb = C_WIDTH // tn

    @pl.when(j == 0)
    def _():
        _norm_rows(x_ref, g_ref, h_scr, tm)
        ssq_scr[...] = jnp.zeros_like(ssq_scr)

    @pl.when(j < nb)
    def _():
        v = _gelu(jnp.dot(h_scr[...], wv_ref[...], preferred_element_type=F32))
        v_scr[j] = v.astype(v_scr.dtype)
        ssq_scr[...] += jnp.sum(v * v, axis=-1, keepdims=True)

    @pl.when(j >= nb)
    def _():
        jj = j - nb
        h = h_scr[...]
        ug = (_gelu(jnp.dot(h, wu_ref[...], preferred_element_type=F32))
              * _silu(jnp.dot(h, wg_ref[...], preferred_element_type=F32)))
        tril = (lax.broadcasted_iota(jnp.int32, (C_CHUNK, C_CHUNK), 0)
                >= lax.broadcasted_iota(jnp.int32, (C_CHUNK, C_CHUNK), 1))
        groups_per_tile = tn // C_GROUP_DIM
        vg = vg_ref[jj]
        for gg in range(groups_per_tile):
            grp = jj * groups_per_tile + gg
            wt = jnp.where(tril, ws_ref[grp], 0.0).astype(BF16)
            cols = slice(gg * C_GROUP_DIM, (gg + 1) * C_GROUP_DIM)
            for c in range(tm // C_CHUNK):
                rows = slice(c * C_CHUNK, (c + 1) * C_CHUNK)
                inv = lax.rsqrt(ssq_scr[rows, :][:, :1] * (1.0 / C_WIDTH) + EPS)
                vn = (v_scr[jj, rows, cols].astype(F32) * inv * vg[:, cols]).astype(BF16)
                s = jnp.dot(wt, vn, preferred_element_type=F32) + bs_ref[grp]
                y_ref[rows, cols] = (ug[rows, cols] * s).astype(y_ref.dtype)


def _odd_in(x2, g, w, layer, vg, ws, bs, casts=(), *, tm=1024, tn=512):
    m, d = x2.shape
    nb = C_WIDTH // tn
    return _call_with_casts(
        functools.partial(_odd_in_kernel, tm=tm, tn=tn),
        grid=(m // tm, 2 * nb),
        in_specs=[pl.BlockSpec((tm, d), lambda i, j: (i, 0)),
                  pl.BlockSpec((1, d), lambda i, j: (0, 0)),
                  pl.BlockSpec((None, d, tn), lambda i, j: (layer, 0, nb + jnp.minimum(j, nb - 1))),
                  pl.BlockSpec((None, d, tn), lambda i, j: (layer, 0, jnp.maximum(j - nb, 0))),
                  pl.BlockSpec((None, d, tn), lambda i, j: (layer, 0, 2 * nb + jnp.maximum(j - nb, 0))),
                  pl.BlockSpec((nb, 1, tn), lambda i, j: (0, 0, 0)),
                  pl.BlockSpec((C_GROUPS, C_CHUNK, C_CHUNK), lambda i, j: (0, 0, 0)),
                  pl.BlockSpec((C_GROUPS, C_CHUNK, 1), lambda i, j: (0, 0, 0))],
        out_specs=pl.BlockSpec((tm, tn), lambda i, j: (i, jnp.maximum(j - nb, 0))),
        out_shape=jax.ShapeDtypeStruct((m, C_WIDTH), BF16),
        scratch_shapes=[pltpu.VMEM((tm, d), BF16),
                        pltpu.VMEM((nb, tm, tn), BF16),
                        pltpu.VMEM((tm, LANES), F32)],
        name="odd_in",
        args=(x2, g, w, w, w, vg.reshape(nb, 1, tn), ws, bs.reshape(C_GROUPS, C_CHUNK, 1)),
        casts=casts)


PAD = BLK
UNROLL = 4
NSLOT = 3


def _and(b, mask):
    return b & mask if isinstance(b, int) else lax.bitwise_and(b, mask)


def _shr(b, s):
    return b >> s if isinstance(b, int) else lax.shift_right_logical(b, s)


def _lane_lo(rows):
    return lax.broadcasted_iota(jnp.int32, (rows, LANES), 1) < HEAD_DIM


def _head_rms(x, gain):
    w = x.shape[1]
    r = lax.broadcasted_iota(jnp.int32, (w, w), 0) // HEAD_DIM
    c = lax.broadcasted_iota(jnp.int32, (w, w), 1) // HEAD_DIM
    ones_bd = (r == c).astype(BF16)
    x2 = x * x
    hi = x2.astype(BF16)
    lo = (x2 - hi.astype(F32)).astype(BF16)
    ssq = (jnp.dot(hi, ones_bd, preferred_element_type=F32)
           + jnp.dot(lo, ones_bd, preferred_element_type=F32))
    return x * lax.rsqrt(ssq * (1.0 / HEAD_DIM) + EPS) * gain


def _attn_a_kernel(q0_ref, q1_ref, q2_ref, q3_ref, k_ref, v_ref, g0_ref, g1_ref, g2_ref, g3_ref,
                   gq_ref, gk_ref, snk_ref, bias_ref, o_ref, qb, kb, v0b, v1b, s_scr, mb_scr, p_scr, m_scr, *, seq):
    q_refs = (q0_ref, q1_ref, q2_ref, q3_ref)
    g_refs = (g0_ref, g1_ref, g2_ref, g3_ref)
    npair = len(q_refs)
    nblk = seq // BLK
    lane = lax.broadcasted_iota(jnp.int32, (BLK, LANES), 1)
    keep = (lane // HEAD_DIM) == pl.program_id(1)
    lo = lane < HEAD_DIM
    hi = jnp.logical_not(lo)
    zeros = jnp.zeros((PAD, LANES), BF16)
    kb[pl.ds(0, PAD), :] = zeros
    v0b[pl.ds(0, PAD), :] = zeros
    v1b[pl.ds(0, PAD), :] = zeros
    gq = gq_ref[...] * (SCALE * LOG2E)
    gq = jnp.concatenate([gq, gq], axis=1)

    def prep(c, carry):
        r = pl.ds(c * BLK, BLK)
        ro = pl.ds(PAD + c * BLK, BLK)
        for pp in range(0, npair, 2):
            qq = jnp.concatenate([q_refs[pp][r, :], q_refs[pp + 1][r, :]], axis=1).astype(F32)
            qn = _head_rms(qq, gq).astype(BF16)
            qb[pp, r, :] = qn[:, :LANES]
            qb[pp + 1, r, :] = qn[:, LANES:]
        kn = _head_rms(k_ref[r, :].astype(F32), gk_ref[...])
        kb[ro, :] = jnp.where(keep, kn, pltpu.roll(kn, HEAD_DIM, 1)).astype(BF16)
        v = v_ref[r, :].astype(F32)
        vd = jnp.where(keep, v, pltpu.roll(v, HEAD_DIM, 1))
        v0b[ro, :] = jnp.where(lo, vd, 1.0).astype(BF16)
        v1b[ro, :] = jnp.where(lo, 1.0, vd).astype(BF16)
        return carry
    lax.fori_loop(0, nblk, prep, 0, unroll=4)

    def scores(n):
        kk = kb[pl.ds(n * BLK, 2 * BLK), :]
        for pp in range(npair):
            q = qb[pp, pl.ds(n * BLK, BLK), :]
            for hh, sel in enumerate((lo, hi)):
                s_scr[n % NSLOT, pp, hh] = lax.dot_general(
                    jnp.where(sel, q, jnp.zeros_like(q)), kk, (((1,), (1,)), ((), ())),
                    preferred_element_type=F32) + bias_ref[2 * pp + hh, int(n == 0)]

    def row_max(n):
        slot = n % NSLOT
        for pp in range(npair):
            for hh in range(2):
                m = jnp.max(s_scr[slot, pp, hh], axis=-1, keepdims=True)
                mb_scr[slot, pp, hh] = jnp.broadcast_to(m, (BLK, LANES))
            m_scr[slot, pp] = jnp.where(lo, mb_scr[slot, pp, 0], mb_scr[slot, pp, 1])

    def exponentials(n):
        slot = n % NSLOT
        for pp in range(npair):
            for hh in range(2):
                mb = mb_scr[slot, pp, hh]
                p_scr[slot, pp, hh] = jnp.exp2(s_scr[slot, pp, hh] - jnp.concatenate([mb, mb], axis=1)).astype(BF16)

    def weighted_values(n):
        slot = n % NSLOT
        r = pl.ds(n * BLK, BLK)
        rk = pl.ds(n * BLK, 2 * BLK)
        for pp in range(npair):
            ul0 = jnp.dot(p_scr[slot, pp, 0], v0b[rk, :], preferred_element_type=F32)
            ul1 = jnp.dot(p_scr[slot, pp, 1], v1b[rk, :], preferred_element_type=F32)
            u = jnp.where(lo, ul0, ul1)
            l = pltpu.roll(jnp.where(lo, ul1, ul0), HEAD_DIM, 1)
            m = m_scr[slot, pp]
            snk = snk_ref[pp:pp + 1, :] * LOG2E
            mx = jnp.maximum(m, snk)
            w = jnp.exp2(m - mx)
            o = u * (w / (l * w + jnp.exp2(snk - mx)))
            o_ref[r, pl.ds(pp * LANES, LANES)] = (o * _silu(g_refs[pp][r, :].astype(F32))).astype(o_ref.dtype)

    for n in range(nblk + 3):
        if n - 3 >= 0:
            weighted_values(n - 3)
        if n < nblk:
            scores(n)
        if 0 <= n - 1 < nblk:
            row_max(n - 1)
        if 0 <= n - 2 < nblk:
            exponentials(n - 2)


def _attn_a(z3, gq2, gk2, snk3, tabs, casts=()):
    bsz, seq, _ = z3.shape
    npair = A_HEADS // A_KV_HEADS // 2

    def col(off, pp):
        return pl.BlockSpec((None, seq, LANES), lambda b, gi: (b, 0, off + npair * gi + pp))

    def fixed(off):
        return pl.BlockSpec((None, seq, LANES), lambda b, gi: (b, 0, off))

    return _call_with_casts(
        functools.partial(_attn_a_kernel, seq=seq),
        grid=(bsz, A_KV_HEADS),
        in_specs=[col(_QA, 0), col(_QA, 1), col(_QA, 2), col(_QA, 3), fixed(_KA), fixed(_VA),
                  col(_GA, 0), col(_GA, 1), col(_GA, 2), col(_GA, 3),
                  pl.BlockSpec((1, LANES), lambda b, gi: (0, 0)),
                  pl.BlockSpec((1, LANES), lambda b, gi: (0, 0)),
                  pl.BlockSpec((None, npair, LANES), lambda b, gi: (gi, 0, 0)),
                  pl.BlockSpec((None, 2 * npair, 2, BLK, 2 * BLK), lambda b, gi: (N_B_PATTERNS, gi, 0, 0, 0))],
        out_specs=pl.BlockSpec((None, seq, npair * LANES), lambda b, gi: (b, 0, gi)),
        out_shape=jax.ShapeDtypeStruct((bsz, seq, A_WIDTH), BF16),
        scratch_shapes=[pltpu.VMEM((npair, seq, LANES), BF16),
                        pltpu.VMEM((PAD + seq, LANES), BF16),
                        pltpu.VMEM((PAD + seq, LANES), BF16),
                        pltpu.VMEM((PAD + seq, LANES), BF16),
                        pltpu.VMEM((NSLOT, npair, 2, BLK, 2 * BLK), F32),
                        pltpu.VMEM((NSLOT, npair, 2, BLK, LANES), F32),
                        pltpu.VMEM((NSLOT, npair, 2, BLK, 2 * BLK), BF16),
                        pltpu.VMEM((NSLOT, npair, BLK, LANES), F32)],
        name="attn_a",
        args=(z3, z3, z3, z3, z3, z3, z3, z3, z3, z3, gq2, gk2, snk3, tabs),
        casts=casts)


def _attn_b_kernel(q_ref, k_ref, v_ref, g_ref, gq_ref, gk_ref, bias_ref, o_ref,
                   qf, kf, vf, q4f, k4f, v4f, qb, kb, v0b, v1b, s_scr, mb_scr, p_scr, m_scr, l_scr, u_scr, *, seq):
    lo = _lane_lo(BLK)
    hi = jnp.logical_not(lo)
    nblk = seq // BLK
    zeros = jnp.zeros((PAD, LANES), BF16)
    for pi in range(2):
        kb[pi, pl.ds(0, PAD), :] = zeros
        v0b[pi, pl.ds(0, PAD), :] = zeros
        v1b[pi, pl.ds(0, PAD), :] = zeros

    def put(pi, dst, q, k, v):
        qb[pi, dst, :] = q.astype(BF16)
        kb[pi, dst, :] = k.astype(BF16)
        v0b[pi, dst, :] = jnp.where(lo, v, 1.0).astype(BF16)
        v1b[pi, dst, :] = jnp.where(lo, 1.0, v).astype(BF16)

    def strided(t):
        return pl.ds(_and(t, 3) * (4 * BLK) + _shr(t, 2), BLK, stride=4)

    gains = jnp.concatenate([gq_ref[...] * (SCALE * LOG2E), gk_ref[...]], axis=1)

    def prep(c, carry):
        r = pl.ds(c * BLK, BLK)
        qk = jnp.concatenate([q_ref[r, :], k_ref[r, :]], axis=1).astype(F32)
        n = _head_rms(qk, gains)
        q, k, v = n[:, :LANES], n[:, LANES:], v_ref[r, :].astype(F32)
        qf[r, :] = q
        kf[r, :] = k
        vf[r, :] = v
        put(0, pl.ds(PAD + c * BLK, BLK), q, k, v)
        return carry
    for c in range(nblk):
        prep(c, 0)

    def deint4(t, carry):
        src = strided(t)
        dst = pl.ds(t * BLK, BLK)
        q, k, v = qf[src, :], kf[src, :], vf[src, :]
        q4f[dst, :] = q
        k4f[dst, :] = k
        v4f[dst, :] = v
        put(1, pl.ds(PAD + t * BLK, BLK), q, k, v)
        return carry
    for t in range(nblk):
        deint4(t, 0)

    def deint16(t, carry):
        src = strided(t)
        put(2, pl.ds(PAD + t * BLK, BLK), q4f[src, :], k4f[src, :], v4f[src, :])
        return carry
    for t in range(nblk):
        deint16(t, 0)

    groups = [(pi, g) for pi in range(N_B_PATTERNS) for g in range(nblk // UNROLL)]

    def geometry(pi):
        single = pi == 2
        return single, (BLK if single else 2 * BLK), (PAD if single else PAD - BLK)

    def out_rows(pi, b):
        return pl.ds(b * BLK, BLK) if pi == 0 else strided(b)

    def scores(t):
        pi, g = groups[t]
        single, keys, koff = geometry(pi)
        for uu in range(UNROLL):
            b = g * UNROLL + uu
            first = int(b % (nblk if pi == 0 else 4) == 0)
            q = qb[pi, pl.ds(PAD + b * BLK, BLK), :]
            kk = kb[pi, pl.ds(koff + b * BLK, keys), :]
            for hh, sel in enumerate((lo, hi)):
                bias = bias_ref[pi, hh, 0, :, pl.ds(BLK, BLK)] if single else bias_ref[pi, hh, first]
                s_scr[(t % NSLOT) * UNROLL + uu, hh, :, pl.ds(0, keys)] = lax.dot_general(
                    jnp.where(sel, q, jnp.zeros_like(q)), kk, (((1,), (1,)), ((), ())),
                    preferred_element_type=F32) + bias

    def row_max(t):
        pi, g = groups[t]
        _, keys, _ = geometry(pi)
        for uu in range(UNROLL):
            b = g * UNROLL + uu
            slot = (t % NSLOT) * UNROLL + uu
            for hh in range(2):
                m = jnp.max(s_scr[slot, hh, :, pl.ds(0, keys)], axis=-1, keepdims=True)
                mb_scr[slot, hh] = jnp.broadcast_to(m, (BLK, LANES))
            m_scr[pi, out_rows(pi, b), :] = jnp.where(lo, mb_scr[slot, 0], mb_scr[slot, 1])

    def exponentials(t):
        pi, g = groups[t]
        _, keys, _ = geometry(pi)
        for uu in range(UNROLL):
            slot = (t % NSLOT) * UNROLL + uu
            for hh in range(2):
                mb = mb_scr[slot, hh]
                mb = mb if keys == BLK else jnp.concatenate([mb, mb], axis=1)
                p_scr[slot, hh, :, pl.ds(0, keys)] = jnp.exp2(s_scr[slot, hh, :, pl.ds(0, keys)] - mb).astype(BF16)

    def weighted_values(t):
        pi, g = groups[t]
        _, keys, koff = geometry(pi)
        for uu in range(UNROLL):
            b = g * UNROLL + uu
            slot = (t % NSLOT) * UNROLL + uu
            rk = pl.ds(koff + b * BLK, keys)
            ul0 = jnp.dot(p_scr[slot, 0, :, pl.ds(0, keys)], v0b[pi, rk, :], preferred_element_type=F32)
            ul1 = jnp.dot(p_scr[slot, 1, :, pl.ds(0, keys)], v1b[pi, rk, :], preferred_element_type=F32)
            u_scr[pi, out_rows(pi, b), :] = jnp.where(lo, ul0, ul1)
            l_scr[pi, out_rows(pi, b), :] = pltpu.roll(jnp.where(lo, ul1, ul0), HEAD_DIM, 1)

    for t in range(len(groups) + 3):
        if t - 3 >= 0:
            weighted_values(t - 3)
        if t < len(groups):
            scores(t)
        if 0 <= t - 1 < len(groups):
            row_max(t - 1)
        if 0 <= t - 2 < len(groups):
            exponentials(t - 2)

    def renat(t, carry):
        for ref in (m_scr, l_scr, u_scr):
            ref[3, strided(t), :] = ref[2, pl.ds(t * BLK, BLK), :]
        return carry
    lax.fori_loop(0, nblk, renat, 0)

    def combine(c, carry):
        r = pl.ds(c * BLK, BLK)
        slots = (0, 1, 3)
        ms = [m_scr[s, r, :] for s in slots]
        mx = jnp.maximum(jnp.maximum(ms[0], ms[1]), ms[2])
        num = jnp.zeros((BLK, LANES), F32)
        den = jnp.zeros((BLK, LANES), F32)
        for s, m in zip(slots, ms):
            w = jnp.exp2(m - mx)
            num = num + w * u_scr[s, r, :]
            den = den + w * l_scr[s, r, :]
        o_ref[r, :] = ((num / den) * _silu(g_ref[r, :].astype(F32))).astype(o_ref.dtype)
        return carry
    lax.fori_loop(0, nblk, combine, 0, unroll=2)


def _attn_b(z3, gq2, gk2, tabs, casts=()):
    bsz, seq, _ = z3.shape
    col = lambda off: pl.BlockSpec((None, seq, LANES), lambda b, p: (b, 0, off + p))
    f32_rows = pltpu.VMEM((seq, LANES), F32)
    bf16_ops = pltpu.VMEM((N_B_PATTERNS, PAD + seq, LANES), BF16)
    stats = pltpu.VMEM((N_B_PATTERNS + 1, seq, LANES), F32)
    return _call_with_casts(
        functools.partial(_attn_b_kernel, seq=seq),
        grid=(bsz, B_HEADS // 2),
        in_specs=[col(_QB), col(_KB), col(_VB), col(_GB),
                  pl.BlockSpec((1, LANES), lambda b, p: (0, 0)),
                  pl.BlockSpec((1, LANES), lambda b, p: (0, 0)),
                  pl.BlockSpec((N_B_PATTERNS, 2, 2, BLK, 2 * BLK), lambda b, p: (0, p, 0, 0, 0))],
        out_specs=pl.BlockSpec((None, seq, LANES), lambda b, p: (b, 0, p)),
        out_shape=jax.ShapeDtypeStruct((bsz, seq, B_WIDTH), BF16),
        scratch_shapes=[f32_rows] * 6 + [bf16_ops] * 4
                       + [pltpu.VMEM((NSLOT * UNROLL, 2, BLK, 2 * BLK), F32),
                          pltpu.VMEM((NSLOT * UNROLL, 2, BLK, LANES), F32),
                          pltpu.VMEM((NSLOT * UNROLL, 2, BLK, 2 * BLK), BF16)] + [stats] * 3,
        name="attn_b",
        args=(z3, z3, z3, z3, gq2, gk2, tabs),
        casts=casts)


def _even_out_kernel(ya_ref, yb_ref, wa_ref, wb_ref, x_ref, o_ref):
    acc = jnp.dot(ya_ref[...], wa_ref[...], preferred_element_type=F32)
    acc = acc + jnp.dot(yb_ref[...], wb_ref[...], preferred_element_type=F32)
    o_ref[...] = x_ref[...] + acc


def _resident(block, index_map, n_col_tiles):
    if n_col_tiles == 1:
        return pl.BlockSpec(block, index_map, pipeline_mode=pl.Buffered(1))
    return pl.BlockSpec(block, index_map)


def _even_out(ya, yb, w, layer, x2, *, tm=512, tn=2048):
    m, d = x2.shape
    ka, kb = ya.shape[1], yb.shape[1]
    assert ka == kb
    return pl.pallas_call(
        _even_out_kernel,
        grid=(m // tm, d // tn),
        in_specs=[pl.BlockSpec((tm, ka), lambda i, j: (i, 0)),
                  pl.BlockSpec((tm, kb), lambda i, j: (i, 0)),
                  _resident((None, ka, tn), lambda i, j: (layer, 0, j), d // tn),
                  _resident((None, kb, tn), lambda i, j: (layer, 1, j), d // tn),
                  pl.BlockSpec((tm, tn), lambda i, j: (i, j))],
        out_specs=pl.BlockSpec((tm, tn), lambda i, j: (i, j)),
        out_shape=jax.ShapeDtypeStruct((m, d), F32),
        compiler_params=_params(("arbitrary", "arbitrary")),
        name="even_out",
    )(ya, yb, w, w, x2)


def _odd_out_kernel(y_ref, w_ref, x_ref, o_ref):
    o_ref[...] = x_ref[...] + jnp.dot(y_ref[...], w_ref[...], preferred_element_type=F32)


def _odd_out(y, w, layer, x2, *, tm=512, tn=2048):
    m, d = x2.shape
    k = y.shape[1]
    return pl.pallas_call(
        _odd_out_kernel,
        grid=(m // tm, d // tn),
        in_specs=[pl.BlockSpec((tm, k), lambda i, j: (i, 0)),
                  _resident((None, k, tn), lambda i, j: (layer, 0, j), d // tn),
                  pl.BlockSpec((tm, tn), lambda i, j: (i, j))],
        out_specs=pl.BlockSpec((tm, tn), lambda i, j: (i, j)),
        out_shape=jax.ShapeDtypeStruct((m, d), F32),
        compiler_params=_params(("arbitrary", "arbitrary")),
        name="odd_out",
    )(y, w, x2)


def kernel(x, ev_ln_g, ev_w_in, ev_qk_g, ev_sinks, ev_w_out, od_ln_g, od_w_in, od_v_g, od_w_s,
           od_b_s, od_w_out, rel_bias):
    bsz, seq, d = x.shape
    n_even, n_odd = ev_ln_g.shape[0], od_ln_g.shape[0]
    flat = lambda w: w.reshape(-1, w.shape[-1])
    whole = lambda w: _Cast(flat(w), 0, w.shape[0] * w.shape[1])
    layer_of = lambda w, j: _Cast(flat(w), j * w.shape[1], w.shape[1])
    tabs, done = _bias_tables(rel_bias, [layer_of(ev_w_in, 0)])
    ev_in_b = {0: done[0][None]}
    od_in_b, ev_out_b, od_out_b = {}, None, None
    x2 = x.reshape(bsz * seq, d)
    for i in range(n_even + n_odd):
        j = i // 2
        if i % 2 == 0:
            z = _even_in(x2, ev_ln_g[j].reshape(1, d), ev_in_b[j], 0)
            z3 = z.reshape(bsz, seq, EVEN_IN)
            gains = jnp.tile(ev_qk_g[j].astype(F32), (1, 2))
            snk3 = jnp.repeat(ev_sinks[j].astype(F32), HEAD_DIM).reshape(A_KV_HEADS, -1, LANES)
            ya, done = _attn_a(z3, gains[0:1], gains[1:2], snk3, tabs, [whole(ev_w_out)] if j == 0 else [])
            if j == 0:
                ev_out_b = done[0].reshape(ev_w_out.shape)
            yb, done = _attn_b(z3, gains[2:3], gains[3:4], tabs, [layer_of(od_w_in, j)] if j < n_odd else [])
            if j < n_odd:
                od_in_b[j] = done[0][None]
            x2 = _even_out(ya.reshape(bsz * seq, A_WIDTH), yb.reshape(bsz * seq, B_WIDTH), ev_out_b, j, x2)
        else:
            casts = ([whole(od_w_out)] if j == 0 else []) + ([layer_of(ev_w_in, j + 1)] if j + 1 < n_even else [])
            y, done = _odd_in(x2, od_ln_g[j].reshape(1, d), od_in_b[j], 0, od_v_g[j].astype(F32),
                              od_w_s[j].astype(F32), od_b_s[j].astype(F32), casts)
            if j == 0:
                od_out_b = done.pop(0).reshape(od_w_out.shape)
            if j + 1 < n_even:
                ev_in_b[j + 1] = done.pop(0)[None]
            x2 = _odd_out(y, od_out_b, j, x2)
    return x2.reshape(bsz, seq, d)
```

```python
import functools
import math
from typing import NamedTuple

import numpy as np
import jax
import jax.numpy as jnp
from jax import lax
from jax.experimental import pallas as pl
from jax.experimental.pallas import tpu as pltpu

F32 = jnp.float32
BF16 = jnp.bfloat16

D_MODEL = 2048
HEAD_DIM = 64
A_HEADS = 16
A_KV_HEADS = 2
B_HEADS = 16
BLK = 128
NUM_BUCKETS = 32
REL_MAX_DISTANCE = 2048
A_WIDTH = A_HEADS * HEAD_DIM
B_WIDTH = B_HEADS * HEAD_DIM
EVEN_IN = 6400
C_WIDTH = 2 * D_MODEL
C_GROUPS = 16
C_GROUP_DIM = C_WIDTH // C_GROUPS
C_CHUNK = 128
EPS = 1e-6
NEG = -1e30
SCALE = HEAD_DIM ** -0.5
LOG2E = 1.4426950408889634
PATTERNS = ((1, 128), (4, 128), (16, 128), (1, 127))
N_B_PATTERNS = 3

LANES = 128
BF16_SUBLANES = 16
NORM_ROWS = 128
VMEM_LIMIT = 56 * 1024 * 1024

_QA, _KA, _VA, _GA = 0, 8, 9, 10
_QB, _KB, _VB, _GB = 18, 26, 34, 42


def _params(sem):
    return pltpu.CompilerParams(dimension_semantics=sem, vmem_limit_bytes=VMEM_LIMIT)


class _Cast(NamedTuple):
    src: jax.Array
    first_row: int
    n_rows: int


def _call_with_casts(body, *, grid, in_specs, out_specs, out_shape, scratch_shapes, name, args, casts=()):
    n_in, n_cast = len(in_specs), len(casts)
    steps = grid[0] * grid[1]
    cast_in, cast_out, cast_shape = [], [], []
    for c in casts:
        cols = c.src.shape[1]
        tile = c.n_rows // steps
        first = c.first_row // tile
        assert tile * steps == c.n_rows and first * tile == c.first_row and tile % BF16_SUBLANES == 0
        cast_in.append(pl.BlockSpec((tile, cols), lambda i, j, first=first: (first + i * grid[1] + j, 0)))
        cast_out.append(pl.BlockSpec((tile, cols), lambda i, j: (i * grid[1] + j, 0)))
        cast_shape.append(jax.ShapeDtypeStruct((c.n_rows, cols), BF16))

    def kernel(*refs):
        ins, srcs = refs[:n_in], refs[n_in:n_in + n_cast]
        out, dsts = refs[n_in + n_cast], refs[n_in + n_cast + 1:n_in + 2 * n_cast + 1]
        for src, dst in zip(srcs, dsts):
            dst[...] = src[...].astype(dst.dtype)
        body(*ins, out, *refs[n_in + 2 * n_cast + 1:])

    res = pl.pallas_call(
        kernel,
        grid=grid,
        in_specs=list(in_specs) + cast_in,
        out_specs=[out_specs] + cast_out,
        out_shape=[out_shape] + cast_shape,
        scratch_shapes=scratch_shapes,
        compiler_params=_params(("arbitrary", "arbitrary")),
        name=name,
    )(*args, *[c.src for c in casts])
    return res[0], list(res[1:])


def _bucket_tables():
    a = np.arange(BLK)[:, None]
    b = np.arange(2 * BLK)[None, :]
    dist = BLK + a - b
    max_exact = NUM_BUCKETS // 2
    out = []
    for dil, max_dist in PATTERNS:
        n = np.maximum(dist * dil, 0)
        large = max_exact + (np.log(np.maximum(n, 1) / max_exact)
                             / np.log(REL_MAX_DISTANCE / max_exact)
                             * (NUM_BUCKETS - max_exact)).astype(np.int32)
        large = np.minimum(large, NUM_BUCKETS - 1)
        bucket = np.where(n < max_exact, n, large).astype(np.int32)
        valid = (dist >= 0) & (dist <= max_dist)
        out.append(np.where(valid, bucket, -1).astype(np.int32))
    return np.stack(out)


def _bias_kernel(tbl_ref, bucket_ref, out_ref):
    @pl.when(pl.program_id(1) == 0)
    def _():
        col0 = jnp.where(pl.program_id(0) < N_B_PATTERNS, A_HEADS, 0)
        bk = bucket_ref[...]
        prev_cols = lax.broadcasted_iota(jnp.int32, bk.shape, 1) < BLK

        def head(h, carry):
            acc = jnp.full(bk.shape, NEG, F32)
            for b in range(NUM_BUCKETS):
                acc = jnp.where(bk == b, tbl_ref[b, col0 + h] * LOG2E, acc)
            out_ref[h, 0] = acc
            out_ref[h, 1] = jnp.where(prev_cols, NEG, acc)
            return carry
        lax.fori_loop(0, out_ref.shape[0], head, 0)


def _bias_tables(rel_bias, casts=()):
    buckets = jnp.asarray(_bucket_tables())
    heads = max(A_HEADS, B_HEADS)
    return _call_with_casts(
        _bias_kernel,
        grid=(len(PATTERNS), 4),
        in_specs=[pl.BlockSpec(memory_space=pltpu.SMEM),
                  pl.BlockSpec((None, BLK, 2 * BLK), lambda t, _: (t, 0, 0))],
        out_specs=pl.BlockSpec((None, heads, 2, BLK, 2 * BLK), lambda t, _: (t, 0, 0, 0, 0)),
        out_shape=jax.ShapeDtypeStruct((len(PATTERNS), heads, 2, BLK, 2 * BLK), F32),
        scratch_shapes=[],
        name="bias_tables",
        args=(rel_bias.astype(F32), buckets),
        casts=casts)


def _norm_rows(x_ref, g_ref, h_scr, tm):
    def body(c, carry):
        r = pl.ds(c * NORM_ROWS, NORM_ROWS)
        x = x_ref[r, :]
        ms = jnp.mean(x * x, axis=-1, keepdims=True)
        h_scr[r, :] = (x * lax.rsqrt(ms + EPS) * g_ref[...]).astype(BF16)
        return carry
    lax.fori_loop(0, tm // NORM_ROWS, body, 0, unroll=2)


def _even_in_kernel(x_ref, g_ref, w_ref, o_ref, h_scr, *, tm):
    @pl.when(pl.program_id(1) == 0)
    def _():
        _norm_rows(x_ref, g_ref, h_scr, tm)
    o_ref[...] = jnp.dot(h_scr[...], w_ref[...], preferred_element_type=F32).astype(o_ref.dtype)


def _even_in(x2, g, w, layer, *, tm=1024, tn=1280):
    m, d = x2.shape
    n = w.shape[2]
    return pl.pallas_call(
        functools.partial(_even_in_kernel, tm=tm),
        grid=(m // tm, n // tn),
        in_specs=[pl.BlockSpec((tm, d), lambda i, j: (i, 0)),
                  pl.BlockSpec((1, d), lambda i, j: (0, 0)),
                  pl.BlockSpec((None, d, tn), lambda i, j: (layer, 0, j))],
        out_specs=pl.BlockSpec((tm, tn), lambda i, j: (i, j)),
        out_shape=jax.ShapeDtypeStruct((m, n), BF16),
        scratch_shapes=[pltpu.VMEM((tm, d), BF16)],
        compiler_params=_params(("arbitrary", "arbitrary")),
        name="even_in",
    )(x2, g, w)


def _gelu(x):
    return 0.5 * x * (1.0 + lax.erf(x * (1.0 / math.sqrt(2.0))))


def _silu(x):
    return (0.5 * x) * (1.0 + jnp.tanh(0.5 * x))


def _odd_in_kernel(x_ref, g_ref, wv_ref, wu_ref, wg_ref, vg_ref, ws_ref, bs_ref, y_ref,
                   h_scr, v_scr, ssq_scr, *, tm, tn):
    j = pl.program_id(1)
    nb = C_WIDTH // tn

    @pl.when(j == 0)
    def _():
        _norm_rows(x_ref, g_ref, h_scr, tm)
        ssq_scr[...] = jnp.zeros_like(ssq_scr)

    @pl.when(j < nb)
    def _():
        v = _gelu(jnp.dot(h_scr[...], wv_ref[...], preferred_element_type=F32))
        v_scr[j] = v.astype(v_scr.dtype)
        ssq_scr[...] += jnp.sum(v * v, axis=-1, keepdims=True)

    @pl.when(j >= nb)
    def _():
        jj = j - nb
        h = h_scr[...]
        ug = (_gelu(jnp.dot(h, wu_ref[...], preferred_element_type=F32))
              * _silu(jnp.dot(h, wg_ref[...], preferred_element_type=F32)))
        tril = (lax.broadcasted_iota(jnp.int32, (C_CHUNK, C_CHUNK), 0)
                >= lax.broadcasted_iota(jnp.int32, (C_CHUNK, C_CHUNK), 1))
        groups_per_tile = tn // C_GROUP_DIM
        vg = vg_ref[jj]
        for gg in range(groups_per_tile):
            grp = jj * groups_per_tile + gg
            wt = jnp.where(tril, ws_ref[grp], 0.0).astype(BF16)
            cols = slice(gg * C_GROUP_DIM, (gg + 1) * C_GROUP_DIM)
            for c in range(tm // C_CHUNK):
                rows = slice(c * C_CHUNK, (c + 1) * C_CHUNK)
                inv = lax.rsqrt(ssq_scr[rows, :][:, :1] * (1.0 / C_WIDTH) + EPS)
                vn = (v_scr[jj, rows, cols].astype(F32) * inv * vg[:, cols]).astype(BF16)
                s = jnp.dot(wt, vn, preferred_element_type=F32) + bs_ref[grp]
                y_ref[rows, cols] = (ug[rows, cols] * s).astype(y_ref.dtype)


def _odd_in(x2, g, w, layer, vg, ws, bs, casts=(), *, tm=1024, tn=512):
    m, d = x2.shape
    nb = C_WIDTH // tn
    return _call_with_casts(
        functools.partial(_odd_in_kernel, tm=tm, tn=tn),
        grid=(m // tm, 2 * nb),
        in_specs=[pl.BlockSpec((tm, d), lambda i, j: (i, 0)),
                  pl.BlockSpec((1, d), lambda i, j: (0, 0)),
                  pl.BlockSpec((None, d, tn), lambda i, j: (layer, 0, nb + jnp.minimum(j, nb - 1))),
                  pl.BlockSpec((None, d, tn), lambda i, j: (layer, 0, jnp.maximum(j - nb, 0))),
                  pl.BlockSpec((None, d, tn), lambda i, j: (layer, 0, 2 * nb + jnp.maximum(j - nb, 0))),
                  pl.BlockSpec((nb, 1, tn), lambda i, j: (0, 0, 0)),
                  pl.BlockSpec((C_GROUPS, C_CHUNK, C_CHUNK), lambda i, j: (0, 0, 0)),
                  pl.BlockSpec((C_GROUPS, C_CHUNK, 1), lambda i, j: (0, 0, 0))],
        out_specs=pl.BlockSpec((tm, tn), lambda i, j: (i, jnp.maximum(j - nb, 0))),
        out_shape=jax.ShapeDtypeStruct((m, C_WIDTH), BF16),
        scratch_shapes=[pltpu.VMEM((tm, d), BF16),
                        pltpu.VMEM((nb, tm, tn), BF16),
                        pltpu.VMEM((tm, LANES), F32)],
        name="odd_in",
        args=(x2, g, w, w, w, vg.reshape(nb, 1, tn), ws, bs.reshape(C_GROUPS, C_CHUNK, 1)),
        casts=casts)


PAD = BLK
UNROLL = 4
NSLOT = 3


def _and(b, mask):
    return b & mask if isinstance(b, int) else lax.bitwise_and(b, mask)


def _shr(b, s):
    return b >> s if isinstance(b, int) else lax.shift_right_logical(b, s)


def _lane_lo(rows):
    return lax.broadcasted_iota(jnp.int32, (rows, LANES), 1) < HEAD_DIM


def _head_rms(x, gain):
    w = x.shape[1]
    r = lax.broadcasted_iota(jnp.int32, (w, w), 0) // HEAD_DIM
    c = lax.broadcasted_iota(jnp.int32, (w, w), 1) // HEAD_DIM
    ones_bd = (r == c).astype(BF16)
    x2 = x * x
    hi = x2.astype(BF16)
    lo = (x2 - hi.astype(F32)).astype(BF16)
    ssq = (jnp.dot(hi, ones_bd, preferred_element_type=F32)
           + jnp.dot(lo, ones_bd, preferred_element_type=F32))
    return x * lax.rsqrt(ssq * (1.0 / HEAD_DIM) + EPS) * gain


def _attn_a_kernel(q0_ref, q1_ref, q2_ref, q3_ref, k_ref, v_ref, g0_ref, g1_ref, g2_ref, g3_ref,
                   gq_ref, gk_ref, snk_ref, bias_ref, o_ref, qb, kb, v0b, v1b, s_scr, mb_scr, p_scr, m_scr, *, seq):
    q_refs = (q0_ref, q1_ref, q2_ref, q3_ref)
    g_refs = (g0_ref, g1_ref, g2_ref, g3_ref)
    npair = len(q_refs)
    nblk = seq // BLK
    lane = lax.broadcasted_iota(jnp.int32, (BLK, LANES), 1)
    keep = (lane // HEAD_DIM) == pl.program_id(1)
    lo = lane < HEAD_DIM
    hi = jnp.logical_not(lo)
    zeros = jnp.zeros((PAD, LANES), BF16)
    kb[pl.ds(0, PAD), :] = zeros
    v0b[pl.ds(0, PAD), :] = zeros
    v1b[pl.ds(0, PAD), :] = zeros
    gq = gq_ref[...] * (SCALE * LOG2E)
    gq = jnp.concatenate([gq, gq], axis=1)

    def prep(c, carry):
        r = pl.ds(c * BLK, BLK)
        ro = pl.ds(PAD + c * BLK, BLK)
        for pp in range(0, npair, 2):
            qq = jnp.concatenate([q_refs[pp][r, :], q_refs[pp + 1][r, :]], axis=1).astype(F32)
            qn = _head_rms(qq, gq).astype(BF16)
            qb[pp, r, :] = qn[:, :LANES]
            qb[pp + 1, r, :] = qn[:, LANES:]
        kn = _head_rms(k_ref[r, :].astype(F32), gk_ref[...])
        kb[ro, :] = jnp.where(keep, kn, pltpu.roll(kn, HEAD_DIM, 1)).astype(BF16)
        v = v_ref[r, :].astype(F32)
        vd = jnp.where(keep, v, pltpu.roll(v, HEAD_DIM, 1))
        v0b[ro, :] = jnp.where(lo, vd, 1.0).astype(BF16)
        v1b[ro, :] = jnp.where(lo, 1.0, vd).astype(BF16)
        return carry
    lax.fori_loop(0, nblk, prep, 0, unroll=4)

    def scores(n):
        kk = kb[pl.ds(n * BLK, 2 * BLK), :]
        for pp in range(npair):
            q = qb[pp, pl.ds(n * BLK, BLK), :]
            for hh, sel in enumerate((lo, hi)):
                s_scr[n % NSLOT, pp, hh] = lax.dot_general(
                    jnp.where(sel, q, jnp.zeros_like(q)), kk, (((1,), (1,)), ((), ())),
                    preferred_element_type=F32) + bias_ref[2 * pp + hh, int(n == 0)]

    def row_max(n):
        slot = n % NSLOT
        for pp in range(npair):
            for hh in range(2):
                m = jnp.max(s_scr[slot, pp, hh], axis=-1, keepdims=True)
                mb_scr[slot, pp, hh] = jnp.broadcast_to(m, (BLK, LANES))
            m_scr[slot, pp] = jnp.where(lo, mb_scr[slot, pp, 0], mb_scr[slot, pp, 1])

    def exponentials(n):
        slot = n % NSLOT
        for pp in range(npair):
            for hh in range(2):
                mb = mb_scr[slot, pp, hh]
                p_scr[slot, pp, hh] = jnp.exp2(s_scr[slot, pp, hh] - jnp.concatenate([mb, mb], axis=1)).astype(BF16)

    def weighted_values(n):
        slot = n % NSLOT
        r = pl.ds(n * BLK, BLK)
        rk = pl.ds(n * BLK, 2 * BLK)
        for pp in range(npair):
            ul0 = jnp.dot(p_scr[slot, pp, 0], v0b[rk, :], preferred_element_type=F32)
            ul1 = jnp.dot(p_scr[slot, pp, 1], v1b[rk, :], preferred_element_type=F32)
            u = jnp.where(lo, ul0, ul1)
            l = pltpu.roll(jnp.where(lo, ul1, ul0), HEAD_DIM, 1)
            m = m_scr[slot, pp]
            snk = snk_ref[pp:pp + 1, :] * LOG2E
            mx = jnp.maximum(m, snk)
            w = jnp.exp2(m - mx)
            o = u * (w / (l * w + jnp.exp2(snk - mx)))
            o_ref[r, pl.ds(pp * LANES, LANES)] = (o * _silu(g_refs[pp][r, :].astype(F32))).astype(o_ref.dtype)

    for n in range(nblk + 3):
        if n - 3 >= 0:
            weighted_values(n - 3)
        if n < nblk:
            scores(n)
        if 0 <= n - 1 < nblk:
            row_max(n - 1)
        if 0 <= n - 2 < nblk:
            exponentials(n - 2)


def _attn_a(z3, gq2, gk2, snk3, tabs, casts=()):
    bsz, seq, _ = z3.shape
    npair = A_HEADS // A_KV_HEADS // 2

    def col(off, pp):
        return pl.BlockSpec((None, seq, LANES), lambda b, gi: (b, 0, off + npair * gi + pp))

    def fixed(off):
        return pl.BlockSpec((None, seq, LANES), lambda b, gi: (b, 0, off))

    return _call_with_casts(
        functools.partial(_attn_a_kernel, seq=seq),
        grid=(bsz, A_KV_HEADS),
        in_specs=[col(_QA, 0), col(_QA, 1), col(_QA, 2), col(_QA, 3), fixed(_KA), fixed(_VA),
                  col(_GA, 0), col(_GA, 1), col(_GA, 2), col(_GA, 3),
                  pl.BlockSpec((1, LANES), lambda b, gi: (0, 0)),
                  pl.BlockSpec((1, LANES), lambda b, gi: (0, 0)),
                  pl.BlockSpec((None, npair, LANES), lambda b, gi: (gi, 0, 0)),
                  pl.BlockSpec((None, 2 * npair, 2, BLK, 2 * BLK), lambda b, gi: (N_B_PATTERNS, gi, 0, 0, 0))],
        out_specs=pl.BlockSpec((None, seq, npair * LANES), lambda b, gi: (b, 0, gi)),
        out_shape=jax.ShapeDtypeStruct((bsz, seq, A_WIDTH), BF16),
        scratch_shapes=[pltpu.VMEM((npair, seq, LANES), BF16),
                        pltpu.VMEM((PAD + seq, LANES), BF16),
                        pltpu.VMEM((PAD + seq, LANES), BF16),
                        pltpu.VMEM((PAD + seq, LANES), BF16),
                        pltpu.VMEM((NSLOT, npair, 2, BLK, 2 * BLK), F32),
                        pltpu.VMEM((NSLOT, npair, 2, BLK, LANES), F32),
                        pltpu.VMEM((NSLOT, npair, 2, BLK, 2 * BLK), BF16),
                        pltpu.VMEM((NSLOT, npair, BLK, LANES), F32)],
        name="attn_a",
        args=(z3, z3, z3, z3, z3, z3, z3, z3, z3, z3, gq2, gk2, snk3, tabs),
        casts=casts)


def _attn_b_kernel(q_ref, k_ref, v_ref, g_ref, gq_ref, gk_ref, bias_ref, o_ref,
                   qf, kf, vf, q4f, k4f, v4f, qb, kb, v0b, v1b, s_scr, mb_scr, p_scr, m_scr, l_scr, u_scr, *, seq):
    lo = _lane_lo(BLK)
    hi = jnp.logical_not(lo)
    nblk = seq // BLK
    zeros = jnp.zeros((PAD, LANES), BF16)
    for pi in range(2):
        kb[pi, pl.ds(0, PAD), :] = zeros
        v0b[pi, pl.ds(0, PAD), :] = zeros
        v1b[pi, pl.ds(0, PAD), :] = zeros

    def put(pi, dst, q, k, v):
        qb[pi, dst, :] = q.astype(BF16)
        kb[pi, dst, :] = k.astype(BF16)
        v0b[pi, dst, :] = jnp.where(lo, v, 1.0).astype(BF16)
        v1b[pi, dst, :] = jnp.where(lo, 1.0, v).astype(BF16)

    def strided(t):
        return pl.ds(_and(t, 3) * (4 * BLK) + _shr(t, 2), BLK, stride=4)

    gains = jnp.concatenate([gq_ref[...] * (SCALE * LOG2E), gk_ref[...]], axis=1)

    def prep(c, carry):
        r = pl.ds(c * BLK, BLK)
        qk = jnp.concatenate([q_ref[r, :], k_ref[r, :]], axis=1).astype(F32)
        n = _head_rms(qk, gains)
        q, k, v = n[:, :LANES], n[:, LANES:], v_ref[r, :].astype(F32)
        qf[r, :] = q
        kf[r, :] = k
        vf[r, :] = v
        put(0, pl.ds(PAD + c * BLK, BLK), q, k, v)
        return carry
    for c in range(nblk):
        prep(c, 0)

    def deint4(t, carry):
        src = strided(t)
        dst = pl.ds(t * BLK, BLK)
        q, k, v = qf[src, :], kf[src, :], vf[src, :]
        q4f[dst, :] = q
        k4f[dst, :] = k
        v4f[dst, :] = v
        put(1, pl.ds(PAD + t * BLK, BLK), q, k, v)
        return carry
    for t in range(nblk):
        deint4(t, 0)

    def deint16(t, carry):
        src = strided(t)
        put(2, pl.ds(PAD + t * BLK, BLK), q4f[src, :], k4f[src, :], v4f[src, :])
        return carry
    for t in range(nblk):
        deint16(t, 0)

    groups = [(pi, g) for pi in range(N_B_PATTERNS) for g in range(nblk // UNROLL)]

    def geometry(pi):
        single = pi == 2
        return single, (BLK if single else 2 * BLK), (PAD if single else PAD - BLK)

    def out_rows(pi, b):
        return pl.ds(b * BLK, BLK) if pi == 0 else strided(b)

    def scores(t):
        pi, g = groups[t]
        single, keys, koff = geometry(pi)
        for uu in range(UNROLL):
            b = g * UNROLL + uu
            first = int(b % (nblk if pi == 0 else 4) == 0)
            q = qb[pi, pl.ds(PAD + b * BLK, BLK), :]
            kk = kb[pi, pl.ds(koff + b * BLK, keys), :]
            for hh, sel in enumerate((lo, hi)):
                bias = bias_ref[pi, hh, 0, :, pl.ds(BLK, BLK)] if single else bias_ref[pi, hh, first]
                s_scr[(t % NSLOT) * UNROLL + uu, hh, :, pl.ds(0, keys)] = lax.dot_general(
                    jnp.where(sel, q, jnp.zeros_like(q)), kk, (((1,), (1,)), ((), ())),
                    preferred_element_type=F32) + bias

    def row_max(t):
        pi, g = groups[t]
        _, keys, _ = geometry(pi)
        for uu in range(UNROLL):
            b = g * UNROLL + uu
            slot = (t % NSLOT) * UNROLL + uu
            for hh in range(2):
                m = jnp.max(s_scr[slot, hh, :, pl.ds(0, keys)], axis=-1, keepdims=True)
                mb_scr[slot, hh] = jnp.broadcast_to(m, (BLK, LANES))
            m_scr[pi, out_rows(pi, b), :] = jnp.where(lo, mb_scr[slot, 0], mb_scr[slot, 1])

    def exponentials(t):
        pi, g = groups[t]
        _, keys, _ = geometry(pi)
        for uu in range(UNROLL):
            slot = (t % NSLOT) * UNROLL + uu
            for hh in range(2):
                mb = mb_scr[slot, hh]
                mb = mb if keys == BLK else jnp.concatenate([mb, mb], axis=1)
                p_scr[slot, hh, :, pl.ds(0, keys)] = jnp.exp2(s_scr[slot, hh, :, pl.ds(0, keys)] - mb).astype(BF16)

    def weighted_values(t):
        pi, g = groups[t]
        _, keys, koff = geometry(pi)
        for uu in range(UNROLL):
            b = g * UNROLL + uu
            slot = (t % NSLOT) * UNROLL + uu
            rk = pl.ds(koff + b * BLK, keys)
            ul0 = jnp.dot(p_scr[slot, 0, :, pl.ds(0, keys)], v0b[pi, rk, :], preferred_element_type=F32)
            ul1 = jnp.dot(p_scr[slot, 1, :, pl.ds(0, keys)], v1b[pi, rk, :], preferred_element_type=F32)
            u_scr[pi, out_rows(pi, b), :] = jnp.where(lo, ul0, ul1)
            l_scr[pi, out_rows(pi, b), :] = pltpu.roll(jnp.where(lo, ul1, ul0), HEAD_DIM, 1)

    for t in range(len(groups) + 3):
        if t - 3 >= 0:
            weighted_values(t - 3)
        if t < len(groups):
            scores(t)
        if 0 <= t - 1 < len(groups):
            row_max(t - 1)
        if 0 <= t - 2 < len(groups):
            exponentials(t - 2)

    def renat(t, carry):
        for ref in (m_scr, l_scr, u_scr):
            ref[3, strided(t), :] = ref[2, pl.ds(t * BLK, BLK), :]
        return carry
    lax.fori_loop(0, nblk, renat, 0)

    def combine(c, carry):
        r = pl.ds(c * BLK, BLK)
        slots = (0, 1, 3)
        ms = [m_scr[s, r, :] for s in slots]
        mx = jnp.maximum(jnp.maximum(ms[0], ms[1]), ms[2])
        num = jnp.zeros((BLK, LANES), F32)
        den = jnp.zeros((BLK, LANES), F32)
        for s, m in zip(slots, ms):
            w = jnp.exp2(m - mx)
            num = num + w * u_scr[s, r, :]
            den = den + w * l_scr[s, r, :]
        o_ref[r, :] = ((num / den) * _silu(g_ref[r, :].astype(F32))).astype(o_ref.dtype)
        return carry
    lax.fori_loop(0, nblk, combine, 0, unroll=2)


def _attn_b(z3, gq2, gk2, tabs, casts=()):
    bsz, seq, _ = z3.shape
    col = lambda off: pl.BlockSpec((None, seq, LANES), lambda b, p: (b, 0, off + p))
    f32_rows = pltpu.VMEM((seq, LANES), F32)
    bf16_ops = pltpu.VMEM((N_B_PATTERNS, PAD + seq, LANES), BF16)
    stats = pltpu.VMEM((N_B_PATTERNS + 1, seq, LANES), F32)
    return _call_with_casts(
        functools.partial(_attn_b_kernel, seq=seq),
        grid=(bsz, B_HEADS // 2),
        in_specs=[col(_QB), col(_KB), col(_VB), col(_GB),
                  pl.BlockSpec((1, LANES), lambda b, p: (0, 0)),
                  pl.BlockSpec((1, LANES), lambda b, p: (0, 0)),
                  pl.BlockSpec((N_B_PATTERNS, 2, 2, BLK, 2 * BLK), lambda b, p: (0, p, 0, 0, 0))],
        out_specs=pl.BlockSpec((None, seq, LANES), lambda b, p: (b, 0, p)),
        out_shape=jax.ShapeDtypeStruct((bsz, seq, B_WIDTH), BF16),
        scratch_shapes=[f32_rows] * 6 + [bf16_ops] * 4
                       + [pltpu.VMEM((NSLOT * UNROLL, 2, BLK, 2 * BLK), F32),
                          pltpu.VMEM((NSLOT * UNROLL, 2, BLK, LANES), F32),
                          pltpu.VMEM((NSLOT * UNROLL, 2, BLK, 2 * BLK), BF16)] + [stats] * 3,
        name="attn_b",
        args=(z3, z3, z3, z3, gq2, gk2, tabs),
        casts=casts)


def _even_out_kernel(ya_ref, yb_ref, wa_ref, wb_ref, x_ref, o_ref):
    acc = jnp.dot(ya_ref[...], wa_ref[...], preferred_element_type=F32)
    acc = acc + jnp.dot(yb_ref[...], wb_ref[...], preferred_element_type=F32)
    o_ref[...] = x_ref[...] + acc


def _resident(block, index_map, n_col_tiles):
    if n_col_tiles == 1:
        return pl.BlockSpec(block, index_map, pipeline_mode=pl.Buffered(1))
    return pl.BlockSpec(block, index_map)


def _even_out(ya, yb, w, layer, x2, *, tm=512, tn=2048):
    m, d = x2.shape
    ka, kb = ya.shape[1], yb.shape[1]
    assert ka == kb
    return pl.pallas_call(
        _even_out_kernel,
        grid=(m // tm, d // tn),
        in_specs=[pl.BlockSpec((tm, ka), lambda i, j: (i, 0)),
                  pl.BlockSpec((tm, kb), lambda i, j: (i, 0)),
                  _resident((None, ka, tn), lambda i, j: (layer, 0, j), d // tn),
                  _resident((None, kb, tn), lambda i, j: (layer, 1, j), d // tn),
                  pl.BlockSpec((tm, tn), lambda i, j: (i, j))],
        out_specs=pl.BlockSpec((tm, tn), lambda i, j: (i, j)),
        out_shape=jax.ShapeDtypeStruct((m, d), F32),
        compiler_params=_params(("arbitrary", "arbitrary")),
        name="even_out",
    )(ya, yb, w, w, x2)


def _odd_out_kernel(y_ref, w_ref, x_ref, o_ref):
    o_ref[...] = x_ref[...] + jnp.dot(y_ref[...], w_ref[...], preferred_element_type=F32)


def _odd_out(y, w, layer, x2, *, tm=512, tn=2048):
    m, d = x2.shape
    k = y.shape[1]
    return pl.pallas_call(
        _odd_out_kernel,
        grid=(m // tm, d // tn),
        in_specs=[pl.BlockSpec((tm, k), lambda i, j: (i, 0)),
                  _resident((None, k, tn), lambda i, j: (layer, 0, j), d // tn),
                  pl.BlockSpec((tm, tn), lambda i, j: (i, j))],
        out_specs=pl.BlockSpec((tm, tn), lambda i, j: (i, j)),
        out_shape=jax.ShapeDtypeStruct((m, d), F32),
        compiler_params=_params(("arbitrary", "arbitrary")),
        name="odd_out",
    )(y, w, x2)


def kernel(x, ev_ln_g, ev_w_in, ev_qk_g, ev_sinks, ev_w_out, od_ln_g, od_w_in, od_v_g, od_w_s,
           od_b_s, od_w_out, rel_bias):
    bsz, seq, d = x.shape
    n_even, n_odd = ev_ln_g.shape[0], od_ln_g.shape[0]
    flat = lambda w: w.reshape(-1, w.shape[-1])
    whole = lambda w: _Cast(flat(w), 0, w.shape[0] * w.shape[1])
    layer_of = lambda w, j: _Cast(flat(w), j * w.shape[1], w.shape[1])
    tabs, done = _bias_tables(rel_bias, [layer_of(ev_w_in, 0)])
    ev_in_b = {0: done[0][None]}
    od_in_b, ev_out_b, od_out_b = {}, None, None
    x2 = x.reshape(bsz * seq, d)
    for i in range(n_even + n_odd):
        j = i // 2
        if i % 2 == 0:
            z = _even_in(x2, ev_ln_g[j].reshape(1, d), ev_in_b[j], 0)
            z3 = z.reshape(bsz, seq, EVEN_IN)
            gains = jnp.tile(ev_qk_g[j].astype(F32), (1, 2))
            snk3 = jnp.repeat(ev_sinks[j].astype(F32), HEAD_DIM).reshape(A_KV_HEADS, -1, LANES)
            ya, done = _attn_a(z3, gains[0:1], gains[1:2], snk3, tabs, [whole(ev_w_out)] if j == 0 else [])
            if j == 0:
                ev_out_b = done[0].reshape(ev_w_out.shape)
            yb, done = _attn_b(z3, gains[2:3], gains[3:4], tabs, [layer_of(od_w_in, j)] if j < n_odd else [])
            if j < n_odd:
                od_in_b[j] = done[0][None]
            x2 = _even_out(ya.reshape(bsz * seq, A_WIDTH), yb.reshape(bsz * seq, B_WIDTH), ev_out_b, j, x2)
        else:
            casts = ([whole(od_w_out)] if j == 0 else []) + ([layer_of(ev_w_in, j + 1)] if j + 1 < n_even else [])
            y, done = _odd_in(x2, od_ln_g[j].reshape(1, d), od_in_b[j], 0, od_v_g[j].astype(F32),
                              od_w_s[j].astype(F32), od_b_s[j].astype(F32), casts)
            if j == 0:
                od_out_b = done.pop(0).reshape(od_w_out.shape)
            if j + 1 < n_even:
                ev_in_b[j + 1] = done.pop(0)[None]
            x2 = _odd_out(y, od_out_b, j, x2)
    return x2.reshape(bsz, seq, d)
```

```python
import functools
import math
from typing import NamedTuple

import numpy as np
import jax
import jax.numpy as jnp
from jax import lax
from jax.experimental import pallas as pl
from jax.experimental.pallas import tpu as pltpu

F32 = jnp.float32
BF16 = jnp.bfloat16

D_MODEL = 2048
HEAD_DIM = 64
A_HEADS = 16
A_KV_HEADS = 2
B_HEADS = 16
BLK = 128
NUM_BUCKETS = 32
REL_MAX_DISTANCE = 2048
A_WIDTH = A_HEADS * HEAD_DIM
B_WIDTH = B_HEADS * HEAD_DIM
EVEN_IN = 6400
C_WIDTH = 2 * D_MODEL
C_GROUPS = 16
C_GROUP_DIM = C_WIDTH // C_GROUPS
C_CHUNK = 128
EPS = 1e-6
NEG = -1e30
SCALE = HEAD_DIM ** -0.5
LOG2E = 1.4426950408889634
PATTERNS = ((1, 128), (4, 128), (16, 128), (1, 127))
N_B_PATTERNS = 3

LANES = 128
BF16_SUBLANES = 16
NORM_ROWS = 128
VMEM_LIMIT = 56 * 1024 * 1024

_QA, _KA, _VA, _GA = 0, 8, 9, 10
_QB, _KB, _VB, _GB = 18, 26, 34, 42


def _params(sem):
    return pltpu.CompilerParams(dimension_semantics=sem, vmem_limit_bytes=VMEM_LIMIT)


class _Cast(NamedTuple):
    src: jax.Array
    first_row: int
    n_rows: int


def _call_with_casts(body, *, grid, in_specs, out_specs, out_shape, scratch_shapes, name, args, casts=()):
    n_in, n_cast = len(in_specs), len(casts)
    steps = grid[0] * grid[1]
    cast_in, cast_out, cast_shape = [], [], []
    for c in casts:
        cols = c.src.shape[1]
        tile = c.n_rows // steps
        first = c.first_row // tile
        assert tile * steps == c.n_rows and first * tile == c.first_row and tile % BF16_SUBLANES == 0
        cast_in.append(pl.BlockSpec((tile, cols), lambda i, j, first=first: (first + i * grid[1] + j, 0)))
        cast_out.append(pl.BlockSpec((tile, cols), lambda i, j: (i * grid[1] + j, 0)))
        cast_shape.append(jax.ShapeDtypeStruct((c.n_rows, cols), BF16))

    def kernel(*refs):
        ins, srcs = refs[:n_in], refs[n_in:n_in + n_cast]
        out, dsts = refs[n_in + n_cast], refs[n_in + n_cast + 1:n_in + 2 * n_cast + 1]
        for src, dst in zip(srcs, dsts):
            dst[...] = src[...].astype(dst.dtype)
        body(*ins, out, *refs[n_in + 2 * n_cast + 1:])

    res = pl.pallas_call(
        kernel,
        grid=grid,
        in_specs=list(in_specs) + cast_in,
        out_specs=[out_specs] + cast_out,
        out_shape=[out_shape] + cast_shape,
        scratch_shapes=scratch_shapes,
        compiler_params=_params(("arbitrary", "arbitrary")),
        name=name,
    )(*args, *[c.src for c in casts])
    return res[0], list(res[1:])


def _bucket_tables():
    a = np.arange(BLK)[:, None]
    b = np.arange(2 * BLK)[None, :]
    dist = BLK + a - b
    max_exact = NUM_BUCKETS // 2
    out = []
    for dil, max_dist in PATTERNS:
        n = np.maximum(dist * dil, 0)
        large = max_exact + (np.log(np.maximum(n, 1) / max_exact)
                             / np.log(REL_MAX_DISTANCE / max_exact)
                             * (NUM_BUCKETS - max_exact)).astype(np.int32)
        large = np.minimum(large, NUM_BUCKETS - 1)
        bucket = np.where(n < max_exact, n, large).astype(np.int32)
        valid = (dist >= 0) & (dist <= max_dist)
        out.append(np.where(valid, bucket, -1).astype(np.int32))
    return np.stack(out)


def _bias_kernel(tbl_ref, bucket_ref, out_ref):
    col0 = jnp.where(pl.program_id(0) < N_B_PATTERNS, A_HEADS, 0)
    bk = bucket_ref[...]
    prev_cols = lax.broadcasted_iota(jnp.int32, bk.shape, 1) < BLK

    def head(h, carry):
        acc = jnp.full(bk.shape, NEG, F32)
        for b in range(NUM_BUCKETS):
            acc = jnp.where(bk == b, tbl_ref[b, col0 + h] * LOG2E, acc)
        out_ref[h, 0] = acc
        out_ref[h, 1] = jnp.where(prev_cols, NEG, acc)
        return carry
    lax.fori_loop(0, out_ref.shape[0], head, 0)


def _bias_tables(rel_bias, casts=()):
    buckets = jnp.asarray(_bucket_tables())
    heads = max(A_HEADS, B_HEADS)
    return _call_with_casts(
        _bias_kernel,
        grid=(len(PATTERNS), 1),
        in_specs=[pl.BlockSpec(memory_space=pltpu.SMEM),
                  pl.BlockSpec((None, BLK, 2 * BLK), lambda t, _: (t, 0, 0))],
        out_specs=pl.BlockSpec((None, heads, 2, BLK, 2 * BLK), lambda t, _: (t, 0, 0, 0, 0)),
        out_shape=jax.ShapeDtypeStruct((len(PATTERNS), heads, 2, BLK, 2 * BLK), F32),
        scratch_shapes=[],
        name="bias_tables",
        args=(rel_bias.astype(F32), buckets),
        casts=casts)


def _norm_rows(x_ref, g_ref, h_scr, tm):
    def body(c, carry):
        r = pl.ds(c * NORM_ROWS, NORM_ROWS)
        x = x_ref[r, :]
        ms = jnp.mean(x * x, axis=-1, keepdims=True)
        h_scr[r, :] = (x * lax.rsqrt(ms + EPS) * g_ref[...]).astype(BF16)
        return carry
    lax.fori_loop(0, tm // NORM_ROWS, body, 0, unroll=2)


def _even_in_kernel(x_ref, g_ref, w_ref, o_ref, h_scr, *, tm):
    @pl.when(pl.program_id(1) == 0)
    def _():
        _norm_rows(x_ref, g_ref, h_scr, tm)
    o_ref[...] = jnp.dot(h_scr[...], w_ref[...], preferred_element_type=F32).astype(o_ref.dtype)


def _even_in(x2, g, w, layer, *, tm=1024, tn=1280):
    m, d = x2.shape
    n = w.shape[2]
    return pl.pallas_call(
        functools.partial(_even_in_kernel, tm=tm),
        grid=(m // tm, n // tn),
        in_specs=[pl.BlockSpec((tm, d), lambda i, j: (i, 0)),
                  pl.BlockSpec((1, d), lambda i, j: (0, 0)),
                  pl.BlockSpec((None, d, tn), lambda i, j: (layer, 0, j))],
        out_specs=pl.BlockSpec((tm, tn), lambda i, j: (i, j)),
        out_shape=jax.ShapeDtypeStruct((m, n), BF16),
        scratch_shapes=[pltpu.VMEM((tm, d), BF16)],
        compiler_params=_params(("arbitrary", "arbitrary")),
        name="even_in",
    )(x2, g, w)


def _gelu(x):
    return 0.5 * x * (1.0 + lax.erf(x * (1.0 / math.sqrt(2.0))))


def _silu(x):
    return (0.5 * x) * (1.0 + jnp.tanh(0.5 * x))


def _odd_in_kernel(x_ref, g_ref, wv_ref, wu_ref, wg_ref, vg_ref, ws_ref, bs_ref, y_ref,
                   h_scr, v_scr, ssq_scr, *, tm, tn):
    j = pl.program_id(1)
    nb = C_WIDTH // tn

    @pl.when(j == 0)
    def _():
        _norm_rows(x_ref, g_ref, h_scr, tm)
        ssq_scr[...] = jnp.zeros_like(ssq_scr)

    @pl.when(j < nb)
    def _():
        v = _gelu(jnp.dot(h_scr[...], wv_ref[...], preferred_element_type=F32))
        v_scr[j] = v.astype(v_scr.dtype)
        ssq_scr[...] += jnp.sum(v * v, axis=-1, keepdims=True)

    @pl.when(j >= nb)
    def _():
        jj = j - nb
        h = h_scr[...]
        ug = (_gelu(jnp.dot(h, wu_ref[...], preferred_element_type=F32))
              * _silu(jnp.dot(h, wg_ref[...], preferred_element_type=F32)))
        tril = (lax.broadcasted_iota(jnp.int32, (C_CHUNK, C_CHUNK), 0)
                >= lax.broadcasted_iota(jnp.int32, (C_CHUNK, C_CHUNK), 1))
        groups_per_tile = tn // C_GROUP_DIM
        vg = vg_ref[jj]
        for gg in range(groups_per_tile):
            grp = jj * groups_per_tile + gg
            wt = jnp.where(tril, ws_ref[grp], 0.0).astype(BF16)
            cols = slice(gg * C_GROUP_DIM, (gg + 1) * C_GROUP_DIM)
            for c in range(tm // C_CHUNK):
                rows = slice(c * C_CHUNK, (c + 1) * C_CHUNK)
                inv = lax.rsqrt(ssq_scr[rows, :][:, :1] * (1.0 / C_WIDTH) + EPS)
                vn = (v_scr[jj, rows, cols].astype(F32) * inv * vg[:, cols]).astype(BF16)
                s = jnp.dot(wt, vn, preferred_element_type=F32) + bs_ref[grp]
                y_ref[rows, cols] = (ug[rows, cols] * s).astype(y_ref.dtype)


def _odd_in(x2, g, w, layer, vg, ws, bs, casts=(), *, tm=1024, tn=512):
    m, d = x2.shape
    nb = C_WIDTH // tn
    return _call_with_casts(
        functools.partial(_odd_in_kernel, tm=tm, tn=tn),
        grid=(m // tm, 2 * nb),
        in_specs=[pl.BlockSpec((tm, d), lambda i, j: (i, 0)),
                  pl.BlockSpec((1, d), lambda i, j: (0, 0)),
                  pl.BlockSpec((None, d, tn), lambda i, j: (layer, 0, nb + jnp.minimum(j, nb - 1))),
                  pl.BlockSpec((None, d, tn), lambda i, j: (layer, 0, jnp.maximum(j - nb, 0))),
                  pl.BlockSpec((None, d, tn), lambda i, j: (layer, 0, 2 * nb + jnp.maximum(j - nb, 0))),
                  pl.BlockSpec((nb, 1, tn), lambda i, j: (0, 0, 0)),
                  pl.BlockSpec((C_GROUPS, C_CHUNK, C_CHUNK), lambda i, j: (0, 0, 0)),
                  pl.BlockSpec((C_GROUPS, C_CHUNK, 1), lambda i, j: (0, 0, 0))],
        out_specs=pl.BlockSpec((tm, tn), lambda i, j: (i, jnp.maximum(j - nb, 0))),
        out_shape=jax.ShapeDtypeStruct((m, C_WIDTH), BF16),
        scratch_shapes=[pltpu.VMEM((tm, d), BF16),
                        pltpu.VMEM((nb, tm, tn), BF16),
                        pltpu.VMEM((tm, LANES), F32)],
        name="odd_in",
        args=(x2, g, w, w, w, vg.reshape(nb, 1, tn), ws, bs.reshape(C_GROUPS, C_CHUNK, 1)),
        casts=casts)


PAD = BLK
UNROLL = 4
NSLOT = 3


def _and(b, mask):
    return b & mask if isinstance(b, int) else lax.bitwise_and(b, mask)


def _shr(b, s):
    return b >> s if isinstance(b, int) else lax.shift_right_logical(b, s)


def _lane_lo(rows):
    return lax.broadcasted_iota(jnp.int32, (rows, LANES), 1) < HEAD_DIM


def _head_rms(x, gain):
    w = x.shape[1]
    r = lax.broadcasted_iota(jnp.int32, (w, w), 0) // HEAD_DIM
    c = lax.broadcasted_iota(jnp.int32, (w, w), 1) // HEAD_DIM
    ones_bd = (r == c).astype(BF16)
    x2 = x * x
    hi = x2.astype(BF16)
    lo = (x2 - hi.astype(F32)).astype(BF16)
    ssq = (jnp.dot(hi, ones_bd, preferred_element_type=F32)
           + jnp.dot(lo, ones_bd, preferred_element_type=F32))
    return x * lax.rsqrt(ssq * (1.0 / HEAD_DIM) + EPS) * gain


def _attn_a_kernel(q0_ref, q1_ref, q2_ref, q3_ref, k_ref, v_ref, g0_ref, g1_ref, g2_ref, g3_ref,
                   gq_ref, gk_ref, snk_ref, bias_ref, o_ref, qb, kb, v0b, v1b, s_scr, mb_scr, p_scr, m_scr, *, seq):
    q_refs = (q0_ref, q1_ref, q2_ref, q3_ref)
    g_refs = (g0_ref, g1_ref, g2_ref, g3_ref)
    npair = len(q_refs)
    nblk = seq // BLK
    lane = lax.broadcasted_iota(jnp.int32, (BLK, LANES), 1)
    keep = (lane // HEAD_DIM) == pl.program_id(1)
    lo = lane < HEAD_DIM
    hi = jnp.logical_not(lo)
    zeros = jnp.zeros((PAD, LANES), BF16)
    kb[pl.ds(0, PAD), :] = zeros
    v0b[pl.ds(0, PAD), :] = zeros
    v1b[pl.ds(0, PAD), :] = zeros
    gq = gq_ref[...] * (SCALE * LOG2E)
    gq = jnp.concatenate([gq, gq], axis=1)

    def prep(c, carry):
        r = pl.ds(c * BLK, BLK)
        ro = pl.ds(PAD + c * BLK, BLK)
        for pp in range(0, npair, 2):
            qq = jnp.concatenate([q_refs[pp][r, :], q_refs[pp + 1][r, :]], axis=1).astype(F32)
            qn = _head_rms(qq, gq).astype(BF16)
            qb[pp, r, :] = qn[:, :LANES]
            qb[pp + 1, r, :] = qn[:, LANES:]
        kn = _head_rms(k_ref[r, :].astype(F32), gk_ref[...])
        kb[ro, :] = jnp.where(keep, kn, pltpu.roll(kn, HEAD_DIM, 1)).astype(BF16)
        v = v_ref[r, :].astype(F32)
        vd = jnp.where(keep, v, pltpu.roll(v, HEAD_DIM, 1))
        v0b[ro, :] = jnp.where(lo, vd, 1.0).astype(BF16)
        v1b[ro, :] = jnp.where(lo, 1.0, vd).astype(BF16)
        return carry
    lax.fori_loop(0, nblk, prep, 0, unroll=4)

    def scores(n):
        kk = kb[pl.ds(n * BLK, 2 * BLK), :]
        for pp in range(npair):
            q = qb[pp, pl.ds(n * BLK, BLK), :]
            for hh, sel in enumerate((lo, hi)):
                s_scr[n % NSLOT, pp, hh] = lax.dot_general(
                    jnp.where(sel, q, jnp.zeros_like(q)), kk, (((1,), (1,)), ((), ())),
                    preferred_element_type=F32) + bias_ref[2 * pp + hh, int(n == 0)]

    def row_max(n):
        slot = n % NSLOT
        for pp in range(npair):
            for hh in range(2):
                m = jnp.max(s_scr[slot, pp, hh], axis=-1, keepdims=True)
                mb_scr[slot, pp, hh] = jnp.broadcast_to(m, (BLK, LANES))
            m_scr[slot, pp] = jnp.where(lo, mb_scr[slot, pp, 0], mb_scr[slot, pp, 1])

    def exponentials(n):
        slot = n % NSLOT
        for pp in range(npair):
            for hh in range(2):
                mb = mb_scr[slot, pp, hh]
                p_scr[slot, pp, hh] = jnp.exp2(s_scr[slot, pp, hh] - jnp.concatenate([mb, mb], axis=1)).astype(BF16)

    def weighted_values(n):
        slot = n % NSLOT
        r = pl.ds(n * BLK, BLK)
        rk = pl.ds(n * BLK, 2 * BLK)
        for pp in range(npair):
            ul0 = jnp.dot(p_scr[slot, pp, 0], v0b[rk, :], preferred_element_type=F32)
            ul1 = jnp.dot(p_scr[slot, pp, 1], v1b[rk, :], preferred_element_type=F32)
            u = jnp.where(lo, ul0, ul1)
            l = pltpu.roll(jnp.where(lo, ul1, ul0), HEAD_DIM, 1)
            m = m_scr[slot, pp]
            snk = snk_ref[pp:pp + 1, :] * LOG2E
            mx = jnp.maximum(m, snk)
            w = jnp.exp2(m - mx)
            o = u * (w / (l * w + jnp.exp2(snk - mx)))
            o_ref[r, pl.ds(pp * LANES, LANES)] = (o * _silu(g_refs[pp][r, :].astype(F32))).astype(o_ref.dtype)

    for n in range(nblk + 3):
        if n - 3 >= 0:
            weighted_values(n - 3)
        if n < nblk:
            scores(n)
        if 0 <= n - 1 < nblk:
            row_max(n - 1)
        if 0 <= n - 2 < nblk:
            exponentials(n - 2)


def _attn_a(z3, gq2, gk2, snk3, tabs, casts=()):
    bsz, seq, _ = z3.shape
    npair = A_HEADS // A_KV_HEADS // 2

    def col(off, pp):
        return pl.BlockSpec((None, seq, LANES), lambda b, gi: (b, 0, off + npair * gi + pp))

    def fixed(off):
        return pl.BlockSpec((None, seq, LANES), lambda b, gi: (b, 0, off))

    return _call_with_casts(
        functools.partial(_attn_a_kernel, seq=seq),
        grid=(bsz, A_KV_HEADS),
        in_specs=[col(_QA, 0), col(_QA, 1), col(_QA, 2), col(_QA, 3), fixed(_KA), fixed(_VA),
                  col(_GA, 0), col(_GA, 1), col(_GA, 2), col(_GA, 3),
                  pl.BlockSpec((1, LANES), lambda b, gi: (0, 0)),
                  pl.BlockSpec((1, LANES), lambda b, gi: (0, 0)),
                  pl.BlockSpec((None, npair, LANES), lambda b, gi: (gi, 0, 0)),
                  pl.BlockSpec((None, 2 * npair, 2, BLK, 2 * BLK), lambda b, gi: (N_B_PATTERNS, gi, 0, 0, 0))],
        out_specs=pl.BlockSpec((None, seq, npair * LANES), lambda b, gi: (b, 0, gi)),
        out_shape=jax.ShapeDtypeStruct((bsz, seq, A_WIDTH), BF16),
        scratch_shapes=[pltpu.VMEM((npair, seq, LANES), BF16),
                        pltpu.VMEM((PAD + seq, LANES), BF16),
                        pltpu.VMEM((PAD + seq, LANES), BF16),
                        pltpu.VMEM((PAD + seq, LANES), BF16),
                        pltpu.VMEM((NSLOT, npair, 2, BLK, 2 * BLK), F32),
                        pltpu.VMEM((NSLOT, npair, 2, BLK, LANES), F32),
                        pltpu.VMEM((NSLOT, npair, 2, BLK, 2 * BLK), BF16),
                        pltpu.VMEM((NSLOT, npair, BLK, LANES), F32)],
        name="attn_a",
        args=(z3, z3, z3, z3, z3, z3, z3, z3, z3, z3, gq2, gk2, snk3, tabs),
        casts=casts)


def _attn_b_kernel(q_ref, k_ref, v_ref, g_ref, gq_ref, gk_ref, bias_ref, o_ref,
                   qf, kf, vf, q4f, k4f, v4f, qb, kb, v0b, v1b, s_scr, mb_scr, p_scr, m_scr, l_scr, u_scr, *, seq):
    lo = _lane_lo(BLK)
    hi = jnp.logical_not(lo)
    nblk = seq // BLK
    zeros = jnp.zeros((PAD, LANES), BF16)
    for pi in range(2):
        kb[pi, pl.ds(0, PAD), :] = zeros
        v0b[pi, pl.ds(0, PAD), :] = zeros
        v1b[pi, pl.ds(0, PAD), :] = zeros

    def put(pi, dst, q, k, v):
        qb[pi, dst, :] = q.astype(BF16)
        kb[pi, dst, :] = k.astype(BF16)
        v0b[pi, dst, :] = jnp.where(lo, v, 1.0).astype(BF16)
        v1b[pi, dst, :] = jnp.where(lo, 1.0, v).astype(BF16)

    def strided(t):
        return pl.ds(_and(t, 3) * (4 * BLK) + _shr(t, 2), BLK, stride=4)

    gains = jnp.concatenate([gq_ref[...] * (SCALE * LOG2E), gk_ref[...]], axis=1)

    def prep(c, carry):
        r = pl.ds(c * BLK, BLK)
        qk = jnp.concatenate([q_ref[r, :], k_ref[r, :]], axis=1).astype(F32)
        n = _head_rms(qk, gains)
        q, k, v = n[:, :LANES], n[:, LANES:], v_ref[r, :].astype(F32)
        qf[r, :] = q
        kf[r, :] = k
        vf[r, :] = v
        put(0, pl.ds(PAD + c * BLK, BLK), q, k, v)
        return carry
    for c in range(nblk):
        prep(c, 0)

    def deint4(t, carry):
        src = strided(t)
        dst = pl.ds(t * BLK, BLK)
        q, k, v = qf[src, :], kf[src, :], vf[src, :]
        q4f[dst, :] = q
        k4f[dst, :] = k
        v4f[dst, :] = v
        put(1, pl.ds(PAD + t * BLK, BLK), q, k, v)
        return carry
    for t in range(nblk):
        deint4(t, 0)

    def deint16(t, carry):
        src = strided(t)
        put(2, pl.ds(PAD + t * BLK, BLK), q4f[src, :], k4f[src, :], v4f[src, :])
        return carry
    for t in range(nblk):
        deint16(t, 0)

    groups = [(pi, g) for pi in range(N_B_PATTERNS) for g in range(nblk // UNROLL)]

    def geometry(pi):
        single = pi == 2
        return single, (BLK if single else 2 * BLK), (PAD if single else PAD - BLK)

    def out_rows(pi, b):
        return pl.ds(b * BLK, BLK) if pi == 0 else strided(b)

    def scores(t):
        pi, g = groups[t]
        single, keys, koff = geometry(pi)
        for uu in range(UNROLL):
            b = g * UNROLL + uu
            first = int(b % (nblk if pi == 0 else 4) == 0)
            q = qb[pi, pl.ds(PAD + b * BLK, BLK), :]
            kk = kb[pi, pl.ds(koff + b * BLK, keys), :]
            for hh, sel in enumerate((lo, hi)):
                bias = bias_ref[pi, hh, 0, :, pl.ds(BLK, BLK)] if single else bias_ref[pi, hh, first]
                s_scr[(t % NSLOT) * UNROLL + uu, hh, :, pl.ds(0, keys)] = lax.dot_general(
                    jnp.where(sel, q, jnp.zeros_like(q)), kk, (((1,), (1,)), ((), ())),
                    preferred_element_type=F32) + bias

    def row_max(t):
        pi, g = groups[t]
        _, keys, _ = geometry(pi)
        for uu in range(UNROLL):
            b = g * UNROLL + uu
            slot = (t % NSLOT) * UNROLL + uu
            for hh in range(2):
                m = jnp.max(s_scr[slot, hh, :, pl.ds(0, keys)], axis=-1, keepdims=True)
                mb_scr[slot, hh] = jnp.broadcast_to(m, (BLK, LANES))
            m_scr[pi, out_rows(pi, b), :] = jnp.where(lo, mb_scr[slot, 0], mb_scr[slot, 1])

    def exponentials(t):
        pi, g = groups[t]
        _, keys, _ = geometry(pi)
        for uu in range(UNROLL):
            slot = (t % NSLOT) * UNROLL + uu
            for hh in range(2):
                mb = mb_scr[slot, hh]
                mb = mb if keys == BLK else jnp.concatenate([mb, mb], axis=1)
                p_scr[slot, hh, :, pl.ds(0, keys)] = jnp.exp2(s_scr[slot, hh, :, pl.ds(0, keys)] - mb).astype(BF16)

    def weighted_values(t):
        pi, g = groups[t]
        _, keys, koff = geometry(pi)
        for uu in range(UNROLL):
            b = g * UNROLL + uu
            slot = (t % NSLOT) * UNROLL + uu
            rk = pl.ds(koff + b * BLK, keys)
            ul0 = jnp.dot(p_scr[slot, 0, :, pl.ds(0, keys)], v0b[pi, rk, :], preferred_element_type=F32)
            ul1 = jnp.dot(p_scr[slot, 1, :, pl.ds(0, keys)], v1b[pi, rk, :], preferred_element_type=F32)
            u_scr[pi, out_rows(pi, b), :] = jnp.where(lo, ul0, ul1)
            l_scr[pi, out_rows(pi, b), :] = pltpu.roll(jnp.where(lo, ul1, ul0), HEAD_DIM, 1)

    for t in range(len(groups) + 3):
        if t - 3 >= 0:
            weighted_values(t - 3)
        if t < len(groups):
            scores(t)
        if 0 <= t - 1 < len(groups):
            row_max(t - 1)
        if 0 <= t - 2 < len(groups):
            exponentials(t - 2)

    def renat(t, carry):
        for ref in (m_scr, l_scr, u_scr):
            ref[3, strided(t), :] = ref[2, pl.ds(t * BLK, BLK), :]
        return carry
    lax.fori_loop(0, nblk, renat, 0)

    def combine(c, carry):
        r = pl.ds(c * BLK, BLK)
        slots = (0, 1, 3)
        ms = [m_scr[s, r, :] for s in slots]
        mx = jnp.maximum(jnp.maximum(ms[0], ms[1]), ms[2])
        num = jnp.zeros((BLK, LANES), F32)
        den = jnp.zeros((BLK, LANES), F32)
        for s, m in zip(slots, ms):
            w = jnp.exp2(m - mx)
            num = num + w * u_scr[s, r, :]
            den = den + w * l_scr[s, r, :]
        o_ref[r, :] = ((num / den) * _silu(g_ref[r, :].astype(F32))).astype(o_ref.dtype)
        return carry
    lax.fori_loop(0, nblk, combine, 0, unroll=2)


def _attn_b(z3, gq2, gk2, tabs, casts=()):
    bsz, seq, _ = z3.shape
    col = lambda off: pl.BlockSpec((None, seq, LANES), lambda b, p: (b, 0, off + p))
    f32_rows = pltpu.VMEM((seq, LANES), F32)
    bf16_ops = pltpu.VMEM((N_B_PATTERNS, PAD + seq, LANES), BF16)
    stats = pltpu.VMEM((N_B_PATTERNS + 1, seq, LANES), F32)
    return _call_with_casts(
        functools.partial(_attn_b_kernel, seq=seq),
        grid=(bsz, B_HEADS // 2),
        in_specs=[col(_QB), col(_KB), col(_VB), col(_GB),
                  pl.BlockSpec((1, LANES), lambda b, p: (0, 0)),
                  pl.BlockSpec((1, LANES), lambda b, p: (0, 0)),
                  pl.BlockSpec((N_B_PATTERNS, 2, 2, BLK, 2 * BLK), lambda b, p: (0, p, 0, 0, 0))],
        out_specs=pl.BlockSpec((None, seq, LANES), lambda b, p: (b, 0, p)),
        out_shape=jax.ShapeDtypeStruct((bsz, seq, B_WIDTH), BF16),
        scratch_shapes=[f32_rows] * 6 + [bf16_ops] * 4
                       + [pltpu.VMEM((NSLOT * UNROLL, 2, BLK, 2 * BLK), F32),
                          pltpu.VMEM((NSLOT * UNROLL, 2, BLK, LANES), F32),
                          pltpu.VMEM((NSLOT * UNROLL, 2, BLK, 2 * BLK), BF16)] + [stats] * 3,
        name="attn_b",
        args=(z3, z3, z3, z3, gq2, gk2, tabs),
        casts=casts)


def _even_out_kernel(ya_ref, yb_ref, wa_ref, wb_ref, x_ref, o_ref):
    acc = jnp.dot(ya_ref[...], wa_ref[...], preferred_element_type=F32)
    acc = acc + jnp.dot(yb_ref[...], wb_ref[...], preferred_element_type=F32)
    o_ref[...] = x_ref[...] + acc


def _resident(block, index_map, n_col_tiles):
    if n_col_tiles == 1:
        return pl.BlockSpec(block, index_map, pipeline_mode=pl.Buffered(1))
    return pl.BlockSpec(block, index_map)


def _even_out(ya, yb, w, layer, x2, *, tm=512, tn=2048):
    m, d = x2.shape
    ka, kb = ya.shape[1], yb.shape[1]
    assert ka == kb
    return pl.pallas_call(
        _even_out_kernel,
        grid=(m // tm, d // tn),
        in_specs=[pl.BlockSpec((tm, ka), lambda i, j: (i, 0)),
                  pl.BlockSpec((tm, kb), lambda i, j: (i, 0)),
                  _resident((None, ka, tn), lambda i, j: (layer, 0, j), d // tn),
                  _resident((None, kb, tn), lambda i, j: (layer, 1, j), d // tn),
                  pl.BlockSpec((tm, tn), lambda i, j: (i, j))],
        out_specs=pl.BlockSpec((tm, tn), lambda i, j: (i, j)),
        out_shape=jax.ShapeDtypeStruct((m, d), F32),
        compiler_params=_params(("arbitrary", "arbitrary")),
        name="even_out",
    )(ya, yb, w, w, x2)


def _odd_out_kernel(y_ref, w_ref, x_ref, o_ref):
    o_ref[...] = x_ref[...] + jnp.dot(y_ref[...], w_ref[...], preferred_element_type=F32)


def _odd_out(y, w, layer, x2, *, tm=512, tn=2048):
    m, d = x2.shape
    k = y.shape[1]
    return pl.pallas_call(
        _odd_out_kernel,
        grid=(m // tm, d // tn),
        in_specs=[pl.BlockSpec((tm, k), lambda i, j: (i, 0)),
                  _resident((None, k, tn), lambda i, j: (layer, 0, j), d // tn),
                  pl.BlockSpec((tm, tn), lambda i, j: (i, j))],
        out_specs=pl.BlockSpec((tm, tn), lambda i, j: (i, j)),
        out_shape=jax.ShapeDtypeStruct((m, d), F32),
        compiler_params=_params(("arbitrary", "arbitrary")),
        name="odd_out",
    )(y, w, x2)


def kernel(x, ev_ln_g, ev_w_in, ev_qk_g, ev_sinks, ev_w_out, od_ln_g, od_w_in, od_v_g, od_w_s,
           od_b_s, od_w_out, rel_bias):
    bsz, seq, d = x.shape
    n_even, n_odd = ev_ln_g.shape[0], od_ln_g.shape[0]
    flat = lambda w: w.reshape(-1, w.shape[-1])
    whole = lambda w: _Cast(flat(w), 0, w.shape[0] * w.shape[1])
    layer_of = lambda w, j: _Cast(flat(w), j * w.shape[1], w.shape[1])
    tabs, done = _bias_tables(rel_bias, [layer_of(ev_w_in, 0)])
    ev_in_b = {0: done[0][None]}
    od_in_b, ev_out_b, od_out_b = {}, None, None
    x2 = x.reshape(bsz * seq, d)
    for i in range(n_even + n_odd):
        j = i // 2
        if i % 2 == 0:
            z = _even_in(x2, ev_ln_g[j].reshape(1, d), ev_in_b[j], 0)
            z3 = z.reshape(bsz, seq, EVEN_IN)
            gains = jnp.tile(ev_qk_g[j].astype(F32), (1, 2))
            snk3 = jnp.repeat(ev_sinks[j].astype(F32), HEAD_DIM).reshape(A_KV_HEADS, -1, LANES)
            ya, done = _attn_a(z3, gains[0:1], gains[1:2], snk3, tabs, [whole(ev_w_out)] if j == 0 else [])
            if j == 0:
                ev_out_b = done[0].reshape(ev_w_out.shape)
            yb, done = _attn_b(z3, gains[2:3], gains[3:4], tabs, [layer_of(od_w_in, j)] if j < n_odd else [])
            if j < n_odd:
                od_in_b[j] = done[0][None]
            x2 = _even_out(ya.reshape(bsz * seq, A_WIDTH), yb.reshape(bsz * seq, B_WIDTH), ev_out_b, j, x2)
        else:
            casts = ([whole(od_w_out)] if j == 0 else []) + ([layer_of(ev_w_in, j + 1)] if j + 1 < n_even else [])
            y, done = _odd_in(x2, od_ln_g[j].reshape(1, d), od_in_b[j], 0, od_v_g[j].astype(F32),
                              od_w_s[j].astype(F32), od_b_s[j].astype(F32), casts)
            if j == 0:
                od_out_b = done.pop(0).reshape(od_w_out.shape)
            if j + 1 < n_even:
                ev_in_b[j + 1] = done.pop(0)[None]
            x2 = _odd_out(y, od_out_b, j, x2)
    return x2.reshape(bsz, seq, d)
```

```python
import functools
import math
from typing import NamedTuple

import numpy as np
import jax
import jax.numpy as jnp
from jax import lax
from jax.experimental import pallas as pl
from jax.experimental.pallas import tpu as pltpu

F32 = jnp.float32
BF16 = jnp.bfloat16

D_MODEL = 2048
HEAD_DIM = 64
A_HEADS = 16
A_KV_HEADS = 2
B_HEADS = 16
BLK = 128
NUM_BUCKETS = 32
REL_MAX_DISTANCE = 2048
A_WIDTH = A_HEADS * HEAD_DIM
B_WIDTH = B_HEADS * HEAD_DIM
EVEN_IN = 6400
C_WIDTH = 2 * D_MODEL
C_GROUPS = 16
C_GROUP_DIM = C_WIDTH // C_GROUPS
C_CHUNK = 128
EPS = 1e-6
NEG = -1e30
SCALE = HEAD_DIM ** -0.5
LOG2E = 1.4426950408889634
PATTERNS = ((1, 128), (4, 128), (16, 128), (1, 127))
N_B_PATTERNS = 3

LANES = 128
BF16_SUBLANES = 16
NORM_ROWS = 128
VMEM_LIMIT = 56 * 1024 * 1024

_QA, _KA, _VA, _GA = 0, 8, 9, 10
_QB, _KB, _VB, _GB = 18, 26, 34, 42


def _params(sem):
    return pltpu.CompilerParams(dimension_semantics=sem, vmem_limit_bytes=VMEM_LIMIT)


class _Cast(NamedTuple):
    src: jax.Array
    first_row: int
    n_rows: int


def _call_with_casts(body, *, grid, in_specs, out_specs, out_shape, scratch_shapes, name, args, casts=()):
    n_in, n_cast = len(in_specs), len(casts)
    steps = grid[0] * grid[1]
    cast_in, cast_out, cast_shape = [], [], []
    for c in casts:
        cols = c.src.shape[1]
        tile = c.n_rows // steps
        first = c.first_row // tile
        assert tile * steps == c.n_rows and first * tile == c.first_row and tile % BF16_SUBLANES == 0
        cast_in.append(pl.BlockSpec((tile, cols), lambda i, j, first=first: (first + i * grid[1] + j, 0)))
        cast_out.append(pl.BlockSpec((tile, cols), lambda i, j: (i * grid[1] + j, 0)))
        cast_shape.append(jax.ShapeDtypeStruct((c.n_rows, cols), BF16))

    def kernel(*refs):
        ins, srcs = refs[:n_in], refs[n_in:n_in + n_cast]
        out, dsts = refs[n_in + n_cast], refs[n_in + n_cast + 1:n_in + 2 * n_cast + 1]
        for src, dst in zip(srcs, dsts):
            dst[...] = src[...].astype(dst.dtype)
        body(*ins, out, *refs[n_in + 2 * n_cast + 1:])

    res = pl.pallas_call(
        kernel,
        grid=grid,
        in_specs=list(in_specs) + cast_in,
        out_specs=[out_specs] + cast_out,
        out_shape=[out_shape] + cast_shape,
        scratch_shapes=scratch_shapes,
        compiler_params=_params(("arbitrary", "arbitrary")),
        name=name,
    )(*args, *[c.src for c in casts])
    return res[0], list(res[1:])


def _bucket_tables():
    a = np.arange(BLK)[:, None]
    b = np.arange(2 * BLK)[None, :]
    dist = BLK + a - b
    max_exact = NUM_BUCKETS // 2
    out = []
    for dil, max_dist in PATTERNS:
        n = np.maximum(dist * dil, 0)
        large = max_exact + (np.log(np.maximum(n, 1) / max_exact)
                             / np.log(REL_MAX_DISTANCE / max_exact)
                             * (NUM_BUCKETS - max_exact)).astype(np.int32)
        large = np.minimum(large, NUM_BUCKETS - 1)
        bucket = np.where(n < max_exact, n, large).astype(np.int32)
        valid = (dist >= 0) & (dist <= max_dist)
        out.append(np.where(valid, bucket, -1).astype(np.int32))
    return np.stack(out)


def _bias_kernel(tbl_ref, bucket_ref, out_ref):
    col0 = jnp.where(pl.program_id(0) < N_B_PATTERNS, A_HEADS, 0)
    bk = bucket_ref[...]
    prev_cols = lax.broadcasted_iota(jnp.int32, bk.shape, 1) < BLK

    def head(h, carry):
        acc = jnp.full(bk.shape, NEG, F32)
        for b in range(NUM_BUCKETS):
            acc = jnp.where(bk == b, tbl_ref[b, col0 + h] * LOG2E, acc)
        out_ref[h, 0] = acc
        out_ref[h, 1] = jnp.where(prev_cols, NEG, acc)
        return carry
    lax.fori_loop(0, out_ref.shape[0], head, 0)


def _bias_tables(rel_bias, casts=()):
    buckets = jnp.asarray(_bucket_tables())
    heads = max(A_HEADS, B_HEADS)
    return _call_with_casts(
        _bias_kernel,
        grid=(len(PATTERNS), 1),
        in_specs=[pl.BlockSpec(memory_space=pltpu.SMEM),
                  pl.BlockSpec((None, BLK, 2 * BLK), lambda t, _: (t, 0, 0))],
        out_specs=pl.BlockSpec((None, heads, 2, BLK, 2 * BLK), lambda t, _: (t, 0, 0, 0, 0)),
        out_shape=jax.ShapeDtypeStruct((len(PATTERNS), heads, 2, BLK, 2 * BLK), F32),
        scratch_shapes=[],
        name="bias_tables",
        args=(rel_bias.astype(F32), buckets),
        casts=casts)


def _norm_rows(x_ref, g_ref, h_scr, tm):
    def body(c, carry):
        r = pl.ds(c * NORM_ROWS, NORM_ROWS)
        x = x_ref[r, :]
        ms = jnp.mean(x * x, axis=-1, keepdims=True)
        h_scr[r, :] = (x * lax.rsqrt(ms + EPS) * g_ref[...]).astype(BF16)
        return carry
    lax.fori_loop(0, tm // NORM_ROWS, body, 0, unroll=2)


def _even_in_kernel(x_ref, g_ref, w_ref, o_ref, h_scr, *, tm):
    @pl.when(pl.program_id(1) == 0)
    def _():
        _norm_rows(x_ref, g_ref, h_scr, tm)
    o_ref[...] = jnp.dot(h_scr[...], w_ref[...], preferred_element_type=F32).astype(o_ref.dtype)


def _even_in(x2, g, w, layer, *, tm=1024, tn=1280):
    m, d = x2.shape
    n = w.shape[2]
    return pl.pallas_call(
        functools.partial(_even_in_kernel, tm=tm),
        grid=(m // tm, n // tn),
        in_specs=[pl.BlockSpec((tm, d), lambda i, j: (i, 0)),
                  pl.BlockSpec((1, d), lambda i, j: (0, 0)),
                  pl.BlockSpec((None, d, tn), lambda i, j: (layer, 0, j))],
        out_specs=pl.BlockSpec((tm, tn), lambda i, j: (i, j)),
        out_shape=jax.ShapeDtypeStruct((m, n), BF16),
        scratch_shapes=[pltpu.VMEM((tm, d), BF16)],
        compiler_params=_params(("arbitrary", "arbitrary")),
        name="even_in",
    )(x2, g, w)


def _gelu(x):
    return 0.5 * x * (1.0 + lax.erf(x * (1.0 / math.sqrt(2.0))))


def _silu(x):
    return (0.5 * x) * (1.0 + jnp.tanh(0.5 * x))


def _odd_in_kernel(x_ref, g_ref, wv_ref, wu_ref, wg_ref, vg_ref, ws_ref, bs_ref, y_ref,
                   h_scr, v_scr, ssq_scr, *, tm, tn, tiles):
    j = pl.program_id(1)
    nbs = C_WIDTH // (tn * tiles)

    @pl.when(j == 0)
    def _():
        _norm_rows(x_ref, g_ref, h_scr, tm)
        ssq_scr[...] = jnp.zeros_like(ssq_scr)

    @pl.when(j < nbs)
    def _():
        for k in range(tiles):
            v = _gelu(jnp.dot(h_scr[...], wv_ref[:, pl.ds(k * tn, tn)], preferred_element_type=F32))
            v_scr[j * tiles + k] = v.astype(v_scr.dtype)
            ssq_scr[...] += jnp.sum(v * v, axis=-1, keepdims=True)

    @pl.when(j >= nbs)
    def _():
        tril = (lax.broadcasted_iota(jnp.int32, (C_CHUNK, C_CHUNK), 0)
                >= lax.broadcasted_iota(jnp.int32, (C_CHUNK, C_CHUNK), 1))
        groups_per_tile = tn // C_GROUP_DIM
        h = h_scr[...]
        for k in range(tiles):
            jj = (j - nbs) * tiles + k
            ug = (_gelu(jnp.dot(h, wu_ref[:, pl.ds(k * tn, tn)], preferred_element_type=F32))
                  * _silu(jnp.dot(h, wg_ref[:, pl.ds(k * tn, tn)], preferred_element_type=F32)))
            vg = vg_ref[jj]
            for gg in range(groups_per_tile):
                grp = jj * groups_per_tile + gg
                wt = jnp.where(tril, ws_ref[grp], 0.0).astype(BF16)
                cols = slice(gg * C_GROUP_DIM, (gg + 1) * C_GROUP_DIM)
                out_cols = pl.ds(k * tn + gg * C_GROUP_DIM, C_GROUP_DIM)
                for c in range(tm // C_CHUNK):
                    rows = slice(c * C_CHUNK, (c + 1) * C_CHUNK)
                    inv = lax.rsqrt(ssq_scr[rows, :][:, :1] * (1.0 / C_WIDTH) + EPS)
                    vn = (v_scr[jj, rows, cols].astype(F32) * inv * vg[:, cols]).astype(BF16)
                    s = jnp.dot(wt, vn, preferred_element_type=F32) + bs_ref[grp]
                    y_ref[rows, out_cols] = (ug[rows, cols] * s).astype(y_ref.dtype)


def _odd_in(x2, g, w, layer, vg, ws, bs, casts=(), *, tm=512, tn=512, tiles=2):
    m, d = x2.shape
    nb = C_WIDTH // tn
    wide = tn * tiles
    nbs = C_WIDTH // wide
    return _call_with_casts(
        functools.partial(_odd_in_kernel, tm=tm, tn=tn, tiles=tiles),
        grid=(m // tm, 2 * nbs),
        in_specs=[pl.BlockSpec((tm, d), lambda i, j: (i, 0)),
                  pl.BlockSpec((1, d), lambda i, j: (0, 0)),
                  pl.BlockSpec((None, d, wide), lambda i, j: (layer, 0, nbs + jnp.minimum(j, nbs - 1))),
                  pl.BlockSpec((None, d, wide), lambda i, j: (layer, 0, jnp.maximum(j - nbs, 0))),
                  pl.BlockSpec((None, d, wide), lambda i, j: (layer, 0, 2 * nbs + jnp.maximum(j - nbs, 0))),
                  pl.BlockSpec((nb, 1, tn), lambda i, j: (0, 0, 0)),
                  pl.BlockSpec((C_GROUPS, C_CHUNK, C_CHUNK), lambda i, j: (0, 0, 0)),
                  pl.BlockSpec((C_GROUPS, C_CHUNK, 1), lambda i, j: (0, 0, 0))],
        out_specs=pl.BlockSpec((tm, wide), lambda i, j: (i, jnp.maximum(j - nbs, 0))),
        out_shape=jax.ShapeDtypeStruct((m, C_WIDTH), BF16),
        scratch_shapes=[pltpu.VMEM((tm, d), BF16),
                        pltpu.VMEM((nb, tm, tn), BF16),
                        pltpu.VMEM((tm, LANES), F32)],
        name="odd_in",
        args=(x2, g, w, w, w, vg.reshape(nb, 1, tn), ws, bs.reshape(C_GROUPS, C_CHUNK, 1)),
        casts=casts)


PAD = BLK
UNROLL = 4
NSLOT = 3


def _and(b, mask):
    return b & mask if isinstance(b, int) else lax.bitwise_and(b, mask)


def _shr(b, s):
    return b >> s if isinstance(b, int) else lax.shift_right_logical(b, s)


def _lane_lo(rows):
    return lax.broadcasted_iota(jnp.int32, (rows, LANES), 1) < HEAD_DIM


def _head_rms(x, gain):
    w = x.shape[1]
    r = lax.broadcasted_iota(jnp.int32, (w, w), 0) // HEAD_DIM
    c = lax.broadcasted_iota(jnp.int32, (w, w), 1) // HEAD_DIM
    ones_bd = (r == c).astype(BF16)
    x2 = x * x
    hi = x2.astype(BF16)
    lo = (x2 - hi.astype(F32)).astype(BF16)
    ssq = (jnp.dot(hi, ones_bd, preferred_element_type=F32)
           + jnp.dot(lo, ones_bd, preferred_element_type=F32))
    return x * lax.rsqrt(ssq * (1.0 / HEAD_DIM) + EPS) * gain


def _attn_a_kernel(q0_ref, q1_ref, q2_ref, q3_ref, k_ref, v_ref, g0_ref, g1_ref, g2_ref, g3_ref,
                   gq_ref, gk_ref, snk_ref, bias_ref, o_ref, qb, kb, v0b, v1b, s_scr, mb_scr, p_scr, m_scr, *, seq):
    q_refs = (q0_ref, q1_ref, q2_ref, q3_ref)
    g_refs = (g0_ref, g1_ref, g2_ref, g3_ref)
    npair = len(q_refs)
    nblk = seq // BLK
    lane = lax.broadcasted_iota(jnp.int32, (BLK, LANES), 1)
    keep = (lane // HEAD_DIM) == pl.program_id(1)
    lo = lane < HEAD_DIM
    hi = jnp.logical_not(lo)
    zeros = jnp.zeros((PAD, LANES), BF16)
    kb[pl.ds(0, PAD), :] = zeros
    v0b[pl.ds(0, PAD), :] = zeros
    v1b[pl.ds(0, PAD), :] = zeros
    gq = gq_ref[...] * (SCALE * LOG2E)
    gq = jnp.concatenate([gq, gq], axis=1)

    def prep(c, carry):
        r = pl.ds(c * BLK, BLK)
        ro = pl.ds(PAD + c * BLK, BLK)
        for pp in range(0, npair, 2):
            qq = jnp.concatenate([q_refs[pp][r, :], q_refs[pp + 1][r, :]], axis=1).astype(F32)
            qn = _head_rms(qq, gq).astype(BF16)
            qb[pp, r, :] = qn[:, :LANES]
            qb[pp + 1, r, :] = qn[:, LANES:]
        kn = _head_rms(k_ref[r, :].astype(F32), gk_ref[...])
        kb[ro, :] = jnp.where(keep, kn, pltpu.roll(kn, HEAD_DIM, 1)).astype(BF16)
        v = v_ref[r, :].astype(F32)
        vd = jnp.where(keep, v, pltpu.roll(v, HEAD_DIM, 1))
        v0b[ro, :] = jnp.where(lo, vd, 1.0).astype(BF16)
        v1b[ro, :] = jnp.where(lo, 1.0, vd).astype(BF16)
        return carry
    lax.fori_loop(0, nblk, prep, 0, unroll=4)

    def scores(n):
        kk = kb[pl.ds(n * BLK, 2 * BLK), :]
        for pp in range(npair):
            q = qb[pp, pl.ds(n * BLK, BLK), :]
            for hh, sel in enumerate((lo, hi)):
                s_scr[n % NSLOT, pp, hh] = lax.dot_general(
                    jnp.where(sel, q, jnp.zeros_like(q)), kk, (((1,), (1,)), ((), ())),
                    preferred_element_type=F32) + bias_ref[2 * pp + hh, int(n == 0)]

    def row_max(n):
        slot = n % NSLOT
        for pp in range(npair):
            for hh in range(2):
                m = jnp.max(s_scr[slot, pp, hh], axis=-1, keepdims=True)
                mb_scr[slot, pp, hh] = jnp.broadcast_to(m, (BLK, LANES))
            m_scr[slot, pp] = jnp.where(lo, mb_scr[slot, pp, 0], mb_scr[slot, pp, 1])

    def exponentials(n):
        slot = n % NSLOT
        for pp in range(npair):
            for hh in range(2):
                mb = mb_scr[slot, pp, hh]
                p_scr[slot, pp, hh] = jnp.exp2(s_scr[slot, pp, hh] - jnp.concatenate([mb, mb], axis=1)).astype(BF16)

    def weighted_values(n):
        slot = n % NSLOT
        r = pl.ds(n * BLK, BLK)
        rk = pl.ds(n * BLK, 2 * BLK)
        for pp in range(npair):
            ul0 = jnp.dot(p_scr[slot, pp, 0], v0b[rk, :], preferred_element_type=F32)
            ul1 = jnp.dot(p_scr[slot, pp, 1], v1b[rk, :], preferred_element_type=F32)
            u = jnp.where(lo, ul0, ul1)
            l = pltpu.roll(jnp.where(lo, ul1, ul0), HEAD_DIM, 1)
            m = m_scr[slot, pp]
            snk = snk_ref[pp:pp + 1, :] * LOG2E
            mx = jnp.maximum(m, snk)
            w = jnp.exp2(m - mx)
            o = u * (w / (l * w + jnp.exp2(snk - mx)))
            o_ref[r, pl.ds(pp * LANES, LANES)] = (o * _silu(g_refs[pp][r, :].astype(F32))).astype(o_ref.dtype)

    for n in range(nblk + 3):
        if n - 3 >= 0:
            weighted_values(n - 3)
        if n < nblk:
            scores(n)
        if 0 <= n - 1 < nblk:
            row_max(n - 1)
        if 0 <= n - 2 < nblk:
            exponentials(n - 2)


def _attn_a(z3, gq2, gk2, snk3, tabs, casts=()):
    bsz, seq, _ = z3.shape
    npair = A_HEADS // A_KV_HEADS // 2

    def col(off, pp):
        return pl.BlockSpec((None, seq, LANES), lambda b, gi: (b, 0, off + npair * gi + pp))

    def fixed(off):
        return pl.BlockSpec((None, seq, LANES), lambda b, gi: (b, 0, off))

    return _call_with_casts(
        functools.partial(_attn_a_kernel, seq=seq),
        grid=(bsz, A_KV_HEADS),
        in_specs=[col(_QA, 0), col(_QA, 1), col(_QA, 2), col(_QA, 3), fixed(_KA), fixed(_VA),
                  col(_GA, 0), col(_GA, 1), col(_GA, 2), col(_GA, 3),
                  pl.BlockSpec((1, LANES), lambda b, gi: (0, 0)),
                  pl.BlockSpec((1, LANES), lambda b, gi: (0, 0)),
                  pl.BlockSpec((None, npair, LANES), lambda b, gi: (gi, 0, 0)),
                  pl.BlockSpec((None, 2 * npair, 2, BLK, 2 * BLK), lambda b, gi: (N_B_PATTERNS, gi, 0, 0, 0))],
        out_specs=pl.BlockSpec((None, seq, npair * LANES), lambda b, gi: (b, 0, gi)),
        out_shape=jax.ShapeDtypeStruct((bsz, seq, A_WIDTH), BF16),
        scratch_shapes=[pltpu.VMEM((npair, seq, LANES), BF16),
                        pltpu.VMEM((PAD + seq, LANES), BF16),
                        pltpu.VMEM((PAD + seq, LANES), BF16),
                        pltpu.VMEM((PAD + seq, LANES), BF16),
                        pltpu.VMEM((NSLOT, npair, 2, BLK, 2 * BLK), F32),
                        pltpu.VMEM((NSLOT, npair, 2, BLK, LANES), F32),
                        pltpu.VMEM((NSLOT, npair, 2, BLK, 2 * BLK), BF16),
                        pltpu.VMEM((NSLOT, npair, BLK, LANES), F32)],
        name="attn_a",
        args=(z3, z3, z3, z3, z3, z3, z3, z3, z3, z3, gq2, gk2, snk3, tabs),
        casts=casts)


def _attn_b_kernel(q_ref, k_ref, v_ref, g_ref, gq_ref, gk_ref, bias_ref, o_ref,
                   qf, kf, vf, q4f, k4f, v4f, qb, kb, v0b, v1b, s_scr, mb_scr, p_scr, m_scr, l_scr, u_scr, *, seq):
    lo = _lane_lo(BLK)
    hi = jnp.logical_not(lo)
    nblk = seq // BLK
    zeros = jnp.zeros((PAD, LANES), BF16)
    for pi in range(2):
        kb[pi, pl.ds(0, PAD), :] = zeros
        v0b[pi, pl.ds(0, PAD), :] = zeros
        v1b[pi, pl.ds(0, PAD), :] = zeros

    def put(pi, dst, q, k, v):
        qb[pi, dst, :] = q.astype(BF16)
        kb[pi, dst, :] = k.astype(BF16)
        v0b[pi, dst, :] = jnp.where(lo, v, 1.0).astype(BF16)
        v1b[pi, dst, :] = jnp.where(lo, 1.0, v).astype(BF16)

    def strided(t):
        return pl.ds(_and(t, 3) * (4 * BLK) + _shr(t, 2), BLK, stride=4)

    gains = jnp.concatenate([gq_ref[...] * (SCALE * LOG2E), gk_ref[...]], axis=1)

    def prep(c, carry):
        r = pl.ds(c * BLK, BLK)
        qk = jnp.concatenate([q_ref[r, :], k_ref[r, :]], axis=1).astype(F32)
        n = _head_rms(qk, gains)
        q, k, v = n[:, :LANES], n[:, LANES:], v_ref[r, :].astype(F32)
        qf[r, :] = q
        kf[r, :] = k
        vf[r, :] = v
        put(0, pl.ds(PAD + c * BLK, BLK), q, k, v)
        return carry
    for c in range(nblk):
        prep(c, 0)

    def deint4(t, carry):
        src = strided(t)
        dst = pl.ds(t * BLK, BLK)
        q, k, v = qf[src, :], kf[src, :], vf[src, :]
        q4f[dst, :] = q
        k4f[dst, :] = k
        v4f[dst, :] = v
        put(1, pl.ds(PAD + t * BLK, BLK), q, k, v)
        return carry
    for t in range(nblk):
        deint4(t, 0)

    def deint16(t, carry):
        src = strided(t)
        put(2, pl.ds(PAD + t * BLK, BLK), q4f[src, :], k4f[src, :], v4f[src, :])
        return carry
    for t in range(nblk):
        deint16(t, 0)

    groups = [(pi, g) for pi in range(N_B_PATTERNS) for g in range(nblk // UNROLL)]

    def geometry(pi):
        single = pi == 2
        return single, (BLK if single else 2 * BLK), (PAD if single else PAD - BLK)

    def out_rows(pi, b):
        return pl.ds(b * BLK, BLK) if pi == 0 else strided(b)

    def scores(t):
        pi, g = groups[t]
        single, keys, koff = geometry(pi)
        for uu in range(UNROLL):
            b = g * UNROLL + uu
            first = int(b % (nblk if pi == 0 else 4) == 0)
            q = qb[pi, pl.ds(PAD + b * BLK, BLK), :]
            kk = kb[pi, pl.ds(koff + b * BLK, keys), :]
            for hh, sel in enumerate((lo, hi)):
                bias = bias_ref[pi, hh, 0, :, pl.ds(BLK, BLK)] if single else bias_ref[pi, hh, first]
                s_scr[(t % NSLOT) * UNROLL + uu, hh, :, pl.ds(0, keys)] = lax.dot_general(
                    jnp.where(sel, q, jnp.zeros_like(q)), kk, (((1,), (1,)), ((), ())),
                    preferred_element_type=F32) + bias

    def row_max(t):
        pi, g = groups[t]
        _, keys, _ = geometry(pi)
        for uu in range(UNROLL):
            b = g * UNROLL + uu
            slot = (t % NSLOT) * UNROLL + uu
            for hh in range(2):
                m = jnp.max(s_scr[slot, hh, :, pl.ds(0, keys)], axis=-1, keepdims=True)
                mb_scr[slot, hh] = jnp.broadcast_to(m, (BLK, LANES))
            m_scr[pi, out_rows(pi, b), :] = jnp.where(lo, mb_scr[slot, 0], mb_scr[slot, 1])

    def exponentials(t):
        pi, g = groups[t]
        _, keys, _ = geometry(pi)
        for uu in range(UNROLL):
            slot = (t % NSLOT) * UNROLL + uu
            for hh in range(2):
                mb = mb_scr[slot, hh]
                mb = mb if keys == BLK else jnp.concatenate([mb, mb], axis=1)
                p_scr[slot, hh, :, pl.ds(0, keys)] = jnp.exp2(s_scr[slot, hh, :, pl.ds(0, keys)] - mb).astype(BF16)

    def weighted_values(t):
        pi, g = groups[t]
        _, keys, koff = geometry(pi)
        for uu in range(UNROLL):
            b = g * UNROLL + uu
            slot = (t % NSLOT) * UNROLL + uu
            rk = pl.ds(koff + b * BLK, keys)
            ul0 = jnp.dot(p_scr[slot, 0, :, pl.ds(0, keys)], v0b[pi, rk, :], preferred_element_type=F32)
            ul1 = jnp.dot(p_scr[slot, 1, :, pl.ds(0, keys)], v1b[pi, rk, :], preferred_element_type=F32)
            u_scr[pi, out_rows(pi, b), :] = jnp.where(lo, ul0, ul1)
            l_scr[pi, out_rows(pi, b), :] = pltpu.roll(jnp.where(lo, ul1, ul0), HEAD_DIM, 1)

    for t in range(len(groups) + 3):
        if t - 3 >= 0:
            weighted_values(t - 3)
        if t < len(groups):
            scores(t)
        if 0 <= t - 1 < len(groups):
            row_max(t - 1)
        if 0 <= t - 2 < len(groups):
            exponentials(t - 2)

    def renat(t, carry):
        for ref in (m_scr, l_scr, u_scr):
            ref[3, strided(t), :] = ref[2, pl.ds(t * BLK, BLK), :]
        return carry
    lax.fori_loop(0, nblk, renat, 0)

    def combine(c, carry):
        r = pl.ds(c * BLK, BLK)
        slots = (0, 1, 3)
        ms = [m_scr[s, r, :] for s in slots]
        mx = jnp.maximum(jnp.maximum(ms[0], ms[1]), ms[2])
        num = jnp.zeros((BLK, LANES), F32)
        den = jnp.zeros((BLK, LANES), F32)
        for s, m in zip(slots, ms):
            w = jnp.exp2(m - mx)
            num = num + w * u_scr[s, r, :]
            den = den + w * l_scr[s, r, :]
        o_ref[r, :] = ((num / den) * _silu(g_ref[r, :].astype(F32))).astype(o_ref.dtype)
        return carry
    lax.fori_loop(0, nblk, combine, 0, unroll=2)


def _attn_b(z3, gq2, gk2, tabs, casts=()):
    bsz, seq, _ = z3.shape
    col = lambda off: pl.BlockSpec((None, seq, LANES), lambda b, p: (b, 0, off + p))
    f32_rows = pltpu.VMEM((seq, LANES), F32)
    bf16_ops = pltpu.VMEM((N_B_PATTERNS, PAD + seq, LANES), BF16)
    stats = pltpu.VMEM((N_B_PATTERNS + 1, seq, LANES), F32)
    return _call_with_casts(
        functools.partial(_attn_b_kernel, seq=seq),
        grid=(bsz, B_HEADS // 2),
        in_specs=[col(_QB), col(_KB), col(_VB), col(_GB),
                  pl.BlockSpec((1, LANES), lambda b, p: (0, 0)),
                  pl.BlockSpec((1, LANES), lambda b, p: (0, 0)),
                  pl.BlockSpec((N_B_PATTERNS, 2, 2, BLK, 2 * BLK), lambda b, p: (0, p, 0, 0, 0))],
        out_specs=pl.BlockSpec((None, seq, LANES), lambda b, p: (b, 0, p)),
        out_shape=jax.ShapeDtypeStruct((bsz, seq, B_WIDTH), BF16),
        scratch_shapes=[f32_rows] * 6 + [bf16_ops] * 4
                       + [pltpu.VMEM((NSLOT * UNROLL, 2, BLK, 2 * BLK), F32),
                          pltpu.VMEM((NSLOT * UNROLL, 2, BLK, LANES), F32),
                          pltpu.VMEM((NSLOT * UNROLL, 2, BLK, 2 * BLK), BF16)] + [stats] * 3,
        name="attn_b",
        args=(z3, z3, z3, z3, gq2, gk2, tabs),
        casts=casts)


def _even_out_kernel(ya_ref, yb_ref, wa_ref, wb_ref, x_ref, o_ref):
    acc = jnp.dot(ya_ref[...], wa_ref[...], preferred_element_type=F32)
    acc = acc + jnp.dot(yb_ref[...], wb_ref[...], preferred_element_type=F32)
    o_ref[...] = x_ref[...] + acc


def _resident(block, index_map, n_col_tiles):
    if n_col_tiles == 1:
        return pl.BlockSpec(block, index_map, pipeline_mode=pl.Buffered(1))
    return pl.BlockSpec(block, index_map)


def _even_out(ya, yb, w, layer, x2, *, tm=512, tn=2048):
    m, d = x2.shape
    ka, kb = ya.shape[1], yb.shape[1]
    assert ka == kb
    return pl.pallas_call(
        _even_out_kernel,
        grid=(m // tm, d // tn),
        in_specs=[pl.BlockSpec((tm, ka), lambda i, j: (i, 0)),
                  pl.BlockSpec((tm, kb), lambda i, j: (i, 0)),
                  _resident((None, ka, tn), lambda i, j: (layer, 0, j), d // tn),
                  _resident((None, kb, tn), lambda i, j: (layer, 1, j), d // tn),
                  pl.BlockSpec((tm, tn), lambda i, j: (i, j))],
        out_specs=pl.BlockSpec((tm, tn), lambda i, j: (i, j)),
        out_shape=jax.ShapeDtypeStruct((m, d), F32),
        compiler_params=_params(("arbitrary", "arbitrary")),
        name="even_out",
    )(ya, yb, w, w, x2)


def _odd_out_kernel(y_ref, w_ref, x_ref, o_ref):
    o_ref[...] = x_ref[...] + jnp.dot(y_ref[...], w_ref[...], preferred_element_type=F32)


def _odd_out(y, w, layer, x2, *, tm=512, tn=2048):
    m, d = x2.shape
    k = y.shape[1]
    return pl.pallas_call(
        _odd_out_kernel,
        grid=(m // tm, d // tn),
        in_specs=[pl.BlockSpec((tm, k), lambda i, j: (i, 0)),
                  _resident((None, k, tn), lambda i, j: (layer, 0, j), d // tn),
                  pl.BlockSpec((tm, tn), lambda i, j: (i, j))],
        out_specs=pl.BlockSpec((tm, tn), lambda i, j: (i, j)),
        out_shape=jax.ShapeDtypeStruct((m, d), F32),
        compiler_params=_params(("arbitrary", "arbitrary")),
        name="odd_out",
    )(y, w, x2)


def kernel(x, ev_ln_g, ev_w_in, ev_qk_g, ev_sinks, ev_w_out, od_ln_g, od_w_in, od_v_g, od_w_s,
           od_b_s, od_w_out, rel_bias):
    bsz, seq, d = x.shape
    n_even, n_odd = ev_ln_g.shape[0], od_ln_g.shape[0]
    flat = lambda w: w.reshape(-1, w.shape[-1])
    whole = lambda w: _Cast(flat(w), 0, w.shape[0] * w.shape[1])
    layer_of = lambda w, j: _Cast(flat(w), j * w.shape[1], w.shape[1])
    tabs, done = _bias_tables(rel_bias, [layer_of(ev_w_in, 0)])
    ev_in_b = {0: done[0][None]}
    od_in_b, ev_out_b, od_out_b = {}, None, None
    x2 = x.reshape(bsz * seq, d)
    for i in range(n_even + n_odd):
        j = i // 2
        if i % 2 == 0:
            z = _even_in(x2, ev_ln_g[j].reshape(1, d), ev_in_b[j], 0)
            z3 = z.reshape(bsz, seq, EVEN_IN)
            gains = jnp.tile(ev_qk_g[j].astype(F32), (1, 2))
            snk3 = jnp.repeat(ev_sinks[j].astype(F32), HEAD_DIM).reshape(A_KV_HEADS, -1, LANES)
            ya, done = _attn_a(z3, gains[0:1], gains[1:2], snk3, tabs, [whole(ev_w_out)] if j == 0 else [])
            if j == 0:
                ev_out_b = done[0].reshape(ev_w_out.shape)
            yb, done = _attn_b(z3, gains[2:3], gains[3:4], tabs, [layer_of(od_w_in, j)] if j < n_odd else [])
            if j < n_odd:
                od_in_b[j] = done[0][None]
            x2 = _even_out(ya.reshape(bsz * seq, A_WIDTH), yb.reshape(bsz * seq, B_WIDTH), ev_out_b, j, x2)
        else:
            casts = ([whole(od_w_out)] if j == 0 else []) + ([layer_of(ev_w_in, j + 1)] if j + 1 < n_even else [])
            y, done = _odd_in(x2, od_ln_g[j].reshape(1, d), od_in_b[j], 0, od_v_g[j].astype(F32),
                              od_w_s[j].astype(F32), od_b_s[j].astype(F32), casts)
            if j == 0:
                od_out_b = done.pop(0).reshape(od_w_out.shape)
            if j + 1 < n_even:
                ev_in_b[j + 1] = done.pop(0)[None]
            x2 = _odd_out(y, od_out_b, j, x2)
    return x2.reshape(bsz, seq, d)
```

```python
import functools
import math
from typing import NamedTuple

import numpy as np
import jax
import jax.numpy as jnp
from jax import lax
from jax.experimental import pallas as pl
from jax.experimental.pallas import tpu as pltpu

F32 = jnp.float32
BF16 = jnp.bfloat16

D_MODEL = 2048
HEAD_DIM = 64
A_HEADS = 16
A_KV_HEADS = 2
B_HEADS = 16
BLK = 128
NUM_BUCKETS = 32
REL_MAX_DISTANCE = 2048
A_WIDTH = A_HEADS * HEAD_DIM
B_WIDTH = B_HEADS * HEAD_DIM
EVEN_IN = 6400
C_WIDTH = 2 * D_MODEL
C_GROUPS = 16
C_GROUP_DIM = C_WIDTH // C_GROUPS
C_CHUNK = 128
EPS = 1e-6
NEG = -1e30
SCALE = HEAD_DIM ** -0.5
LOG2E = 1.4426950408889634
PATTERNS = ((1, 128), (4, 128), (16, 128), (1, 127))
N_B_PATTERNS = 3

LANES = 128
BF16_SUBLANES = 16
NORM_ROWS = 128
VMEM_LIMIT = 60 * 1024 * 1024

_QA, _KA, _VA, _GA = 0, 8, 9, 10
_QB, _KB, _VB, _GB = 18, 26, 34, 42


def _params(sem):
    return pltpu.CompilerParams(dimension_semantics=sem, vmem_limit_bytes=VMEM_LIMIT)


class _Cast(NamedTuple):
    src: jax.Array
    first_row: int
    n_rows: int


def _call_with_casts(body, *, grid, in_specs, out_specs, out_shape, scratch_shapes, name, args, casts=(),
                     cast_steps=None):
    n_in, n_cast = len(in_specs), len(casts)
    inner = grid[1] if cast_steps is None else cast_steps
    steps = grid[0] * inner
    tile_of = lambda i, j: i * inner + jnp.minimum(j, inner - 1)
    cast_in, cast_out, cast_shape = [], [], []
    for c in casts:
        cols = c.src.shape[1]
        tile = c.n_rows // steps
        first = c.first_row // tile
        assert tile * steps == c.n_rows and first * tile == c.first_row and tile % BF16_SUBLANES == 0
        cast_in.append(pl.BlockSpec((tile, cols), lambda i, j, first=first: (first + tile_of(i, j), 0)))
        cast_out.append(pl.BlockSpec((tile, cols), lambda i, j: (tile_of(i, j), 0)))
        cast_shape.append(jax.ShapeDtypeStruct((c.n_rows, cols), BF16))

    def kernel(*refs):
        ins, srcs = refs[:n_in], refs[n_in:n_in + n_cast]
        out, dsts = refs[n_in + n_cast], refs[n_in + n_cast + 1:n_in + 2 * n_cast + 1]
        for src, dst in zip(srcs, dsts):
            dst[...] = src[...].astype(dst.dtype)
        body(*ins, out, *refs[n_in + 2 * n_cast + 1:])

    res = pl.pallas_call(
        kernel,
        grid=grid,
        in_specs=list(in_specs) + cast_in,
        out_specs=[out_specs] + cast_out,
        out_shape=[out_shape] + cast_shape,
        scratch_shapes=scratch_shapes,
        compiler_params=_params(("arbitrary", "arbitrary")),
        name=name,
    )(*args, *[c.src for c in casts])
    return res[0], list(res[1:])


def _bucket_tables():
    a = np.arange(BLK)[:, None]
    b = np.arange(2 * BLK)[None, :]
    dist = BLK + a - b
    max_exact = NUM_BUCKETS // 2
    out = []
    for dil, max_dist in PATTERNS:
        n = np.maximum(dist * dil, 0)
        large = max_exact + (np.log(np.maximum(n, 1) / max_exact)
                             / np.log(REL_MAX_DISTANCE / max_exact)
                             * (NUM_BUCKETS - max_exact)).astype(np.int32)
        large = np.minimum(large, NUM_BUCKETS - 1)
        bucket = np.where(n < max_exact, n, large).astype(np.int32)
        valid = (dist >= 0) & (dist <= max_dist)
        out.append(np.where(valid, bucket, -1).astype(np.int32))
    return np.stack(out)


def _bias_kernel(tbl_ref, bucket_ref, out_ref):
    col0 = jnp.where(pl.program_id(0) < N_B_PATTERNS, A_HEADS, 0)
    bk = bucket_ref[...]
    prev_cols = lax.broadcasted_iota(jnp.int32, bk.shape, 1) < BLK

    def head(h, carry):
        acc = jnp.full(bk.shape, NEG, F32)
        for b in range(NUM_BUCKETS):
            acc = jnp.where(bk == b, tbl_ref[b, col0 + h] * LOG2E, acc)
        out_ref[h, 0] = acc
        out_ref[h, 1] = jnp.where(prev_cols, NEG, acc)
        return carry
    lax.fori_loop(0, out_ref.shape[0], head, 0)


def _bias_tables(rel_bias, casts=()):
    buckets = jnp.asarray(_bucket_tables())
    heads = max(A_HEADS, B_HEADS)
    return _call_with_casts(
        _bias_kernel,
        grid=(len(PATTERNS), 1),
        in_specs=[pl.BlockSpec(memory_space=pltpu.SMEM),
                  pl.BlockSpec((None, BLK, 2 * BLK), lambda t, _: (t, 0, 0))],
        out_specs=pl.BlockSpec((None, heads, 2, BLK, 2 * BLK), lambda t, _: (t, 0, 0, 0, 0)),
        out_shape=jax.ShapeDtypeStruct((len(PATTERNS), heads, 2, BLK, 2 * BLK), F32),
        scratch_shapes=[],
        name="bias_tables",
        args=(rel_bias.astype(F32), buckets),
        casts=casts)


def _norm_rows(x_ref, g_ref, h_scr, tm):
    def body(c, carry):
        r = pl.ds(c * NORM_ROWS, NORM_ROWS)
        x = x_ref[r, :]
        ms = jnp.mean(x * x, axis=-1, keepdims=True)
        h_scr[r, :] = (x * lax.rsqrt(ms + EPS) * g_ref[...]).astype(BF16)
        return carry
    lax.fori_loop(0, tm // NORM_ROWS, body, 0, unroll=2)


def _even_in_kernel(x_ref, g_ref, w_ref, o_ref, h_scr, *, tm):
    @pl.when(pl.program_id(1) == 0)
    def _():
        _norm_rows(x_ref, g_ref, h_scr, tm)
    o_ref[...] = jnp.dot(h_scr[...], w_ref[...], preferred_element_type=F32).astype(o_ref.dtype)


def _even_in(x2, g, w, layer, *, tm=1024, tn=1280):
    m, d = x2.shape
    n = w.shape[2]
    return pl.pallas_call(
        functools.partial(_even_in_kernel, tm=tm),
        grid=(m // tm, n // tn),
        in_specs=[pl.BlockSpec((tm, d), lambda i, j: (i, 0)),
                  pl.BlockSpec((1, d), lambda i, j: (0, 0)),
                  pl.BlockSpec((None, d, tn), lambda i, j: (layer, 0, j))],
        out_specs=pl.BlockSpec((tm, tn), lambda i, j: (i, j)),
        out_shape=jax.ShapeDtypeStruct((m, n), BF16),
        scratch_shapes=[pltpu.VMEM((tm, d), BF16)],
        compiler_params=_params(("arbitrary", "arbitrary")),
        name="even_in",
    )(x2, g, w)


def _gelu(x):
    return 0.5 * x * (1.0 + lax.erf(x * (1.0 / math.sqrt(2.0))))


def _silu(x):
    return (0.5 * x) * (1.0 + jnp.tanh(0.5 * x))


def _odd_in_kernel(x_ref, g_ref, wv_ref, wu_ref, wg_ref, vg_ref, ws_ref, bs_ref, y_ref,
                   h_scr, v_scr, ssq_scr, *, tm, tn, v_tiles):
    j = pl.program_id(1)
    nbv = C_WIDTH // (tn * v_tiles)

    @pl.when(j == 0)
    def _():
        _norm_rows(x_ref, g_ref, h_scr, tm)
        ssq_scr[...] = jnp.zeros_like(ssq_scr)

    @pl.when(j < nbv)
    def _():
        v = _gelu(jnp.dot(h_scr[...], wv_ref[...], preferred_element_type=F32))
        for k in range(v_tiles):
            v_scr[j * v_tiles + k] = v[:, k * tn:(k + 1) * tn].astype(v_scr.dtype)
        ssq_scr[...] += jnp.sum(v * v, axis=-1, keepdims=True)

    @pl.when(j >= nbv)
    def _():
        jj = j - nbv
        h = h_scr[...]
        ug = (_gelu(jnp.dot(h, wu_ref[...], preferred_element_type=F32))
              * _silu(jnp.dot(h, wg_ref[...], preferred_element_type=F32)))
        tril = (lax.broadcasted_iota(jnp.int32, (C_CHUNK, C_CHUNK), 0)
                >= lax.broadcasted_iota(jnp.int32, (C_CHUNK, C_CHUNK), 1))
        groups_per_tile = tn // C_GROUP_DIM
        vg = vg_ref[jj]
        for gg in range(groups_per_tile):
            grp = jj * groups_per_tile + gg
            wt = jnp.where(tril, ws_ref[grp], 0.0).astype(BF16)
            cols = slice(gg * C_GROUP_DIM, (gg + 1) * C_GROUP_DIM)
            for c in range(tm // C_CHUNK):
                rows = slice(c * C_CHUNK, (c + 1) * C_CHUNK)
                inv = lax.rsqrt(ssq_scr[rows, :][:, :1] * (1.0 / C_WIDTH) + EPS)
                vn = (v_scr[jj, rows, cols].astype(F32) * inv * vg[:, cols]).astype(BF16)
                s = jnp.dot(wt, vn, preferred_element_type=F32) + bs_ref[grp]
                y_ref[rows, cols] = (ug[rows, cols] * s).astype(y_ref.dtype)


def _odd_in(x2, g, w, layer, vg, ws, bs, casts=(), *, tm=1024, tn=512, v_tiles=2):
    m, d = x2.shape
    nb = C_WIDTH // tn
    nbv = nb // v_tiles
    return _call_with_casts(
        functools.partial(_odd_in_kernel, tm=tm, tn=tn, v_tiles=v_tiles),
        grid=(m // tm, nbv + nb),
        in_specs=[pl.BlockSpec((tm, d), lambda i, j: (i, 0)),
                  pl.BlockSpec((1, d), lambda i, j: (0, 0)),
                  pl.BlockSpec((None, d, tn * v_tiles), lambda i, j: (layer, 0, nbv + jnp.minimum(j, nbv - 1))),
                  pl.BlockSpec((None, d, tn), lambda i, j: (layer, 0, jnp.maximum(j - nbv, 0))),
                  pl.BlockSpec((None, d, tn), lambda i, j: (layer, 0, 2 * nb + jnp.maximum(j - nbv, 0))),
                  pl.BlockSpec((nb, 1, tn), lambda i, j: (0, 0, 0)),
                  pl.BlockSpec((C_GROUPS, C_CHUNK, C_CHUNK), lambda i, j: (0, 0, 0)),
                  pl.BlockSpec((C_GROUPS, C_CHUNK, 1), lambda i, j: (0, 0, 0))],
        out_specs=pl.BlockSpec((tm, tn), lambda i, j: (i, jnp.maximum(j - nbv, 0))),
        out_shape=jax.ShapeDtypeStruct((m, C_WIDTH), BF16),
        scratch_shapes=[pltpu.VMEM((tm, d), BF16),
                        pltpu.VMEM((nb, tm, tn), BF16),
                        pltpu.VMEM((tm, LANES), F32)],
        name="odd_in",
        args=(x2, g, w, w, w, vg.reshape(nb, 1, tn), ws, bs.reshape(C_GROUPS, C_CHUNK, 1)),
        casts=casts,
        cast_steps=nb)


PAD = BLK
UNROLL = 4
NSLOT = 3


def _and(b, mask):
    return b & mask if isinstance(b, int) else lax.bitwise_and(b, mask)


def _shr(b, s):
    return b >> s if isinstance(b, int) else lax.shift_right_logical(b, s)


def _lane_lo(rows):
    return lax.broadcasted_iota(jnp.int32, (rows, LANES), 1) < HEAD_DIM


def _head_rms(x, gain):
    w = x.shape[1]
    r = lax.broadcasted_iota(jnp.int32, (w, w), 0) // HEAD_DIM
    c = lax.broadcasted_iota(jnp.int32, (w, w), 1) // HEAD_DIM
    ones_bd = (r == c).astype(BF16)
    x2 = x * x
    hi = x2.astype(BF16)
    lo = (x2 - hi.astype(F32)).astype(BF16)
    ssq = (jnp.dot(hi, ones_bd, preferred_element_type=F32)
           + jnp.dot(lo, ones_bd, preferred_element_type=F32))
    return x * lax.rsqrt(ssq * (1.0 / HEAD_DIM) + EPS) * gain


def _attn_a_kernel(q0_ref, q1_ref, q2_ref, q3_ref, k_ref, v_ref, g0_ref, g1_ref, g2_ref, g3_ref,
                   gq_ref, gk_ref, snk_ref, bias_ref, o_ref, qb, kb, v0b, v1b, s_scr, mb_scr, p_scr, m_scr, *, seq):
    q_refs = (q0_ref, q1_ref, q2_ref, q3_ref)
    g_refs = (g0_ref, g1_ref, g2_ref, g3_ref)
    npair = len(q_refs)
    nblk = seq // BLK
    lane = lax.broadcasted_iota(jnp.int32, (BLK, LANES), 1)
    keep = (lane // HEAD_DIM) == pl.program_id(1)
    lo = lane < HEAD_DIM
    hi = jnp.logical_not(lo)
    zeros = jnp.zeros((PAD, LANES), BF16)
    kb[pl.ds(0, PAD), :] = zeros
    v0b[pl.ds(0, PAD), :] = zeros
    v1b[pl.ds(0, PAD), :] = zeros
    gq = gq_ref[...] * (SCALE * LOG2E)
    gq = jnp.concatenate([gq, gq], axis=1)

    def prep(c, carry):
        r = pl.ds(c * BLK, BLK)
        ro = pl.ds(PAD + c * BLK, BLK)
        for pp in range(0, npair, 2):
            qq = jnp.concatenate([q_refs[pp][r, :], q_refs[pp + 1][r, :]], axis=1).astype(F32)
            qn = _head_rms(qq, gq).astype(BF16)
            qb[pp, r, :] = qn[:, :LANES]
            qb[pp + 1, r, :] = qn[:, LANES:]
        kn = _head_rms(k_ref[r, :].astype(F32), gk_ref[...])
        kb[ro, :] = jnp.where(keep, kn, pltpu.roll(kn, HEAD_DIM, 1)).astype(BF16)
        v = v_ref[r, :].astype(F32)
        vd = jnp.where(keep, v, pltpu.roll(v, HEAD_DIM, 1))
        v0b[ro, :] = jnp.where(lo, vd, 1.0).astype(BF16)
        v1b[ro, :] = jnp.where(lo, 1.0, vd).astype(BF16)
        return carry
    lax.fori_loop(0, nblk, prep, 0, unroll=4)

    def scores(n):
        kk = kb[pl.ds(n * BLK, 2 * BLK), :]
        for pp in range(npair):
            q = qb[pp, pl.ds(n * BLK, BLK), :]
            for hh, sel in enumerate((lo, hi)):
                s_scr[n % NSLOT, pp, hh] = lax.dot_general(
                    jnp.where(sel, q, jnp.zeros_like(q)), kk, (((1,), (1,)), ((), ())),
                    preferred_element_type=F32) + bias_ref[2 * pp + hh, int(n == 0)]

    def row_max(n):
        slot = n % NSLOT
        for pp in range(npair):
            for hh in range(2):
                m = jnp.max(s_scr[slot, pp, hh], axis=-1, keepdims=True)
                mb_scr[slot, pp, hh] = jnp.broadcast_to(m, (BLK, LANES))
            m_scr[slot, pp] = jnp.where(lo, mb_scr[slot, pp, 0], mb_scr[slot, pp, 1])

    def exponentials(n):
        slot = n % NSLOT
        for pp in range(npair):
            for hh in range(2):
                mb = mb_scr[slot, pp, hh]
                p_scr[slot, pp, hh] = jnp.exp2(s_scr[slot, pp, hh] - jnp.concatenate([mb, mb], axis=1)).astype(BF16)

    def weighted_values(n):
        slot = n % NSLOT
        r = pl.ds(n * BLK, BLK)
        rk = pl.ds(n * BLK, 2 * BLK)
        for pp in range(npair):
            ul0 = jnp.dot(p_scr[slot, pp, 0], v0b[rk, :], preferred_element_type=F32)
            ul1 = jnp.dot(p_scr[slot, pp, 1], v1b[rk, :], preferred_element_type=F32)
            u = jnp.where(lo, ul0, ul1)
            l = pltpu.roll(jnp.where(lo, ul1, ul0), HEAD_DIM, 1)
            m = m_scr[slot, pp]
            snk = snk_ref[pp:pp + 1, :] * LOG2E
            mx = jnp.maximum(m, snk)
            w = jnp.exp2(m - mx)
            o = u * (w / (l * w + jnp.exp2(snk - mx)))
            o_ref[r, pl.ds(pp * LANES, LANES)] = (o * _silu(g_refs[pp][r, :].astype(F32))).astype(o_ref.dtype)

    for n in range(nblk + 3):
        if n - 3 >= 0:
            weighted_values(n - 3)
        if n < nblk:
            scores(n)
        if 0 <= n - 1 < nblk:
            row_max(n - 1)
        if 0 <= n - 2 < nblk:
            exponentials(n - 2)


def _attn_a(z3, gq2, gk2, snk3, tabs, casts=()):
    bsz, seq, _ = z3.shape
    npair = A_HEADS // A_KV_HEADS // 2

    def col(off, pp):
        return pl.BlockSpec((None, seq, LANES), lambda b, gi: (b, 0, off + npair * gi + pp))

    def fixed(off):
        return pl.BlockSpec((None, seq, LANES), lambda b, gi: (b, 0, off))

    return _call_with_casts(
        functools.partial(_attn_a_kernel, seq=seq),
        grid=(bsz, A_KV_HEADS),
        in_specs=[col(_QA, 0), col(_QA, 1), col(_QA, 2), col(_QA, 3), fixed(_KA), fixed(_VA),
                  col(_GA, 0), col(_GA, 1), col(_GA, 2), col(_GA, 3),
                  pl.BlockSpec((1, LANES), lambda b, gi: (0, 0)),
                  pl.BlockSpec((1, LANES), lambda b, gi: (0, 0)),
                  pl.BlockSpec((None, npair, LANES), lambda b, gi: (gi, 0, 0)),
                  pl.BlockSpec((None, 2 * npair, 2, BLK, 2 * BLK), lambda b, gi: (N_B_PATTERNS, gi, 0, 0, 0))],
        out_specs=pl.BlockSpec((None, seq, npair * LANES), lambda b, gi: (b, 0, gi)),
        out_shape=jax.ShapeDtypeStruct((bsz, seq, A_WIDTH), BF16),
        scratch_shapes=[pltpu.VMEM((npair, seq, LANES), BF16),
                        pltpu.VMEM((PAD + seq, LANES), BF16),
                        pltpu.VMEM((PAD + seq, LANES), BF16),
                        pltpu.VMEM((PAD + seq, LANES), BF16),
                        pltpu.VMEM((NSLOT, npair, 2, BLK, 2 * BLK), F32),
                        pltpu.VMEM((NSLOT, npair, 2, BLK, LANES), F32),
                        pltpu.VMEM((NSLOT, npair, 2, BLK, 2 * BLK), BF16),
                        pltpu.VMEM((NSLOT, npair, BLK, LANES), F32)],
        name="attn_a",
        args=(z3, z3, z3, z3, z3, z3, z3, z3, z3, z3, gq2, gk2, snk3, tabs),
        casts=casts)


def _attn_b_kernel(q_ref, k_ref, v_ref, g_ref, gq_ref, gk_ref, bias_ref, o_ref,
                   qf, kf, vf, q4f, k4f, v4f, qb, kb, v0b, v1b, s_scr, mb_scr, p_scr, m_scr, l_scr, u_scr, *, seq):
    lo = _lane_lo(BLK)
    hi = jnp.logical_not(lo)
    nblk = seq // BLK
    zeros = jnp.zeros((PAD, LANES), BF16)
    for pi in range(2):
        kb[pi, pl.ds(0, PAD), :] = zeros
        v0b[pi, pl.ds(0, PAD), :] = zeros
        v1b[pi, pl.ds(0, PAD), :] = zeros

    def put(pi, dst, q, k, v):
        qb[pi, dst, :] = q.astype(BF16)
        kb[pi, dst, :] = k.astype(BF16)
        v0b[pi, dst, :] = jnp.where(lo, v, 1.0).astype(BF16)
        v1b[pi, dst, :] = jnp.where(lo, 1.0, v).astype(BF16)

    def strided(t):
        return pl.ds(_and(t, 3) * (4 * BLK) + _shr(t, 2), BLK, stride=4)

    gains = jnp.concatenate([gq_ref[...] * (SCALE * LOG2E), gk_ref[...]], axis=1)

    def prep(c, carry):
        r = pl.ds(c * BLK, BLK)
        qk = jnp.concatenate([q_ref[r, :], k_ref[r, :]], axis=1).astype(F32)
        n = _head_rms(qk, gains)
        q, k, v = n[:, :LANES], n[:, LANES:], v_ref[r, :].astype(F32)
        qf[r, :] = q
        kf[r, :] = k
        vf[r, :] = v
        put(0, pl.ds(PAD + c * BLK, BLK), q, k, v)
        return carry
    for c in range(nblk):
        prep(c, 0)

    def deint4(t, carry):
        src = strided(t)
        dst = pl.ds(t * BLK, BLK)
        q, k, v = qf[src, :], kf[src, :], vf[src, :]
        q4f[dst, :] = q
        k4f[dst, :] = k
        v4f[dst, :] = v
        put(1, pl.ds(PAD + t * BLK, BLK), q, k, v)
        return carry
    for t in range(nblk):
        deint4(t, 0)

    def deint16(t, carry):
        src = strided(t)
        put(2, pl.ds(PAD + t * BLK, BLK), q4f[src, :], k4f[src, :], v4f[src, :])
        return carry
    for t in range(nblk):
        deint16(t, 0)

    groups = [(pi, g) for pi in range(N_B_PATTERNS) for g in range(nblk // UNROLL)]

    def geometry(pi):
        single = pi == 2
        return single, (BLK if single else 2 * BLK), (PAD if single else PAD - BLK)

    def out_rows(pi, b):
        return pl.ds(b * BLK, BLK) if pi == 0 else strided(b)

    def scores(t):
        pi, g = groups[t]
        single, keys, koff = geometry(pi)
        for uu in range(UNROLL):
            b = g * UNROLL + uu
            first = int(b % (nblk if pi == 0 else 4) == 0)
            q = qb[pi, pl.ds(PAD + b * BLK, BLK), :]
            kk = kb[pi, pl.ds(koff + b * BLK, keys), :]
            for hh, sel in enumerate((lo, hi)):
                bias = bias_ref[pi, hh, 0, :, pl.ds(BLK, BLK)] if single else bias_ref[pi, hh, first]
                s_scr[(t % NSLOT) * UNROLL + uu, hh, :, pl.ds(0, keys)] = lax.dot_general(
                    jnp.where(sel, q, jnp.zeros_like(q)), kk, (((1,), (1,)), ((), ())),
                    preferred_element_type=F32) + bias

    def row_max(t):
        pi, g = groups[t]
        _, keys, _ = geometry(pi)
        for uu in range(UNROLL):
            b = g * UNROLL + uu
            slot = (t % NSLOT) * UNROLL + uu
            for hh in range(2):
                m = jnp.max(s_scr[slot, hh, :, pl.ds(0, keys)], axis=-1, keepdims=True)
                mb_scr[slot, hh] = jnp.broadcast_to(m, (BLK, LANES))
            m_scr[pi, out_rows(pi, b), :] = jnp.where(lo, mb_scr[slot, 0], mb_scr[slot, 1])

    def exponentials(t):
        pi, g = groups[t]
        _, keys, _ = geometry(pi)
        for uu in range(UNROLL):
            slot = (t % NSLOT) * UNROLL + uu
            for hh in range(2):
                mb = mb_scr[slot, hh]
                mb = mb if keys == BLK else jnp.concatenate([mb, mb], axis=1)
                p_scr[slot, hh, :, pl.ds(0, keys)] = jnp.exp2(s_scr[slot, hh, :, pl.ds(0, keys)] - mb).astype(BF16)

    def weighted_values(t):
        pi, g = groups[t]
        _, keys, koff = geometry(pi)
        for uu in range(UNROLL):
            b = g * UNROLL + uu
            slot = (t % NSLOT) * UNROLL + uu
            rk = pl.ds(koff + b * BLK, keys)
            ul0 = jnp.dot(p_scr[slot, 0, :, pl.ds(0, keys)], v0b[pi, rk, :], preferred_element_type=F32)
            ul1 = jnp.dot(p_scr[slot, 1, :, pl.ds(0, keys)], v1b[pi, rk, :], preferred_element_type=F32)
            u_scr[pi, out_rows(pi, b), :] = jnp.where(lo, ul0, ul1)
            l_scr[pi, out_rows(pi, b), :] = pltpu.roll(jnp.where(lo, ul1, ul0), HEAD_DIM, 1)

    for t in range(len(groups) + 3):
        if t - 3 >= 0:
            weighted_values(t - 3)
        if t < len(groups):
            scores(t)
        if 0 <= t - 1 < len(groups):
            row_max(t - 1)
        if 0 <= t - 2 < len(groups):
            exponentials(t - 2)

    def renat(t, carry):
        for ref in (m_scr, l_scr, u_scr):
            ref[3, strided(t), :] = ref[2, pl.ds(t * BLK, BLK), :]
        return carry
    lax.fori_loop(0, nblk, renat, 0)

    def combine(c, carry):
        r = pl.ds(c * BLK, BLK)
        slots = (0, 1, 3)
        ms = [m_scr[s, r, :] for s in slots]
        mx = jnp.maximum(jnp.maximum(ms[0], ms[1]), ms[2])
        num = jnp.zeros((BLK, LANES), F32)
        den = jnp.zeros((BLK, LANES), F32)
        for s, m in zip(slots, ms):
            w = jnp.exp2(m - mx)
            num = num + w * u_scr[s, r, :]
            den = den + w * l_scr[s, r, :]
        o_ref[r, :] = ((num / den) * _silu(g_ref[r, :].astype(F32))).astype(o_ref.dtype)
        return carry
    lax.fori_loop(0, nblk, combine, 0, unroll=2)


def _attn_b(z3, gq2, gk2, tabs, casts=()):
    bsz, seq, _ = z3.shape
    col = lambda off: pl.BlockSpec((None, seq, LANES), lambda b, p: (b, 0, off + p))
    f32_rows = pltpu.VMEM((seq, LANES), F32)
    bf16_ops = pltpu.VMEM((N_B_PATTERNS, PAD + seq, LANES), BF16)
    stats = pltpu.VMEM((N_B_PATTERNS + 1, seq, LANES), F32)
    return _call_with_casts(
        functools.partial(_attn_b_kernel, seq=seq),
        grid=(bsz, B_HEADS // 2),
        in_specs=[col(_QB), col(_KB), col(_VB), col(_GB),
                  pl.BlockSpec((1, LANES), lambda b, p: (0, 0)),
                  pl.BlockSpec((1, LANES), lambda b, p: (0, 0)),
                  pl.BlockSpec((N_B_PATTERNS, 2, 2, BLK, 2 * BLK), lambda b, p: (0, p, 0, 0, 0))],
        out_specs=pl.BlockSpec((None, seq, LANES), lambda b, p: (b, 0, p)),
        out_shape=jax.ShapeDtypeStruct((bsz, seq, B_WIDTH), BF16),
        scratch_shapes=[f32_rows] * 6 + [bf16_ops] * 4
                       + [pltpu.VMEM((NSLOT * UNROLL, 2, BLK, 2 * BLK), F32),
                          pltpu.VMEM((NSLOT * UNROLL, 2, BLK, LANES), F32),
                          pltpu.VMEM((NSLOT * UNROLL, 2, BLK, 2 * BLK), BF16)] + [stats] * 3,
        name="attn_b",
        args=(z3, z3, z3, z3, gq2, gk2, tabs),
        casts=casts)


def _even_out_kernel(ya_ref, yb_ref, wa_ref, wb_ref, x_ref, o_ref):
    acc = jnp.dot(ya_ref[...], wa_ref[...], preferred_element_type=F32)
    acc = acc + jnp.dot(yb_ref[...], wb_ref[...], preferred_element_type=F32)
    o_ref[...] = x_ref[...] + acc


def _resident(block, index_map, n_col_tiles):
    if n_col_tiles == 1:
        return pl.BlockSpec(block, index_map, pipeline_mode=pl.Buffered(1))
    return pl.BlockSpec(block, index_map)


def _even_out(ya, yb, w, layer, x2, *, tm=512, tn=2048):
    m, d = x2.shape
    ka, kb = ya.shape[1], yb.shape[1]
    assert ka == kb
    return pl.pallas_call(
        _even_out_kernel,
        grid=(m // tm, d // tn),
        in_specs=[pl.BlockSpec((tm, ka), lambda i, j: (i, 0)),
                  pl.BlockSpec((tm, kb), lambda i, j: (i, 0)),
                  _resident((None, ka, tn), lambda i, j: (layer, 0, j), d // tn),
                  _resident((None, kb, tn), lambda i, j: (layer, 1, j), d // tn),
                  pl.BlockSpec((tm, tn), lambda i, j: (i, j))],
        out_specs=pl.BlockSpec((tm, tn), lambda i, j: (i, j)),
        out_shape=jax.ShapeDtypeStruct((m, d), F32),
        compiler_params=_params(("arbitrary", "arbitrary")),
        name="even_out",
    )(ya, yb, w, w, x2)


def _odd_out_kernel(y_ref, w_ref, x_ref, o_ref):
    o_ref[...] = x_ref[...] + jnp.dot(y_ref[...], w_ref[...], preferred_element_type=F32)


def _odd_out(y, w, layer, x2, *, tm=512, tn=2048):
    m, d = x2.shape
    k = y.shape[1]
    return pl.pallas_call(
        _odd_out_kernel,
        grid=(m // tm, d // tn),
        in_specs=[pl.BlockSpec((tm, k), lambda i, j: (i, 0)),
                  _resident((None, k, tn), lambda i, j: (layer, 0, j), d // tn),
                  pl.BlockSpec((tm, tn), lambda i, j: (i, j))],
        out_specs=pl.BlockSpec((tm, tn), lambda i, j: (i, j)),
        out_shape=jax.ShapeDtypeStruct((m, d), F32),
        compiler_params=_params(("arbitrary", "arbitrary")),
        name="odd_out",
    )(y, w, x2)


def kernel(x, ev_ln_g, ev_w_in, ev_qk_g, ev_sinks, ev_w_out, od_ln_g, od_w_in, od_v_g, od_w_s,
           od_b_s, od_w_out, rel_bias):
    bsz, seq, d = x.shape
    n_even, n_odd = ev_ln_g.shape[0], od_ln_g.shape[0]
    flat = lambda w: w.reshape(-1, w.shape[-1])
    whole = lambda w: _Cast(flat(w), 0, w.shape[0] * w.shape[1])
    layer_of = lambda w, j: _Cast(flat(w), j * w.shape[1], w.shape[1])
    tabs, done = _bias_tables(rel_bias, [layer_of(ev_w_in, 0)])
    ev_in_b = {0: done[0][None]}
    od_in_b, ev_out_b, od_out_b = {}, None, None
    x2 = x.reshape(bsz * seq, d)
    for i in range(n_even + n_odd):
        j = i // 2
        if i % 2 == 0:
            z = _even_in(x2, ev_ln_g[j].reshape(1, d), ev_in_b[j], 0)
            z3 = z.reshape(bsz, seq, EVEN_IN)
            gains = jnp.tile(ev_qk_g[j].astype(F32), (1, 2))
            snk3 = jnp.repeat(ev_sinks[j].astype(F32), HEAD_DIM).reshape(A_KV_HEADS, -1, LANES)
            ya, done = _attn_a(z3, gains[0:1], gains[1:2], snk3, tabs, [whole(ev_w_out)] if j == 0 else [])
            if j == 0:
                ev_out_b = done[0].reshape(ev_w_out.shape)
            yb, done = _attn_b(z3, gains[2:3], gains[3:4], tabs, [layer_of(od_w_in, j)] if j < n_odd else [])
            if j < n_odd:
                od_in_b[j] = done[0][None]
            x2 = _even_out(ya.reshape(bsz * seq, A_WIDTH), yb.reshape(bsz * seq, B_WIDTH), ev_out_b, j, x2)
        else:
            casts = ([whole(od_w_out)] if j == 0 else []) + ([layer_of(ev_w_in, j + 1)] if j + 1 < n_even else [])
            y, done = _odd_in(x2, od_ln_g[j].reshape(1, d), od_in_b[j], 0, od_v_g[j].astype(F32),
                              od_w_s[j].astype(F32), od_b_s[j].astype(F32), casts)
            if j == 0:
                od_out_b = done.pop(0).reshape(od_w_out.shape)
            if j + 1 < n_even:
                ev_in_b[j + 1] = done.pop(0)[None]
            x2 = _odd_out(y, od_out_b, j, x2)
    return x2.reshape(bsz, seq, d)
```

```python
import functools
import math
from typing import NamedTuple

import numpy as np
import jax
import jax.numpy as jnp
from jax import lax
from jax.experimental import pallas as pl
from jax.experimental.pallas import tpu as pltpu

F32 = jnp.float32
BF16 = jnp.bfloat16

D_MODEL = 2048
HEAD_DIM = 64
A_HEADS = 16
A_KV_HEADS = 2
B_HEADS = 16
BLK = 128
NUM_BUCKETS = 32
REL_MAX_DISTANCE = 2048
A_WIDTH = A_HEADS * HEAD_DIM
B_WIDTH = B_HEADS * HEAD_DIM
EVEN_IN = 6400
C_WIDTH = 2 * D_MODEL
C_GROUPS = 16
C_GROUP_DIM = C_WIDTH // C_GROUPS
C_CHUNK = 128
EPS = 1e-6
NEG = -1e30
SCALE = HEAD_DIM ** -0.5
LOG2E = 1.4426950408889634
PATTERNS = ((1, 128), (4, 128), (16, 128), (1, 127))
N_B_PATTERNS = 3

LANES = 128
BF16_SUBLANES = 16
NORM_ROWS = 128
VMEM_LIMIT = 60 * 1024 * 1024

_QA, _KA, _VA, _GA = 0, 8, 9, 10
_QB, _KB, _VB, _GB = 18, 26, 34, 42


def _params(sem):
    return pltpu.CompilerParams(dimension_semantics=sem, vmem_limit_bytes=VMEM_LIMIT)


class _Cast(NamedTuple):
    src: jax.Array
    first_row: int
    n_rows: int


def _call_with_casts(body, *, grid, in_specs, out_specs, out_shape, scratch_shapes, name, args, casts=(),
                     cast_steps=None):
    n_in, n_cast = len(in_specs), len(casts)
    inner = grid[1] if cast_steps is None else cast_steps
    steps = grid[0] * inner
    tile_of = lambda i, j: i * inner + jnp.minimum(j, inner - 1)
    cast_in, cast_out, cast_shape = [], [], []
    for c in casts:
        cols = c.src.shape[1]
        tile = c.n_rows // steps
        first = c.first_row // tile
        assert tile * steps == c.n_rows and first * tile == c.first_row and tile % BF16_SUBLANES == 0
        cast_in.append(pl.BlockSpec((tile, cols), lambda i, j, first=first: (first + tile_of(i, j), 0)))
        cast_out.append(pl.BlockSpec((tile, cols), lambda i, j: (tile_of(i, j), 0)))
        cast_shape.append(jax.ShapeDtypeStruct((c.n_rows, cols), BF16))

    def kernel(*refs):
        ins, srcs = refs[:n_in], refs[n_in:n_in + n_cast]
        out, dsts = refs[n_in + n_cast], refs[n_in + n_cast + 1:n_in + 2 * n_cast + 1]
        for src, dst in zip(srcs, dsts):
            dst[...] = src[...].astype(dst.dtype)
        body(*ins, out, *refs[n_in + 2 * n_cast + 1:])

    res = pl.pallas_call(
        kernel,
        grid=grid,
        in_specs=list(in_specs) + cast_in,
        out_specs=[out_specs] + cast_out,
        out_shape=[out_shape] + cast_shape,
        scratch_shapes=scratch_shapes,
        compiler_params=_params(("arbitrary", "arbitrary")),
        name=name,
    )(*args, *[c.src for c in casts])
    return res[0], list(res[1:])


def _bucket_tables():
    a = np.arange(BLK)[:, None]
    b = np.arange(2 * BLK)[None, :]
    dist = BLK + a - b
    max_exact = NUM_BUCKETS // 2
    out = []
    for dil, max_dist in PATTERNS:
        n = np.maximum(dist * dil, 0)
        large = max_exact + (np.log(np.maximum(n, 1) / max_exact)
                             / np.log(REL_MAX_DISTANCE / max_exact)
                             * (NUM_BUCKETS - max_exact)).astype(np.int32)
        large = np.minimum(large, NUM_BUCKETS - 1)
        bucket = np.where(n < max_exact, n, large).astype(np.int32)
        valid = (dist >= 0) & (dist <= max_dist)
        out.append(np.where(valid, bucket, -1).astype(np.int32))
    return np.stack(out)


def _bias_kernel(tbl_ref, bucket_ref, out_ref):
    col0 = jnp.where(pl.program_id(0) < N_B_PATTERNS, A_HEADS, 0)
    bk = bucket_ref[...]
    prev_cols = lax.broadcasted_iota(jnp.int32, bk.shape, 1) < BLK

    def head(h, carry):
        acc = jnp.full(bk.shape, NEG, F32)
        for b in range(NUM_BUCKETS):
            acc = jnp.where(bk == b, tbl_ref[b, col0 + h] * LOG2E, acc)
        out_ref[h, 0] = acc
        out_ref[h, 1] = jnp.where(prev_cols, NEG, acc)
        return carry
    lax.fori_loop(0, out_ref.shape[0], head, 0)


def _bias_tables(rel_bias, casts=()):
    buckets = jnp.asarray(_bucket_tables())
    heads = max(A_HEADS, B_HEADS)
    return _call_with_casts(
        _bias_kernel,
        grid=(len(PATTERNS), 1),
        in_specs=[pl.BlockSpec(memory_space=pltpu.SMEM),
                  pl.BlockSpec((None, BLK, 2 * BLK), lambda t, _: (t, 0, 0))],
        out_specs=pl.BlockSpec((None, heads, 2, BLK, 2 * BLK), lambda t, _: (t, 0, 0, 0, 0)),
        out_shape=jax.ShapeDtypeStruct((len(PATTERNS), heads, 2, BLK, 2 * BLK), F32),
        scratch_shapes=[],
        name="bias_tables",
        args=(rel_bias.astype(F32), buckets),
        casts=casts)


def _norm_rows(x_ref, g_ref, h_scr, tm):
    def body(c, carry):
        r = pl.ds(c * NORM_ROWS, NORM_ROWS)
        x = x_ref[r, :]
        ms = jnp.mean(x * x, axis=-1, keepdims=True)
        h_scr[r, :] = (x * lax.rsqrt(ms + EPS) * g_ref[...]).astype(BF16)
        return carry
    lax.fori_loop(0, tm // NORM_ROWS, body, 0, unroll=2)


def _even_in_kernel(x_ref, g_ref, w_ref, o_ref, h_scr, *, tm):
    @pl.when(pl.program_id(1) == 0)
    def _():
        _norm_rows(x_ref, g_ref, h_scr, tm)
    o_ref[...] = jnp.dot(h_scr[...], w_ref[...], preferred_element_type=F32).astype(o_ref.dtype)


def _even_in(x2, g, w, layer, *, tm=1024, tn=1280):
    m, d = x2.shape
    n = w.shape[2]
    return pl.pallas_call(
        functools.partial(_even_in_kernel, tm=tm),
        grid=(m // tm, n // tn),
        in_specs=[pl.BlockSpec((tm, d), lambda i, j: (i, 0)),
                  pl.BlockSpec((1, d), lambda i, j: (0, 0)),
                  pl.BlockSpec((None, d, tn), lambda i, j: (layer, 0, j))],
        out_specs=pl.BlockSpec((tm, tn), lambda i, j: (i, j)),
        out_shape=jax.ShapeDtypeStruct((m, n), BF16),
        scratch_shapes=[pltpu.VMEM((tm, d), BF16)],
        compiler_params=_params(("arbitrary", "arbitrary")),
        name="even_in",
    )(x2, g, w)


def _gelu(x):
    return 0.5 * x * (1.0 + lax.erf(x * (1.0 / math.sqrt(2.0))))


def _silu(x):
    return (0.5 * x) * (1.0 + jnp.tanh(0.5 * x))


def _odd_in_kernel(x_ref, g_ref, wv_ref, wu_ref, wg_ref, vg_ref, ws_ref, bs_ref, y_ref,
                   h_scr, v_scr, ssq_scr, *, tm, tn, v_tiles):
    j = pl.program_id(1)
    nbv = C_WIDTH // (tn * v_tiles)

    @pl.when(j == 0)
    def _():
        _norm_rows(x_ref, g_ref, h_scr, tm)
        ssq_scr[...] = jnp.zeros_like(ssq_scr)

    @pl.when(j < nbv)
    def _():
        v = _gelu(jnp.dot(h_scr[...], wv_ref[...], preferred_element_type=F32))
        for k in range(v_tiles):
            v_scr[j * v_tiles + k] = v[:, k * tn:(k + 1) * tn].astype(v_scr.dtype)
        ssq_scr[...] += jnp.sum(v * v, axis=-1, keepdims=True)

    @pl.when(j >= nbv)
    def _():
        jj = j - nbv
        h = h_scr[...]
        ug = (_gelu(jnp.dot(h, wu_ref[...], preferred_element_type=F32))
              * _silu(jnp.dot(h, wg_ref[...], preferred_element_type=F32)))
        tril = (lax.broadcasted_iota(jnp.int32, (C_CHUNK, C_CHUNK), 0)
                >= lax.broadcasted_iota(jnp.int32, (C_CHUNK, C_CHUNK), 1))
        groups_per_tile = tn // C_GROUP_DIM
        vg = vg_ref[jj]
        for gg in range(groups_per_tile):
            grp = jj * groups_per_tile + gg
            wt = jnp.where(tril, ws_ref[grp], 0.0).astype(BF16)
            cols = slice(gg * C_GROUP_DIM, (gg + 1) * C_GROUP_DIM)
            for c in range(tm // C_CHUNK):
                rows = slice(c * C_CHUNK, (c + 1) * C_CHUNK)
                inv = lax.rsqrt(ssq_scr[rows, :][:, :1] * (1.0 / C_WIDTH) + EPS)
                vn = (v_scr[jj, rows, cols].astype(F32) * inv * vg[:, cols]).astype(BF16)
                s = jnp.dot(wt, vn, preferred_element_type=F32) + bs_ref[grp]
                y_ref[rows, cols] = (ug[rows, cols] * s).astype(y_ref.dtype)


def _odd_in(x2, g, w, layer, vg, ws, bs, casts=(), *, tm=1024, tn=512, v_tiles=2):
    m, d = x2.shape
    nb = C_WIDTH // tn
    nbv = nb // v_tiles
    return _call_with_casts(
        functools.partial(_odd_in_kernel, tm=tm, tn=tn, v_tiles=v_tiles),
        grid=(m // tm, nbv + nb),
        in_specs=[pl.BlockSpec((tm, d), lambda i, j: (i, 0)),
                  pl.BlockSpec((1, d), lambda i, j: (0, 0)),
                  pl.BlockSpec((None, d, tn * v_tiles), lambda i, j: (layer, 0, nbv + jnp.minimum(j, nbv - 1))),
                  pl.BlockSpec((None, d, tn), lambda i, j: (layer, 0, jnp.maximum(j - nbv, 0))),
                  pl.BlockSpec((None, d, tn), lambda i, j: (layer, 0, 2 * nb + jnp.maximum(j - nbv, 0))),
                  pl.BlockSpec((nb, 1, tn), lambda i, j: (0, 0, 0)),
                  pl.BlockSpec((C_GROUPS, C_CHUNK, C_CHUNK), lambda i, j: (0, 0, 0)),
                  pl.BlockSpec((C_GROUPS, C_CHUNK, 1), lambda i, j: (0, 0, 0))],
        out_specs=pl.BlockSpec((tm, tn), lambda i, j: (i, jnp.maximum(j - nbv, 0))),
        out_shape=jax.ShapeDtypeStruct((m, C_WIDTH), BF16),
        scratch_shapes=[pltpu.VMEM((tm, d), BF16),
                        pltpu.VMEM((nb, tm, tn), BF16),
                        pltpu.VMEM((tm, LANES), F32)],
        name="odd_in",
        args=(x2, g, w, w, w, vg.reshape(nb, 1, tn), ws, bs.reshape(C_GROUPS, C_CHUNK, 1)),
        casts=casts,
        cast_steps=nb)


PAD = BLK
UNROLL = 4
NSLOT = 3


def _and(b, mask):
    return b & mask if isinstance(b, int) else lax.bitwise_and(b, mask)


def _shr(b, s):
    return b >> s if isinstance(b, int) else lax.shift_right_logical(b, s)


def _lane_lo(rows):
    return lax.broadcasted_iota(jnp.int32, (rows, LANES), 1) < HEAD_DIM


def _head_rms(x, gain):
    w = x.shape[1]
    r = lax.broadcasted_iota(jnp.int32, (w, w), 0) // HEAD_DIM
    c = lax.broadcasted_iota(jnp.int32, (w, w), 1) // HEAD_DIM
    ones_bd = (r == c).astype(BF16)
    x2 = x * x
    hi = x2.astype(BF16)
    lo = (x2 - hi.astype(F32)).astype(BF16)
    ssq = (jnp.dot(hi, ones_bd, preferred_element_type=F32)
           + jnp.dot(lo, ones_bd, preferred_element_type=F32))
    return x * lax.rsqrt(ssq * (1.0 / HEAD_DIM) + EPS) * gain


def _attn_a_kernel(q0_ref, q1_ref, q2_ref, q3_ref, k_ref, v_ref, g0_ref, g1_ref, g2_ref, g3_ref,
                   gq_ref, gk_ref, snk_ref, bias_ref, o_ref, qb, kb, v0b, v1b, s_scr, mb_scr, p_scr, m_scr, *, seq):
    q_refs = (q0_ref, q1_ref, q2_ref, q3_ref)
    g_refs = (g0_ref, g1_ref, g2_ref, g3_ref)
    npair = len(q_refs)
    nblk = seq // BLK
    lane = lax.broadcasted_iota(jnp.int32, (BLK, LANES), 1)
    keep = (lane // HEAD_DIM) == pl.program_id(1)
    lo = lane < HEAD_DIM
    hi = jnp.logical_not(lo)
    zeros = jnp.zeros((PAD, LANES), BF16)
    kb[pl.ds(0, PAD), :] = zeros
    v0b[pl.ds(0, PAD), :] = zeros
    v1b[pl.ds(0, PAD), :] = zeros
    gq = gq_ref[...] * (SCALE * LOG2E)
    gq = jnp.concatenate([gq, gq], axis=1)

    def prep(c, carry):
        r = pl.ds(c * BLK, BLK)
        ro = pl.ds(PAD + c * BLK, BLK)
        for pp in range(0, npair, 2):
            qq = jnp.concatenate([q_refs[pp][r, :], q_refs[pp + 1][r, :]], axis=1).astype(F32)
            qn = _head_rms(qq, gq).astype(BF16)
            qb[pp, r, :] = qn[:, :LANES]
            qb[pp + 1, r, :] = qn[:, LANES:]
        kn = _head_rms(k_ref[r, :].astype(F32), gk_ref[...])
        kb[ro, :] = jnp.where(keep, kn, pltpu.roll(kn, HEAD_DIM, 1)).astype(BF16)
        v = v_ref[r, :].astype(F32)
        vd = jnp.where(keep, v, pltpu.roll(v, HEAD_DIM, 1))
        v0b[ro, :] = jnp.where(lo, vd, 1.0).astype(BF16)
        v1b[ro, :] = jnp.where(lo, 1.0, vd).astype(BF16)
        return carry
    lax.fori_loop(0, nblk, prep, 0, unroll=4)

    def scores(n):
        kk = kb[pl.ds(n * BLK, 2 * BLK), :]
        for pp in range(npair):
            q = qb[pp, pl.ds(n * BLK, BLK), :]
            for hh, sel in enumerate((lo, hi)):
                s_scr[n % NSLOT, pp, hh] = lax.dot_general(
                    jnp.where(sel, q, jnp.zeros_like(q)), kk, (((1,), (1,)), ((), ())),
                    preferred_element_type=F32) + bias_ref[2 * pp + hh, int(n == 0)]

    def row_max(n):
        slot = n % NSLOT
        for pp in range(npair):
            for hh in range(2):
                m = jnp.max(s_scr[slot, pp, hh], axis=-1, keepdims=True)
                mb_scr[slot, pp, hh] = jnp.broadcast_to(m, (BLK, LANES))
            m_scr[slot, pp] = jnp.where(lo, mb_scr[slot, pp, 0], mb_scr[slot, pp, 1])

    def exponentials(n):
        slot = n % NSLOT
        for pp in range(npair):
            for hh in range(2):
                mb = mb_scr[slot, pp, hh]
                p_scr[slot, pp, hh] = jnp.exp2(s_scr[slot, pp, hh] - jnp.concatenate([mb, mb], axis=1)).astype(BF16)

    def weighted_values(n):
        slot = n % NSLOT
        r = pl.ds(n * BLK, BLK)
        rk = pl.ds(n * BLK, 2 * BLK)
        for pp in range(npair):
            ul0 = jnp.dot(p_scr[slot, pp, 0], v0b[rk, :], preferred_element_type=F32)
            ul1 = jnp.dot(p_scr[slot, pp, 1], v1b[rk, :], preferred_element_type=F32)
            u = jnp.where(lo, ul0, ul1)
            l = pltpu.roll(jnp.where(lo, ul1, ul0), HEAD_DIM, 1)
            o = u / (l + jnp.exp2(snk_ref[pp:pp + 1, :] * LOG2E - m_scr[slot, pp]))
            o_ref[r, pl.ds(pp * LANES, LANES)] = (o * _silu(g_refs[pp][r, :].astype(F32))).astype(o_ref.dtype)

    for n in range(nblk + 3):
        if n - 3 >= 0:
            weighted_values(n - 3)
        if n < nblk:
            scores(n)
        if 0 <= n - 1 < nblk:
            row_max(n - 1)
        if 0 <= n - 2 < nblk:
            exponentials(n - 2)


def _attn_a(z3, gq2, gk2, snk3, tabs, casts=()):
    bsz, seq, _ = z3.shape
    npair = A_HEADS // A_KV_HEADS // 2

    def col(off, pp):
        return pl.BlockSpec((None, seq, LANES), lambda b, gi: (b, 0, off + npair * gi + pp))

    def fixed(off):
        return pl.BlockSpec((None, seq, LANES), lambda b, gi: (b, 0, off))

    return _call_with_casts(
        functools.partial(_attn_a_kernel, seq=seq),
        grid=(bsz, A_KV_HEADS),
        in_specs=[col(_QA, 0), col(_QA, 1), col(_QA, 2), col(_QA, 3), fixed(_KA), fixed(_VA),
                  col(_GA, 0), col(_GA, 1), col(_GA, 2), col(_GA, 3),
                  pl.BlockSpec((1, LANES), lambda b, gi: (0, 0)),
                  pl.BlockSpec((1, LANES), lambda b, gi: (0, 0)),
                  pl.BlockSpec((None, npair, LANES), lambda b, gi: (gi, 0, 0)),
                  pl.BlockSpec((None, 2 * npair, 2, BLK, 2 * BLK), lambda b, gi: (N_B_PATTERNS, gi, 0, 0, 0))],
        out_specs=pl.BlockSpec((None, seq, npair * LANES), lambda b, gi: (b, 0, gi)),
        out_shape=jax.ShapeDtypeStruct((bsz, seq, A_WIDTH), BF16),
        scratch_shapes=[pltpu.VMEM((npair, seq, LANES), BF16),
                        pltpu.VMEM((PAD + seq, LANES), BF16),
                        pltpu.VMEM((PAD + seq, LANES), BF16),
                        pltpu.VMEM((PAD + seq, LANES), BF16),
                        pltpu.VMEM((NSLOT, npair, 2, BLK, 2 * BLK), F32),
                        pltpu.VMEM((NSLOT, npair, 2, BLK, LANES), F32),
                        pltpu.VMEM((NSLOT, npair, 2, BLK, 2 * BLK), BF16),
                        pltpu.VMEM((NSLOT, npair, BLK, LANES), F32)],
        name="attn_a",
        args=(z3, z3, z3, z3, z3, z3, z3, z3, z3, z3, gq2, gk2, snk3, tabs),
        casts=casts)


def _attn_b_kernel(q_ref, k_ref, v_ref, g_ref, gq_ref, gk_ref, bias_ref, o_ref,
                   qf, kf, vf, q4f, k4f, v4f, qb, kb, v0b, v1b, s_scr, mb_scr, p_scr, m_scr, l_scr, u_scr, *, seq):
    lo = _lane_lo(BLK)
    hi = jnp.logical_not(lo)
    nblk = seq // BLK
    zeros = jnp.zeros((PAD, LANES), BF16)
    for pi in range(2):
        kb[pi, pl.ds(0, PAD), :] = zeros
        v0b[pi, pl.ds(0, PAD), :] = zeros
        v1b[pi, pl.ds(0, PAD), :] = zeros

    def put(pi, dst, q, k, v):
        qb[pi, dst, :] = q.astype(BF16)
        kb[pi, dst, :] = k.astype(BF16)
        v0b[pi, dst, :] = jnp.where(lo, v, 1.0).astype(BF16)
        v1b[pi, dst, :] = jnp.where(lo, 1.0, v).astype(BF16)

    def strided(t):
        return pl.ds(_and(t, 3) * (4 * BLK) + _shr(t, 2), BLK, stride=4)

    gains = jnp.concatenate([gq_ref[...] * (SCALE * LOG2E), gk_ref[...]], axis=1)

    def prep(c, carry):
        r = pl.ds(c * BLK, BLK)
        qk = jnp.concatenate([q_ref[r, :], k_ref[r, :]], axis=1).astype(F32)
        n = _head_rms(qk, gains)
        q, k, v = n[:, :LANES], n[:, LANES:], v_ref[r, :].astype(F32)
        qf[r, :] = q
        kf[r, :] = k
        vf[r, :] = v
        put(0, pl.ds(PAD + c * BLK, BLK), q, k, v)
        return carry
    for c in range(nblk):
        prep(c, 0)

    def deint4(t, carry):
        src = strided(t)
        dst = pl.ds(t * BLK, BLK)
        q, k, v = qf[src, :], kf[src, :], vf[src, :]
        q4f[dst, :] = q
        k4f[dst, :] = k
        v4f[dst, :] = v
        put(1, pl.ds(PAD + t * BLK, BLK), q, k, v)
        return carry
    for t in range(nblk):
        deint4(t, 0)

    def deint16(t, carry):
        src = strided(t)
        put(2, pl.ds(PAD + t * BLK, BLK), q4f[src, :], k4f[src, :], v4f[src, :])
        return carry
    for t in range(nblk):
        deint16(t, 0)

    groups = [(pi, g) for pi in range(N_B_PATTERNS) for g in range(nblk // UNROLL)]

    def geometry(pi):
        single = pi == 2
        return single, (BLK if single else 2 * BLK), (PAD if single else PAD - BLK)

    def out_rows(pi, b):
        return pl.ds(b * BLK, BLK) if pi == 0 else strided(b)

    def scores(t):
        pi, g = groups[t]
        single, keys, koff = geometry(pi)
        for uu in range(UNROLL):
            b = g * UNROLL + uu
            first = int(b % (nblk if pi == 0 else 4) == 0)
            q = qb[pi, pl.ds(PAD + b * BLK, BLK), :]
            kk = kb[pi, pl.ds(koff + b * BLK, keys), :]
            for hh, sel in enumerate((lo, hi)):
                bias = bias_ref[pi, hh, 0, :, pl.ds(BLK, BLK)] if single else bias_ref[pi, hh, first]
                s_scr[(t % NSLOT) * UNROLL + uu, hh, :, pl.ds(0, keys)] = lax.dot_general(
                    jnp.where(sel, q, jnp.zeros_like(q)), kk, (((1,), (1,)), ((), ())),
                    preferred_element_type=F32) + bias

    def row_max(t):
        pi, g = groups[t]
        _, keys, _ = geometry(pi)
        for uu in range(UNROLL):
            b = g * UNROLL + uu
            slot = (t % NSLOT) * UNROLL + uu
            for hh in range(2):
                m = jnp.max(s_scr[slot, hh, :, pl.ds(0, keys)], axis=-1, keepdims=True)
                mb_scr[slot, hh] = jnp.broadcast_to(m, (BLK, LANES))
            m_scr[pi, out_rows(pi, b), :] = jnp.where(lo, mb_scr[slot, 0], mb_scr[slot, 1])

    def exponentials(t):
        pi, g = groups[t]
        _, keys, _ = geometry(pi)
        for uu in range(UNROLL):
            slot = (t % NSLOT) * UNROLL + uu
            for hh in range(2):
                mb = mb_scr[slot, hh]
                mb = mb if keys == BLK else jnp.concatenate([mb, mb], axis=1)
                p_scr[slot, hh, :, pl.ds(0, keys)] = jnp.exp2(s_scr[slot, hh, :, pl.ds(0, keys)] - mb).astype(BF16)

    def weighted_values(t):
        pi, g = groups[t]
        _, keys, koff = geometry(pi)
        for uu in range(UNROLL):
            b = g * UNROLL + uu
            slot = (t % NSLOT) * UNROLL + uu
            rk = pl.ds(koff + b * BLK, keys)
            ul0 = jnp.dot(p_scr[slot, 0, :, pl.ds(0, keys)], v0b[pi, rk, :], preferred_element_type=F32)
            ul1 = jnp.dot(p_scr[slot, 1, :, pl.ds(0, keys)], v1b[pi, rk, :], preferred_element_type=F32)
            u_scr[pi, out_rows(pi, b), :] = jnp.where(lo, ul0, ul1)
            l_scr[pi, out_rows(pi, b), :] = pltpu.roll(jnp.where(lo, ul1, ul0), HEAD_DIM, 1)

    for t in range(len(groups) + 3):
        if t - 3 >= 0:
            weighted_values(t - 3)
        if t < len(groups):
            scores(t)
        if 0 <= t - 1 < len(groups):
            row_max(t - 1)
        if 0 <= t - 2 < len(groups):
            exponentials(t - 2)

    def renat(t, carry):
        for ref in (m_scr, l_scr, u_scr):
            ref[3, strided(t), :] = ref[2, pl.ds(t * BLK, BLK), :]
        return carry
    lax.fori_loop(0, nblk, renat, 0)

    def combine(c, carry):
        r = pl.ds(c * BLK, BLK)
        slots = (0, 1, 3)
        ms = [m_scr[s, r, :] for s in slots]
        mx = jnp.maximum(jnp.maximum(ms[0], ms[1]), ms[2])
        ws = [jnp.exp2(m - mx) for m in ms]
        add = lambda a, b: a + b
        num = functools.reduce(add, [w * u_scr[s, r, :] for s, w in zip(slots, ws)])
        den = functools.reduce(add, [w * l_scr[s, r, :] for s, w in zip(slots, ws)])
        o_ref[r, :] = ((num / den) * _silu(g_ref[r, :].astype(F32))).astype(o_ref.dtype)
        return carry
    lax.fori_loop(0, nblk, combine, 0, unroll=2)


def _attn_b(z3, gq2, gk2, tabs, casts=()):
    bsz, seq, _ = z3.shape
    col = lambda off: pl.BlockSpec((None, seq, LANES), lambda b, p: (b, 0, off + p))
    f32_rows = pltpu.VMEM((seq, LANES), F32)
    bf16_ops = pltpu.VMEM((N_B_PATTERNS, PAD + seq, LANES), BF16)
    stats = pltpu.VMEM((N_B_PATTERNS + 1, seq, LANES), F32)
    return _call_with_casts(
        functools.partial(_attn_b_kernel, seq=seq),
        grid=(bsz, B_HEADS // 2),
        in_specs=[col(_QB), col(_KB), col(_VB), col(_GB),
                  pl.BlockSpec((1, LANES), lambda b, p: (0, 0)),
                  pl.BlockSpec((1, LANES), lambda b, p: (0, 0)),
                  pl.BlockSpec((N_B_PATTERNS, 2, 2, BLK, 2 * BLK), lambda b, p: (0, p, 0, 0, 0))],
        out_specs=pl.BlockSpec((None, seq, LANES), lambda b, p: (b, 0, p)),
        out_shape=jax.ShapeDtypeStruct((bsz, seq, B_WIDTH), BF16),
        scratch_shapes=[f32_rows] * 6 + [bf16_ops] * 4
                       + [pltpu.VMEM((NSLOT * UNROLL, 2, BLK, 2 * BLK), F32),
                          pltpu.VMEM((NSLOT * UNROLL, 2, BLK, LANES), F32),
                          pltpu.VMEM((NSLOT * UNROLL, 2, BLK, 2 * BLK), BF16)] + [stats] * 3,
        name="attn_b",
        args=(z3, z3, z3, z3, gq2, gk2, tabs),
        casts=casts)


def _even_out_kernel(ya_ref, yb_ref, wa_ref, wb_ref, x_ref, o_ref):
    acc = jnp.dot(ya_ref[...], wa_ref[...], preferred_element_type=F32)
    acc = acc + jnp.dot(yb_ref[...], wb_ref[...], preferred_element_type=F32)
    o_ref[...] = x_ref[...] + acc


def _resident(block, index_map, n_col_tiles):
    if n_col_tiles == 1:
        return pl.BlockSpec(block, index_map, pipeline_mode=pl.Buffered(1))
    return pl.BlockSpec(block, index_map)


def _even_out(ya, yb, w, layer, x2, *, tm=512, tn=2048):
    m, d = x2.shape
    ka, kb = ya.shape[1], yb.shape[1]
    assert ka == kb
    return pl.pallas_call(
        _even_out_kernel,
        grid=(m // tm, d // tn),
        in_specs=[pl.BlockSpec((tm, ka), lambda i, j: (i, 0)),
                  pl.BlockSpec((tm, kb), lambda i, j: (i, 0)),
                  _resident((None, ka, tn), lambda i, j: (layer, 0, j), d // tn),
                  _resident((None, kb, tn), lambda i, j: (layer, 1, j), d // tn),
                  pl.BlockSpec((tm, tn), lambda i, j: (i, j))],
        out_specs=pl.BlockSpec((tm, tn), lambda i, j: (i, j)),
        out_shape=jax.ShapeDtypeStruct((m, d), F32),
        compiler_params=_params(("arbitrary", "arbitrary")),
        name="even_out",
    )(ya, yb, w, w, x2)


def _odd_out_kernel(y_ref, w_ref, x_ref, o_ref):
    o_ref[...] = x_ref[...] + jnp.dot(y_ref[...], w_ref[...], preferred_element_type=F32)


def _odd_out(y, w, layer, x2, *, tm=512, tn=2048):
    m, d = x2.shape
    k = y.shape[1]
    return pl.pallas_call(
        _odd_out_kernel,
        grid=(m // tm, d // tn),
        in_specs=[pl.BlockSpec((tm, k), lambda i, j: (i, 0)),
                  _resident((None, k, tn), lambda i, j: (layer, 0, j), d // tn),
                  pl.BlockSpec((tm, tn), lambda i, j: (i, j))],
        out_specs=pl.BlockSpec((tm, tn), lambda i, j: (i, j)),
        out_shape=jax.ShapeDtypeStruct((m, d), F32),
        compiler_params=_params(("arbitrary", "arbitrary")),
        name="odd_out",
    )(y, w, x2)


def kernel(x, ev_ln_g, ev_w_in, ev_qk_g, ev_sinks, ev_w_out, od_ln_g, od_w_in, od_v_g, od_w_s,
           od_b_s, od_w_out, rel_bias):
    bsz, seq, d = x.shape
    n_even, n_odd = ev_ln_g.shape[0], od_ln_g.shape[0]
    flat = lambda w: w.reshape(-1, w.shape[-1])
    whole = lambda w: _Cast(flat(w), 0, w.shape[0] * w.shape[1])
    layer_of = lambda w, j: _Cast(flat(w), j * w.shape[1], w.shape[1])
    tabs, done = _bias_tables(rel_bias, [layer_of(ev_w_in, 0)])
    ev_in_b = {0: done[0][None]}
    od_in_b, ev_out_b, od_out_b = {}, None, None
    x2 = x.reshape(bsz * seq, d)
    for i in range(n_even + n_odd):
        j = i // 2
        if i % 2 == 0:
            z = _even_in(x2, ev_ln_g[j].reshape(1, d), ev_in_b[j], 0)
            z3 = z.reshape(bsz, seq, EVEN_IN)
            gains = jnp.tile(ev_qk_g[j].astype(F32), (1, 2))
            snk3 = jnp.repeat(ev_sinks[j].astype(F32), HEAD_DIM).reshape(A_KV_HEADS, -1, LANES)
            ya, done = _attn_a(z3, gains[0:1], gains[1:2], snk3, tabs, [whole(ev_w_out)] if j == 0 else [])
            if j == 0:
                ev_out_b = done[0].reshape(ev_w_out.shape)
            yb, done = _attn_b(z3, gains[2:3], gains[3:4], tabs, [layer_of(od_w_in, j)] if j < n_odd else [])
            if j < n_odd:
                od_in_b[j] = done[0][None]
            x2 = _even_out(ya.reshape(bsz * seq, A_WIDTH), yb.reshape(bsz * seq, B_WIDTH), ev_out_b, j, x2)
        else:
            casts = ([whole(od_w_out)] if j == 0 else []) + ([layer_of(ev_w_in, j + 1)] if j + 1 < n_even else [])
            y, done = _odd_in(x2, od_ln_g[j].reshape(1, d), od_in_b[j], 0, od_v_g[j].astype(F32),
                              od_w_s[j].astype(F32), od_b_s[j].astype(F32), casts)
            if j == 0:
                od_out_b = done.pop(0).reshape(od_w_out.shape)
            if j + 1 < n_even:
                ev_in_b[j + 1] = done.pop(0)[None]
            x2 = _odd_out(y, od_out_b, j, x2)
    return x2.reshape(bsz, seq, d)
```

```python
import functools
import math
from typing import NamedTuple

import numpy as np
import jax
import jax.numpy as jnp
from jax import lax
from jax.experimental import pallas as pl
from jax.experimental.pallas import tpu as pltpu

F32 = jnp.float32
BF16 = jnp.bfloat16

D_MODEL = 2048
HEAD_DIM = 64
A_HEADS = 16
A_KV_HEADS = 2
B_HEADS = 16
BLK = 128
NUM_BUCKETS = 32
REL_MAX_DISTANCE = 2048
A_WIDTH = A_HEADS * HEAD_DIM
B_WIDTH = B_HEADS * HEAD_DIM
EVEN_IN = 6400
C_WIDTH = 2 * D_MODEL
C_GROUPS = 16
C_GROUP_DIM = C_WIDTH // C_GROUPS
C_CHUNK = 128
EPS = 1e-6
NEG = -1e30
SCALE = HEAD_DIM ** -0.5
LOG2E = 1.4426950408889634
PATTERNS = ((1, 128), (4, 128), (16, 128), (1, 127))
N_B_PATTERNS = 3

LANES = 128
BF16_SUBLANES = 16
NORM_ROWS = 128
VMEM_LIMIT = 60 * 1024 * 1024

_QA, _KA, _VA, _GA = 0, 8, 9, 10
_QB, _KB, _VB, _GB = 18, 26, 34, 42


def _params(sem):
    return pltpu.CompilerParams(dimension_semantics=sem, vmem_limit_bytes=VMEM_LIMIT)


class _Cast(NamedTuple):
    src: jax.Array
    first_row: int
    n_rows: int


def _call_with_casts(body, *, grid, in_specs, out_specs, out_shape, scratch_shapes, name, args, casts=(),
                     cast_steps=None):
    n_in, n_cast = len(in_specs), len(casts)
    inner = grid[1] if cast_steps is None else cast_steps
    steps = grid[0] * inner
    tile_of = lambda i, j: i * inner + jnp.minimum(j, inner - 1)
    cast_in, cast_out, cast_shape = [], [], []
    for c in casts:
        cols = c.src.shape[1]
        tile = c.n_rows // steps
        first = c.first_row // tile
        assert tile * steps == c.n_rows and first * tile == c.first_row and tile % BF16_SUBLANES == 0
        cast_in.append(pl.BlockSpec((tile, cols), lambda i, j, first=first: (first + tile_of(i, j), 0)))
        cast_out.append(pl.BlockSpec((tile, cols), lambda i, j: (tile_of(i, j), 0)))
        cast_shape.append(jax.ShapeDtypeStruct((c.n_rows, cols), BF16))

    def kernel(*refs):
        ins, srcs = refs[:n_in], refs[n_in:n_in + n_cast]
        out, dsts = refs[n_in + n_cast], refs[n_in + n_cast + 1:n_in + 2 * n_cast + 1]
        for src, dst in zip(srcs, dsts):
            dst[...] = src[...].astype(dst.dtype)
        body(*ins, out, *refs[n_in + 2 * n_cast + 1:])

    res = pl.pallas_call(
        kernel,
        grid=grid,
        in_specs=list(in_specs) + cast_in,
        out_specs=[out_specs] + cast_out,
        out_shape=[out_shape] + cast_shape,
        scratch_shapes=scratch_shapes,
        compiler_params=_params(("arbitrary", "arbitrary")),
        name=name,
    )(*args, *[c.src for c in casts])
    return res[0], list(res[1:])


def _bucket_tables():
    a = np.arange(BLK)[:, None]
    b = np.arange(2 * BLK)[None, :]
    dist = BLK + a - b
    max_exact = NUM_BUCKETS // 2
    out = []
    for dil, max_dist in PATTERNS:
        n = np.maximum(dist * dil, 0)
        large = max_exact + (np.log(np.maximum(n, 1) / max_exact)
                             / np.log(REL_MAX_DISTANCE / max_exact)
                             * (NUM_BUCKETS - max_exact)).astype(np.int32)
        large = np.minimum(large, NUM_BUCKETS - 1)
        bucket = np.where(n < max_exact, n, large).astype(np.int32)
        valid = (dist >= 0) & (dist <= max_dist)
        out.append(np.where(valid, bucket, -1).astype(np.int32))
    return np.stack(out)


def _bias_kernel(tbl_ref, bucket_ref, out_ref):
    col0 = jnp.where(pl.program_id(0) < N_B_PATTERNS, A_HEADS, 0)
    bk = bucket_ref[...]
    prev_cols = lax.broadcasted_iota(jnp.int32, bk.shape, 1) < BLK

    def head(h, carry):
        acc = jnp.full(bk.shape, NEG, F32)
        for b in range(NUM_BUCKETS):
            acc = jnp.where(bk == b, tbl_ref[b, col0 + h] * LOG2E, acc)
        out_ref[h, 0] = acc
        out_ref[h, 1] = jnp.where(prev_cols, NEG, acc)
        return carry
    lax.fori_loop(0, out_ref.shape[0], head, 0)


def _bias_tables(rel_bias, casts=()):
    buckets = jnp.asarray(_bucket_tables())
    heads = max(A_HEADS, B_HEADS)
    return _call_with_casts(
        _bias_kernel,
        grid=(len(PATTERNS), 1),
        in_specs=[pl.BlockSpec(memory_space=pltpu.SMEM),
                  pl.BlockSpec((None, BLK, 2 * BLK), lambda t, _: (t, 0, 0))],
        out_specs=pl.BlockSpec((None, heads, 2, BLK, 2 * BLK), lambda t, _: (t, 0, 0, 0, 0)),
        out_shape=jax.ShapeDtypeStruct((len(PATTERNS), heads, 2, BLK, 2 * BLK), F32),
        scratch_shapes=[],
        name="bias_tables",
        args=(rel_bias.astype(F32), buckets),
        casts=casts)


def _norm_rows(x_ref, g_ref, h_scr, tm):
    def body(c, carry):
        r = pl.ds(c * NORM_ROWS, NORM_ROWS)
        x = x_ref[r, :]
        ms = jnp.mean(x * x, axis=-1, keepdims=True)
        h_scr[r, :] = (x * lax.rsqrt(ms + EPS) * g_ref[...]).astype(BF16)
        return carry
    lax.fori_loop(0, tm // NORM_ROWS, body, 0, unroll=2)


def _even_in_kernel(x_ref, g_ref, w_ref, o_ref, h_scr, *, tm):
    @pl.when(pl.program_id(1) == 0)
    def _():
        _norm_rows(x_ref, g_ref, h_scr, tm)
    o_ref[...] = jnp.dot(h_scr[...], w_ref[...], preferred_element_type=F32).astype(o_ref.dtype)


def _even_in(x2, g, w, layer, *, tm=1024, tn=1280):
    m, d = x2.shape
    n = w.shape[2]
    return pl.pallas_call(
        functools.partial(_even_in_kernel, tm=tm),
        grid=(m // tm, n // tn),
        in_specs=[pl.BlockSpec((tm, d), lambda i, j: (i, 0)),
                  pl.BlockSpec((1, d), lambda i, j: (0, 0)),
                  pl.BlockSpec((None, d, tn), lambda i, j: (layer, 0, j))],
        out_specs=pl.BlockSpec((tm, tn), lambda i, j: (i, j)),
        out_shape=jax.ShapeDtypeStruct((m, n), BF16),
        scratch_shapes=[pltpu.VMEM((tm, d), BF16)],
        compiler_params=_params(("arbitrary", "arbitrary")),
        name="even_in",
    )(x2, g, w)


def _gelu(x):
    return 0.5 * x * (1.0 + lax.erf(x * (1.0 / math.sqrt(2.0))))


def _silu(x):
    return (0.5 * x) * (1.0 + jnp.tanh(0.5 * x))


def _odd_in_kernel(x_ref, g_ref, wv_ref, wu_ref, wg_ref, vg_ref, ws_ref, bs_ref, y_ref,
                   h_scr, v_scr, ssq_scr, *, tm, tn, v_tiles):
    j = pl.program_id(1)
    nbv = C_WIDTH // (tn * v_tiles)

    @pl.when(j == 0)
    def _():
        _norm_rows(x_ref, g_ref, h_scr, tm)
        ssq_scr[...] = jnp.zeros_like(ssq_scr)

    @pl.when(j < nbv)
    def _():
        v = _gelu(jnp.dot(h_scr[...], wv_ref[...], preferred_element_type=F32))
        for k in range(v_tiles):
            v_scr[j * v_tiles + k] = v[:, k * tn:(k + 1) * tn].astype(v_scr.dtype)
        ssq_scr[...] += jnp.sum(v * v, axis=-1, keepdims=True)

    @pl.when(j >= nbv)
    def _():
        jj = j - nbv
        h = h_scr[...]
        ug = (_gelu(jnp.dot(h, wu_ref[...], preferred_element_type=F32))
              * _silu(jnp.dot(h, wg_ref[...], preferred_element_type=F32)))
        tril = (lax.broadcasted_iota(jnp.int32, (C_CHUNK, C_CHUNK), 0)
                >= lax.broadcasted_iota(jnp.int32, (C_CHUNK, C_CHUNK), 1))
        groups_per_tile = tn // C_GROUP_DIM
        vg = vg_ref[jj]
        for gg in range(groups_per_tile):
            grp = jj * groups_per_tile + gg
            wt = jnp.where(tril, ws_ref[grp], 0.0).astype(BF16)
            cols = slice(gg * C_GROUP_DIM, (gg + 1) * C_GROUP_DIM)
            for c in range(tm // C_CHUNK):
                rows = slice(c * C_CHUNK, (c + 1) * C_CHUNK)
                inv = lax.rsqrt(ssq_scr[rows, :][:, :1] * (1.0 / C_WIDTH) + EPS)
                vn = (v_scr[jj, rows, cols].astype(F32) * inv * vg[:, cols]).astype(BF16)
                s = jnp.dot(wt, vn, preferred_element_type=F32) + bs_ref[grp]
                y_ref[rows, cols] = (ug[rows, cols] * s).astype(y_ref.dtype)


def _odd_in(x2, g, w, layer, vg, ws, bs, casts=(), *, tm=1024, tn=512, v_tiles=2):
    m, d = x2.shape
    nb = C_WIDTH // tn
    nbv = nb // v_tiles
    return _call_with_casts(
        functools.partial(_odd_in_kernel, tm=tm, tn=tn, v_tiles=v_tiles),
        grid=(m // tm, nbv + nb),
        in_specs=[pl.BlockSpec((tm, d), lambda i, j: (i, 0)),
                  pl.BlockSpec((1, d), lambda i, j: (0, 0)),
                  pl.BlockSpec((None, d, tn * v_tiles), lambda i, j: (layer, 0, nbv + jnp.minimum(j, nbv - 1))),
                  pl.BlockSpec((None, d, tn), lambda i, j: (layer, 0, jnp.maximum(j - nbv, 0))),
                  pl.BlockSpec((None, d, tn), lambda i, j: (layer, 0, 2 * nb + jnp.maximum(j - nbv, 0))),
                  pl.BlockSpec((nb, 1, tn), lambda i, j: (0, 0, 0)),
                  pl.BlockSpec((C_GROUPS, C_CHUNK, C_CHUNK), lambda i, j: (0, 0, 0)),
                  pl.BlockSpec((C_GROUPS, C_CHUNK, 1), lambda i, j: (0, 0, 0))],
        out_specs=pl.BlockSpec((tm, tn), lambda i, j: (i, jnp.maximum(j - nbv, 0))),
        out_shape=jax.ShapeDtypeStruct((m, C_WIDTH), BF16),
        scratch_shapes=[pltpu.VMEM((tm, d), BF16),
                        pltpu.VMEM((nb, tm, tn), BF16),
                        pltpu.VMEM((tm, LANES), F32)],
        name="odd_in",
        args=(x2, g, w, w, w, vg.reshape(nb, 1, tn), ws, bs.reshape(C_GROUPS, C_CHUNK, 1)),
        casts=casts,
        cast_steps=nb)


PAD = BLK
UNROLL = 4
NSLOT = 3


def _and(b, mask):
    return b & mask if isinstance(b, int) else lax.bitwise_and(b, mask)


def _shr(b, s):
    return b >> s if isinstance(b, int) else lax.shift_right_logical(b, s)


def _lane_lo(rows):
    return lax.broadcasted_iota(jnp.int32, (rows, LANES), 1) < HEAD_DIM


def _head_rms(x, gain):
    w = x.shape[1]
    r = lax.broadcasted_iota(jnp.int32, (w, w), 0) // HEAD_DIM
    c = lax.broadcasted_iota(jnp.int32, (w, w), 1) // HEAD_DIM
    mean_bd = jnp.where(r == c, 1.0 / HEAD_DIM, 0.0).astype(BF16)
    x2 = x * x
    hi = x2.astype(BF16)
    lo = (x2 - hi.astype(F32)).astype(BF16)
    ms = (jnp.dot(hi, mean_bd, preferred_element_type=F32)
          + jnp.dot(lo, mean_bd, preferred_element_type=F32))
    return x * lax.rsqrt(ms + EPS) * gain


def _attn_a_kernel(q0_ref, q1_ref, q2_ref, q3_ref, k_ref, v_ref, g0_ref, g1_ref, g2_ref, g3_ref,
                   gq_ref, gk_ref, snk_ref, bias_ref, o_ref, qb, kb, v0b, v1b, s_scr, mb_scr, p_scr, m_scr, *, seq):
    q_refs = (q0_ref, q1_ref, q2_ref, q3_ref)
    g_refs = (g0_ref, g1_ref, g2_ref, g3_ref)
    npair = len(q_refs)
    nblk = seq // BLK
    lane = lax.broadcasted_iota(jnp.int32, (BLK, LANES), 1)
    keep = (lane // HEAD_DIM) == pl.program_id(1)
    lo = lane < HEAD_DIM
    hi = jnp.logical_not(lo)
    zeros = jnp.zeros((PAD, LANES), BF16)
    kb[pl.ds(0, PAD), :] = zeros
    v0b[pl.ds(0, PAD), :] = zeros
    v1b[pl.ds(0, PAD), :] = zeros
    gq = gq_ref[...] * (SCALE * LOG2E)
    gq = jnp.concatenate([gq, gq], axis=1)
    snk = snk_ref[...] * LOG2E

    def prep(c, carry):
        r = pl.ds(c * BLK, BLK)
        ro = pl.ds(PAD + c * BLK, BLK)
        for pp in range(0, npair, 2):
            qq = jnp.concatenate([q_refs[pp][r, :], q_refs[pp + 1][r, :]], axis=1).astype(F32)
            qn = _head_rms(qq, gq).astype(BF16)
            qb[pp, r, :] = qn[:, :LANES]
            qb[pp + 1, r, :] = qn[:, LANES:]
        kn = _head_rms(k_ref[r, :].astype(F32), gk_ref[...])
        kb[ro, :] = jnp.where(keep, kn, pltpu.roll(kn, HEAD_DIM, 1)).astype(BF16)
        v = v_ref[r, :].astype(F32)
        vd = jnp.where(keep, v, pltpu.roll(v, HEAD_DIM, 1))
        v0b[ro, :] = jnp.where(lo, vd, 1.0).astype(BF16)
        v1b[ro, :] = jnp.where(lo, 1.0, vd).astype(BF16)
        return carry
    lax.fori_loop(0, nblk, prep, 0, unroll=4)

    def scores(n):
        kk = kb[pl.ds(n * BLK, 2 * BLK), :]
        for pp in range(npair):
            q = qb[pp, pl.ds(n * BLK, BLK), :]
            for hh, sel in enumerate((lo, hi)):
                s_scr[n % NSLOT, pp, hh] = lax.dot_general(
                    jnp.where(sel, q, jnp.zeros_like(q)), kk, (((1,), (1,)), ((), ())),
                    preferred_element_type=F32) + bias_ref[2 * pp + hh, int(n == 0)]

    def row_max(n):
        slot = n % NSLOT
        for pp in range(npair):
            for hh in range(2):
                m = jnp.max(s_scr[slot, pp, hh], axis=-1, keepdims=True)
                mb_scr[slot, pp, hh] = jnp.broadcast_to(m, (BLK, LANES))
            m_scr[slot, pp] = jnp.where(lo, mb_scr[slot, pp, 0], mb_scr[slot, pp, 1])

    def exponentials(n):
        slot = n % NSLOT
        for pp in range(npair):
            for hh in range(2):
                mb = mb_scr[slot, pp, hh]
                p_scr[slot, pp, hh] = jnp.exp2(s_scr[slot, pp, hh] - jnp.concatenate([mb, mb], axis=1)).astype(BF16)

    def weighted_values(n):
        slot = n % NSLOT
        r = pl.ds(n * BLK, BLK)
        rk = pl.ds(n * BLK, 2 * BLK)
        for pp in range(npair):
            ul0 = jnp.dot(p_scr[slot, pp, 0], v0b[rk, :], preferred_element_type=F32)
            ul1 = jnp.dot(p_scr[slot, pp, 1], v1b[rk, :], preferred_element_type=F32)
            u = jnp.where(lo, ul0, ul1)
            l = pltpu.roll(jnp.where(lo, ul1, ul0), HEAD_DIM, 1)
            o = u / (l + jnp.exp2(snk[pp:pp + 1, :] - m_scr[slot, pp]))
            o_ref[r, pl.ds(pp * LANES, LANES)] = (o * _silu(g_refs[pp][r, :].astype(F32))).astype(o_ref.dtype)

    for n in range(nblk + 3):
        if n - 3 >= 0:
            weighted_values(n - 3)
        if n < nblk:
            scores(n)
        if 0 <= n - 1 < nblk:
            row_max(n - 1)
        if 0 <= n - 2 < nblk:
            exponentials(n - 2)


def _attn_a(z3, gq2, gk2, snk3, tabs, casts=()):
    bsz, seq, _ = z3.shape
    npair = A_HEADS // A_KV_HEADS // 2

    def col(off, pp):
        return pl.BlockSpec((None, seq, LANES), lambda b, gi: (b, 0, off + npair * gi + pp))

    def fixed(off):
        return pl.BlockSpec((None, seq, LANES), lambda b, gi: (b, 0, off))

    return _call_with_casts(
        functools.partial(_attn_a_kernel, seq=seq),
        grid=(bsz, A_KV_HEADS),
        in_specs=[col(_QA, 0), col(_QA, 1), col(_QA, 2), col(_QA, 3), fixed(_KA), fixed(_VA),
                  col(_GA, 0), col(_GA, 1), col(_GA, 2), col(_GA, 3),
                  pl.BlockSpec((1, LANES), lambda b, gi: (0, 0)),
                  pl.BlockSpec((1, LANES), lambda b, gi: (0, 0)),
                  pl.BlockSpec((None, npair, LANES), lambda b, gi: (gi, 0, 0)),
                  pl.BlockSpec((None, 2 * npair, 2, BLK, 2 * BLK), lambda b, gi: (N_B_PATTERNS, gi, 0, 0, 0))],
        out_specs=pl.BlockSpec((None, seq, npair * LANES), lambda b, gi: (b, 0, gi)),
        out_shape=jax.ShapeDtypeStruct((bsz, seq, A_WIDTH), BF16),
        scratch_shapes=[pltpu.VMEM((npair, seq, LANES), BF16),
                        pltpu.VMEM((PAD + seq, LANES), BF16),
                        pltpu.VMEM((PAD + seq, LANES), BF16),
                        pltpu.VMEM((PAD + seq, LANES), BF16),
                        pltpu.VMEM((NSLOT, npair, 2, BLK, 2 * BLK), F32),
                        pltpu.VMEM((NSLOT, npair, 2, BLK, LANES), F32),
                        pltpu.VMEM((NSLOT, npair, 2, BLK, 2 * BLK), BF16),
                        pltpu.VMEM((NSLOT, npair, BLK, LANES), F32)],
        name="attn_a",
        args=(z3, z3, z3, z3, z3, z3, z3, z3, z3, z3, gq2, gk2, snk3, tabs),
        casts=casts)


def _attn_b_kernel(q_ref, k_ref, v_ref, g_ref, gq_ref, gk_ref, bias_ref, o_ref,
                   qf, kf, vf, q4f, k4f, v4f, qb, kb, v0b, v1b, s_scr, mb_scr, p_scr, m_scr, l_scr, u_scr, *, seq):
    lo = _lane_lo(BLK)
    hi = jnp.logical_not(lo)
    nblk = seq // BLK
    zeros = jnp.zeros((PAD, LANES), BF16)
    for pi in range(2):
        kb[pi, pl.ds(0, PAD), :] = zeros
        v0b[pi, pl.ds(0, PAD), :] = zeros
        v1b[pi, pl.ds(0, PAD), :] = zeros

    def put(pi, dst, q, k, v):
        qb[pi, dst, :] = q.astype(BF16)
        kb[pi, dst, :] = k.astype(BF16)
        v0b[pi, dst, :] = jnp.where(lo, v, 1.0).astype(BF16)
        v1b[pi, dst, :] = jnp.where(lo, 1.0, v).astype(BF16)

    def strided(t):
        return pl.ds(_and(t, 3) * (4 * BLK) + _shr(t, 2), BLK, stride=4)

    gains = jnp.concatenate([gq_ref[...] * (SCALE * LOG2E), gk_ref[...]], axis=1)

    def prep(c, carry):
        r = pl.ds(c * BLK, BLK)
        qk = jnp.concatenate([q_ref[r, :], k_ref[r, :]], axis=1).astype(F32)
        n = _head_rms(qk, gains)
        q, k, v = n[:, :LANES], n[:, LANES:], v_ref[r, :].astype(F32)
        qf[r, :] = q
        kf[r, :] = k
        vf[r, :] = v
        put(0, pl.ds(PAD + c * BLK, BLK), q, k, v)
        return carry
    for c in range(nblk):
        prep(c, 0)

    def deint4(t, carry):
        src = strided(t)
        dst = pl.ds(t * BLK, BLK)
        q, k, v = qf[src, :], kf[src, :], vf[src, :]
        q4f[dst, :] = q
        k4f[dst, :] = k
        v4f[dst, :] = v
        put(1, pl.ds(PAD + t * BLK, BLK), q, k, v)
        return carry
    for t in range(nblk):
        deint4(t, 0)

    def deint16(t, carry):
        src = strided(t)
        put(2, pl.ds(PAD + t * BLK, BLK), q4f[src, :], k4f[src, :], v4f[src, :])
        return carry
    for t in range(nblk):
        deint16(t, 0)

    groups = [(pi, g) for pi in range(N_B_PATTERNS) for g in range(nblk // UNROLL)]

    def geometry(pi):
        single = pi == 2
        return single, (BLK if single else 2 * BLK), (PAD if single else PAD - BLK)

    def out_rows(pi, b):
        return pl.ds(b * BLK, BLK) if pi == 0 else strided(b)

    def scores(t):
        pi, g = groups[t]
        single, keys, koff = geometry(pi)
        for uu in range(UNROLL):
            b = g * UNROLL + uu
            first = int(b % (nblk if pi == 0 else 4) == 0)
            q = qb[pi, pl.ds(PAD + b * BLK, BLK), :]
            kk = kb[pi, pl.ds(koff + b * BLK, keys), :]
            for hh, sel in enumerate((lo, hi)):
                bias = bias_ref[pi, hh, 0, :, pl.ds(BLK, BLK)] if single else bias_ref[pi, hh, first]
                s_scr[(t % NSLOT) * UNROLL + uu, hh, :, pl.ds(0, keys)] = lax.dot_general(
                    jnp.where(sel, q, jnp.zeros_like(q)), kk, (((1,), (1,)), ((), ())),
                    preferred_element_type=F32) + bias

    def row_max(t):
        pi, g = groups[t]
        _, keys, _ = geometry(pi)
        for uu in range(UNROLL):
            b = g * UNROLL + uu
            slot = (t % NSLOT) * UNROLL + uu
            for hh in range(2):
                m = jnp.max(s_scr[slot, hh, :, pl.ds(0, keys)], axis=-1, keepdims=True)
                mb_scr[slot, hh] = jnp.broadcast_to(m, (BLK, LANES))
            m_scr[pi, out_rows(pi, b), :] = jnp.where(lo, mb_scr[slot, 0], mb_scr[slot, 1])

    def exponentials(t):
        pi, g = groups[t]
        _, keys, _ = geometry(pi)
        for uu in range(UNROLL):
            slot = (t % NSLOT) * UNROLL + uu
            for hh in range(2):
                mb = mb_scr[slot, hh]
                mb = mb if keys == BLK else jnp.concatenate([mb, mb], axis=1)
                p_scr[slot, hh, :, pl.ds(0, keys)] = jnp.exp2(s_scr[slot, hh, :, pl.ds(0, keys)] - mb).astype(BF16)

    def weighted_values(t):
        pi, g = groups[t]
        _, keys, koff = geometry(pi)
        for uu in range(UNROLL):
            b = g * UNROLL + uu
            slot = (t % NSLOT) * UNROLL + uu
            rk = pl.ds(koff + b * BLK, keys)
            ul0 = jnp.dot(p_scr[slot, 0, :, pl.ds(0, keys)], v0b[pi, rk, :], preferred_element_type=F32)
            ul1 = jnp.dot(p_scr[slot, 1, :, pl.ds(0, keys)], v1b[pi, rk, :], preferred_element_type=F32)
            u_scr[pi, out_rows(pi, b), :] = jnp.where(lo, ul0, ul1)
            l_scr[pi, out_rows(pi, b), :] = pltpu.roll(jnp.where(lo, ul1, ul0), HEAD_DIM, 1)

    for t in range(len(groups) + 3):
        if t - 3 >= 0:
            weighted_values(t - 3)
        if t < len(groups):
            scores(t)
        if 0 <= t - 1 < len(groups):
            row_max(t - 1)
        if 0 <= t - 2 < len(groups):
            exponentials(t - 2)

    def renat(t, carry):
        for ref in (m_scr, l_scr, u_scr):
            ref[3, strided(t), :] = ref[2, pl.ds(t * BLK, BLK), :]
        return carry
    lax.fori_loop(0, nblk, renat, 0)

    def combine(c, carry):
        r = pl.ds(c * BLK, BLK)
        slots = (0, 1, 3)
        ms = [m_scr[s, r, :] for s in slots]
        mx = jnp.maximum(jnp.maximum(ms[0], ms[1]), ms[2])
        ws = [jnp.exp2(m - mx) for m in ms]
        add = lambda a, b: a + b
        num = functools.reduce(add, [w * u_scr[s, r, :] for s, w in zip(slots, ws)])
        den = functools.reduce(add, [w * l_scr[s, r, :] for s, w in zip(slots, ws)])
        o_ref[r, :] = ((num / den) * _silu(g_ref[r, :].astype(F32))).astype(o_ref.dtype)
        return carry
    lax.fori_loop(0, nblk, combine, 0, unroll=2)


def _attn_b(z3, gq2, gk2, tabs, casts=()):
    bsz, seq, _ = z3.shape
    col = lambda off: pl.BlockSpec((None, seq, LANES), lambda b, p: (b, 0, off + p))
    f32_rows = pltpu.VMEM((seq, LANES), F32)
    bf16_ops = pltpu.VMEM((N_B_PATTERNS, PAD + seq, LANES), BF16)
    stats = pltpu.VMEM((N_B_PATTERNS + 1, seq, LANES), F32)
    return _call_with_casts(
        functools.partial(_attn_b_kernel, seq=seq),
        grid=(bsz, B_HEADS // 2),
        in_specs=[col(_QB), col(_KB), col(_VB), col(_GB),
                  pl.BlockSpec((1, LANES), lambda b, p: (0, 0)),
                  pl.BlockSpec((1, LANES), lambda b, p: (0, 0)),
                  pl.BlockSpec((N_B_PATTERNS, 2, 2, BLK, 2 * BLK), lambda b, p: (0, p, 0, 0, 0))],
        out_specs=pl.BlockSpec((None, seq, LANES), lambda b, p: (b, 0, p)),
        out_shape=jax.ShapeDtypeStruct((bsz, seq, B_WIDTH), BF16),
        scratch_shapes=[f32_rows] * 6 + [bf16_ops] * 4
                       + [pltpu.VMEM((NSLOT * UNROLL, 2, BLK, 2 * BLK), F32),
                          pltpu.VMEM((NSLOT * UNROLL, 2, BLK, LANES), F32),
                          pltpu.VMEM((NSLOT * UNROLL, 2, BLK, 2 * BLK), BF16)] + [stats] * 3,
        name="attn_b",
        args=(z3, z3, z3, z3, gq2, gk2, tabs),
        casts=casts)


def _even_out_kernel(ya_ref, yb_ref, wa_ref, wb_ref, x_ref, o_ref):
    acc = jnp.dot(ya_ref[...], wa_ref[...], preferred_element_type=F32)
    acc = acc + jnp.dot(yb_ref[...], wb_ref[...], preferred_element_type=F32)
    o_ref[...] = x_ref[...] + acc


def _resident(block, index_map, n_col_tiles):
    if n_col_tiles == 1:
        return pl.BlockSpec(block, index_map, pipeline_mode=pl.Buffered(1))
    return pl.BlockSpec(block, index_map)


def _even_out(ya, yb, w, layer, x2, *, tm=512, tn=2048):
    m, d = x2.shape
    ka, kb = ya.shape[1], yb.shape[1]
    assert ka == kb
    return pl.pallas_call(
        _even_out_kernel,
        grid=(m // tm, d // tn),
        in_specs=[pl.BlockSpec((tm, ka), lambda i, j: (i, 0)),
                  pl.BlockSpec((tm, kb), lambda i, j: (i, 0)),
                  _resident((None, ka, tn), lambda i, j: (layer, 0, j), d // tn),
                  _resident((None, kb, tn), lambda i, j: (layer, 1, j), d // tn),
                  pl.BlockSpec((tm, tn), lambda i, j: (i, j))],
        out_specs=pl.BlockSpec((tm, tn), lambda i, j: (i, j)),
        out_shape=jax.ShapeDtypeStruct((m, d), F32),
        compiler_params=_params(("arbitrary", "arbitrary")),
        name="even_out",
    )(ya, yb, w, w, x2)


def _odd_out_kernel(y_ref, w_ref, x_ref, o_ref):
    o_ref[...] = x_ref[...] + jnp.dot(y_ref[...], w_ref[...], preferred_element_type=F32)


def _odd_out(y, w, layer, x2, *, tm=512, tn=2048):
    m, d = x2.shape
    k = y.shape[1]
    return pl.pallas_call(
        _odd_out_kernel,
        grid=(m // tm, d // tn),
        in_specs=[pl.BlockSpec((tm, k), lambda i, j: (i, 0)),
                  _resident((None, k, tn), lambda i, j: (layer, 0, j), d // tn),
                  pl.BlockSpec((tm, tn), lambda i, j: (i, j))],
        out_specs=pl.BlockSpec((tm, tn), lambda i, j: (i, j)),
        out_shape=jax.ShapeDtypeStruct((m, d), F32),
        compiler_params=_params(("arbitrary", "arbitrary")),
        name="odd_out",
    )(y, w, x2)


def kernel(x, ev_ln_g, ev_w_in, ev_qk_g, ev_sinks, ev_w_out, od_ln_g, od_w_in, od_v_g, od_w_s,
           od_b_s, od_w_out, rel_bias):
    bsz, seq, d = x.shape
    n_even, n_odd = ev_ln_g.shape[0], od_ln_g.shape[0]
    flat = lambda w: w.reshape(-1, w.shape[-1])
    whole = lambda w: _Cast(flat(w), 0, w.shape[0] * w.shape[1])
    layer_of = lambda w, j: _Cast(flat(w), j * w.shape[1], w.shape[1])
    tabs, done = _bias_tables(rel_bias, [layer_of(ev_w_in, 0)])
    ev_in_b = {0: done[0][None]}
    od_in_b, ev_out_b, od_out_b = {}, None, None
    x2 = x.reshape(bsz * seq, d)
    for i in range(n_even + n_odd):
        j = i // 2
        if i % 2 == 0:
            z = _even_in(x2, ev_ln_g[j].reshape(1, d), ev_in_b[j], 0)
            z3 = z.reshape(bsz, seq, EVEN_IN)
            gains = jnp.tile(ev_qk_g[j].astype(F32), (1, 2))
            snk3 = jnp.repeat(ev_sinks[j].astype(F32), HEAD_DIM).reshape(A_KV_HEADS, -1, LANES)
            ya, done = _attn_a(z3, gains[0:1], gains[1:2], snk3, tabs, [whole(ev_w_out)] if j == 0 else [])
            if j == 0:
                ev_out_b = done[0].reshape(ev_w_out.shape)
            yb, done = _attn_b(z3, gains[2:3], gains[3:4], tabs, [layer_of(od_w_in, j)] if j < n_odd else [])
            if j < n_odd:
                od_in_b[j] = done[0][None]
            x2 = _even_out(ya.reshape(bsz * seq, A_WIDTH), yb.reshape(bsz * seq, B_WIDTH), ev_out_b, j, x2)
        else:
            casts = ([whole(od_w_out)] if j == 0 else []) + ([layer_of(ev_w_in, j + 1)] if j + 1 < n_even else [])
            y, done = _odd_in(x2, od_ln_g[j].reshape(1, d), od_in_b[j], 0, od_v_g[j].astype(F32),
                              od_w_s[j].astype(F32), od_b_s[j].astype(F32), casts)
            if j == 0:
                od_out_b = done.pop(0).reshape(od_w_out.shape)
            if j + 1 < n_even:
                ev_in_b[j + 1] = done.pop(0)[None]
            x2 = _odd_out(y, od_out_b, j, x2)
    return x2.reshape(bsz, seq, d)
```

```python
import functools
import math
from typing import NamedTuple

import numpy as np
import jax
import jax.numpy as jnp
from jax import lax
from jax.experimental import pallas as pl
from jax.experimental.pallas import tpu as pltpu

F32 = jnp.float32
BF16 = jnp.bfloat16

D_MODEL = 2048
HEAD_DIM = 64
A_HEADS = 16
A_KV_HEADS = 2
B_HEADS = 16
BLK = 128
NUM_BUCKETS = 32
REL_MAX_DISTANCE = 2048
A_WIDTH = A_HEADS * HEAD_DIM
B_WIDTH = B_HEADS * HEAD_DIM
EVEN_IN = 6400
C_WIDTH = 2 * D_MODEL
C_GROUPS = 16
C_GROUP_DIM = C_WIDTH // C_GROUPS
C_CHUNK = 128
EPS = 1e-6
NEG = -1e30
SCALE = HEAD_DIM ** -0.5
LOG2E = 1.4426950408889634
PATTERNS = ((1, 128), (4, 128), (16, 128), (1, 127))
N_B_PATTERNS = 3

LANES = 128
BF16_SUBLANES = 16
NORM_ROWS = 128
VMEM_LIMIT = 60 * 1024 * 1024

_QA, _KA, _VA, _GA = 0, 8, 9, 10
_QB, _KB, _VB, _GB = 18, 26, 34, 42


def _params(sem):
    return pltpu.CompilerParams(dimension_semantics=sem, vmem_limit_bytes=VMEM_LIMIT)


class _Cast(NamedTuple):
    src: jax.Array
    first_row: int
    n_rows: int


def _call_with_casts(body, *, grid, in_specs, out_specs, out_shape, scratch_shapes, name, args, casts=(),
                     cast_steps=None):
    n_in, n_cast = len(in_specs), len(casts)
    inner = grid[1] if cast_steps is None else cast_steps
    steps = grid[0] * inner
    tile_of = lambda i, j: i * inner + jnp.minimum(j, inner - 1)
    cast_in, cast_out, cast_shape = [], [], []
    for c in casts:
        cols = c.src.shape[1]
        tile = c.n_rows // steps
        first = c.first_row // tile
        assert tile * steps == c.n_rows and first * tile == c.first_row and tile % BF16_SUBLANES == 0
        cast_in.append(pl.BlockSpec((tile, cols), lambda i, j, first=first: (first + tile_of(i, j), 0)))
        cast_out.append(pl.BlockSpec((tile, cols), lambda i, j: (tile_of(i, j), 0)))
        cast_shape.append(jax.ShapeDtypeStruct((c.n_rows, cols), BF16))

    def kernel(*refs):
        ins, srcs = refs[:n_in], refs[n_in:n_in + n_cast]
        out, dsts = refs[n_in + n_cast], refs[n_in + n_cast + 1:n_in + 2 * n_cast + 1]
        for src, dst in zip(srcs, dsts):
            dst[...] = src[...].astype(dst.dtype)
        body(*ins, out, *refs[n_in + 2 * n_cast + 1:])

    res = pl.pallas_call(
        kernel,
        grid=grid,
        in_specs=list(in_specs) + cast_in,
        out_specs=[out_specs] + cast_out,
        out_shape=[out_shape] + cast_shape,
        scratch_shapes=scratch_shapes,
        compiler_params=_params(("arbitrary", "arbitrary")),
        name=name,
    )(*args, *[c.src for c in casts])
    return res[0], list(res[1:])


def _bucket_tables():
    a = np.arange(BLK)[:, None]
    b = np.arange(2 * BLK)[None, :]
    dist = BLK + a - b
    max_exact = NUM_BUCKETS // 2
    out = []
    for dil, max_dist in PATTERNS:
        n = np.maximum(dist * dil, 0)
        large = max_exact + (np.log(np.maximum(n, 1) / max_exact)
                             / np.log(REL_MAX_DISTANCE / max_exact)
                             * (NUM_BUCKETS - max_exact)).astype(np.int32)
        large = np.minimum(large, NUM_BUCKETS - 1)
        bucket = np.where(n < max_exact, n, large).astype(np.int32)
        valid = (dist >= 0) & (dist <= max_dist)
        out.append(np.where(valid, bucket, -1).astype(np.int32))
    return np.stack(out)


def _bias_kernel(tbl_ref, bucket_ref, out_ref):
    col0 = jnp.where(pl.program_id(0) < N_B_PATTERNS, A_HEADS, 0)
    bk = bucket_ref[...]
    prev_cols = lax.broadcasted_iota(jnp.int32, bk.shape, 1) < BLK

    def head(h, carry):
        acc = jnp.full(bk.shape, NEG, F32)
        for b in range(NUM_BUCKETS):
            acc = jnp.where(bk == b, tbl_ref[b, col0 + h] * LOG2E, acc)
        out_ref[h, 0] = acc
        out_ref[h, 1] = jnp.where(prev_cols, NEG, acc)
        return carry
    lax.fori_loop(0, out_ref.shape[0], head, 0)


def _bias_tables(rel_bias, casts=()):
    buckets = jnp.asarray(_bucket_tables())
    heads = max(A_HEADS, B_HEADS)
    return _call_with_casts(
        _bias_kernel,
        grid=(len(PATTERNS), 1),
        in_specs=[pl.BlockSpec(memory_space=pltpu.SMEM),
                  pl.BlockSpec((None, BLK, 2 * BLK), lambda t, _: (t, 0, 0))],
        out_specs=pl.BlockSpec((None, heads, 2, BLK, 2 * BLK), lambda t, _: (t, 0, 0, 0, 0)),
        out_shape=jax.ShapeDtypeStruct((len(PATTERNS), heads, 2, BLK, 2 * BLK), F32),
        scratch_shapes=[],
        name="bias_tables",
        args=(rel_bias.astype(F32), buckets),
        casts=casts)


def _norm_rows(x_ref, g_ref, h_scr, first, count):
    for c in range(first // NORM_ROWS, (first + count) // NORM_ROWS):
        r = pl.ds(c * NORM_ROWS, NORM_ROWS)
        x = x_ref[r, :]
        ms = jnp.mean(x * x, axis=-1, keepdims=True)
        h_scr[r, :] = (x * lax.rsqrt(ms + EPS) * g_ref[...]).astype(BF16)


def _even_in_kernel(x_ref, g_ref, w_ref, o_ref, h_scr, *, tm):
    j = pl.program_id(1)

    @pl.when(j == 0)
    def _():
        half = tm // 2
        for first in (0, half):
            _norm_rows(x_ref, g_ref, h_scr, first, half)
            rows = pl.ds(first, half)
            o_ref[rows, :] = jnp.dot(h_scr[rows, :], w_ref[...], preferred_element_type=F32).astype(o_ref.dtype)

    @pl.when(j > 0)
    def _():
        o_ref[...] = jnp.dot(h_scr[...], w_ref[...], preferred_element_type=F32).astype(o_ref.dtype)


def _even_in(x2, g, w, layer, *, tm=1024, tn=1280):
    m, d = x2.shape
    n = w.shape[2]
    return pl.pallas_call(
        functools.partial(_even_in_kernel, tm=tm),
        grid=(m // tm, n // tn),
        in_specs=[pl.BlockSpec((tm, d), lambda i, j: (i, 0)),
                  pl.BlockSpec((1, d), lambda i, j: (0, 0)),
                  pl.BlockSpec((None, d, tn), lambda i, j: (layer, 0, j))],
        out_specs=pl.BlockSpec((tm, tn), lambda i, j: (i, j)),
        out_shape=jax.ShapeDtypeStruct((m, n), BF16),
        scratch_shapes=[pltpu.VMEM((tm, d), BF16)],
        compiler_params=_params(("arbitrary", "arbitrary")),
        name="even_in",
    )(x2, g, w)


def _gelu(x):
    return 0.5 * x * (1.0 + lax.erf(x * (1.0 / math.sqrt(2.0))))


def _silu(x):
    return (0.5 * x) * (1.0 + jnp.tanh(0.5 * x))


def _odd_in_kernel(x_ref, g_ref, wv_ref, wu_ref, wg_ref, vg_ref, ws_ref, bs_ref, y_ref,
                   h_scr, v_scr, ssq_scr, *, tm, tn, v_tiles):
    j = pl.program_id(1)
    nbv = C_WIDTH // (tn * v_tiles)

    def v_tiles_of(first, count):
        rows = pl.ds(first, count)
        v = _gelu(jnp.dot(h_scr[rows, :], wv_ref[...], preferred_element_type=F32))
        for k in range(v_tiles):
            v_scr[j * v_tiles + k, rows, :] = v[:, k * tn:(k + 1) * tn].astype(v_scr.dtype)
        ssq_scr[rows, :] += jnp.sum(v * v, axis=-1, keepdims=True)

    @pl.when(j == 0)
    def _():
        ssq_scr[...] = jnp.zeros_like(ssq_scr)
        half = tm // 2
        for first in (0, half):
            _norm_rows(x_ref, g_ref, h_scr, first, half)
            v_tiles_of(first, half)

    @pl.when(jnp.logical_and(j > 0, j < nbv))
    def _():
        v_tiles_of(0, tm)

    @pl.when(j >= nbv)
    def _():
        jj = j - nbv
        h = h_scr[...]
        ug = (_gelu(jnp.dot(h, wu_ref[...], preferred_element_type=F32))
              * _silu(jnp.dot(h, wg_ref[...], preferred_element_type=F32)))
        tril = (lax.broadcasted_iota(jnp.int32, (C_CHUNK, C_CHUNK), 0)
                >= lax.broadcasted_iota(jnp.int32, (C_CHUNK, C_CHUNK), 1))
        groups_per_tile = tn // C_GROUP_DIM
        vg = vg_ref[jj]
        for gg in range(groups_per_tile):
            grp = jj * groups_per_tile + gg
            wt = jnp.where(tril, ws_ref[grp], 0.0).astype(BF16)
            cols = slice(gg * C_GROUP_DIM, (gg + 1) * C_GROUP_DIM)
            for c in range(tm // C_CHUNK):
                rows = slice(c * C_CHUNK, (c + 1) * C_CHUNK)
                inv = lax.rsqrt(ssq_scr[rows, :][:, :1] * (1.0 / C_WIDTH) + EPS)
                vn = (v_scr[jj, rows, cols].astype(F32) * inv * vg[:, cols]).astype(BF16)
                s = jnp.dot(wt, vn, preferred_element_type=F32) + bs_ref[grp]
                y_ref[rows, cols] = (ug[rows, cols] * s).astype(y_ref.dtype)


def _odd_in(x2, g, w, layer, vg, ws, bs, casts=(), *, tm=1024, tn=512, v_tiles=2):
    m, d = x2.shape
    nb = C_WIDTH // tn
    nbv = nb // v_tiles
    return _call_with_casts(
        functools.partial(_odd_in_kernel, tm=tm, tn=tn, v_tiles=v_tiles),
        grid=(m // tm, nbv + nb),
        in_specs=[pl.BlockSpec((tm, d), lambda i, j: (i, 0)),
                  pl.BlockSpec((1, d), lambda i, j: (0, 0)),
                  pl.BlockSpec((None, d, tn * v_tiles), lambda i, j: (layer, 0, nbv + jnp.minimum(j, nbv - 1))),
                  pl.BlockSpec((None, d, tn), lambda i, j: (layer, 0, jnp.maximum(j - nbv, 0))),
                  pl.BlockSpec((None, d, tn), lambda i, j: (layer, 0, 2 * nb + jnp.maximum(j - nbv, 0))),
                  pl.BlockSpec((nb, 1, tn), lambda i, j: (0, 0, 0)),
                  pl.BlockSpec((C_GROUPS, C_CHUNK, C_CHUNK), lambda i, j: (0, 0, 0)),
                  pl.BlockSpec((C_GROUPS, C_CHUNK, 1), lambda i, j: (0, 0, 0))],
        out_specs=pl.BlockSpec((tm, tn), lambda i, j: (i, jnp.maximum(j - nbv, 0))),
        out_shape=jax.ShapeDtypeStruct((m, C_WIDTH), BF16),
        scratch_shapes=[pltpu.VMEM((tm, d), BF16),
                        pltpu.VMEM((nb, tm, tn), BF16),
                        pltpu.VMEM((tm, LANES), F32)],
        name="odd_in",
        args=(x2, g, w, w, w, vg.reshape(nb, 1, tn), ws, bs.reshape(C_GROUPS, C_CHUNK, 1)),
        casts=casts,
        cast_steps=nb)


PAD = BLK
UNROLL = 4
NSLOT = 3


def _and(b, mask):
    return b & mask if isinstance(b, int) else lax.bitwise_and(b, mask)


def _shr(b, s):
    return b >> s if isinstance(b, int) else lax.shift_right_logical(b, s)


def _lane_lo(rows):
    return lax.broadcasted_iota(jnp.int32, (rows, LANES), 1) < HEAD_DIM


def _head_rms(x, gain):
    w = x.shape[1]
    r = lax.broadcasted_iota(jnp.int32, (w, w), 0) // HEAD_DIM
    c = lax.broadcasted_iota(jnp.int32, (w, w), 1) // HEAD_DIM
    mean_bd = jnp.where(r == c, 1.0 / HEAD_DIM, 0.0).astype(BF16)
    x2 = x * x
    hi = x2.astype(BF16)
    lo = (x2 - hi.astype(F32)).astype(BF16)
    ms = (jnp.dot(hi, mean_bd, preferred_element_type=F32)
          + jnp.dot(lo, mean_bd, preferred_element_type=F32))
    return x * lax.rsqrt(ms + EPS) * gain


def _attn_a_kernel(q0_ref, q1_ref, q2_ref, q3_ref, k_ref, v_ref, g0_ref, g1_ref, g2_ref, g3_ref,
                   gq_ref, gk_ref, snk_ref, bias_ref, o_ref, qb, kb, v0b, v1b, s_scr, mb_scr, p_scr, m_scr, *, seq):
    q_refs = (q0_ref, q1_ref, q2_ref, q3_ref)
    g_refs = (g0_ref, g1_ref, g2_ref, g3_ref)
    npair = len(q_refs)
    nblk = seq // BLK
    lane = lax.broadcasted_iota(jnp.int32, (BLK, LANES), 1)
    keep = (lane // HEAD_DIM) == pl.program_id(1)
    lo = lane < HEAD_DIM
    hi = jnp.logical_not(lo)
    zeros = jnp.zeros((PAD, LANES), BF16)
    kb[pl.ds(0, PAD), :] = zeros
    v0b[pl.ds(0, PAD), :] = zeros
    v1b[pl.ds(0, PAD), :] = zeros
    gq = gq_ref[...] * (SCALE * LOG2E)
    gq = jnp.concatenate([gq, gq], axis=1)
    snk = snk_ref[...] * LOG2E

    def prep(c, carry):
        r = pl.ds(c * BLK, BLK)
        ro = pl.ds(PAD + c * BLK, BLK)
        for pp in range(0, npair, 2):
            qq = jnp.concatenate([q_refs[pp][r, :], q_refs[pp + 1][r, :]], axis=1).astype(F32)
            qn = _head_rms(qq, gq).astype(BF16)
            qb[pp, r, :] = qn[:, :LANES]
            qb[pp + 1, r, :] = qn[:, LANES:]
        kn = _head_rms(k_ref[r, :].astype(F32), gk_ref[...])
        kb[ro, :] = jnp.where(keep, kn, pltpu.roll(kn, HEAD_DIM, 1)).astype(BF16)
        v = v_ref[r, :].astype(F32)
        vd = jnp.where(keep, v, pltpu.roll(v, HEAD_DIM, 1))
        v0b[ro, :] = jnp.where(lo, vd, 1.0).astype(BF16)
        v1b[ro, :] = jnp.where(lo, 1.0, vd).astype(BF16)
        return carry
    lax.fori_loop(0, nblk, prep, 0, unroll=4)

    def scores(n):
        kk = kb[pl.ds(n * BLK, 2 * BLK), :]
        for pp in range(npair):
            q = qb[pp, pl.ds(n * BLK, BLK), :]
            for hh, sel in enumerate((lo, hi)):
                s_scr[n % NSLOT, pp, hh] = lax.dot_general(
                    jnp.where(sel, q, jnp.zeros_like(q)), kk, (((1,), (1,)), ((), ())),
                    preferred_element_type=F32) + bias_ref[2 * pp + hh, int(n == 0)]

    def row_max(n):
        slot = n % NSLOT
        for pp in range(npair):
            for hh in range(2):
                m = jnp.max(s_scr[slot, pp, hh], axis=-1, keepdims=True)
                mb_scr[slot, pp, hh] = jnp.broadcast_to(m, (BLK, LANES))
            m_scr[slot, pp] = jnp.where(lo, mb_scr[slot, pp, 0], mb_scr[slot, pp, 1])

    def exponentials(n):
        slot = n % NSLOT
        for pp in range(npair):
            for hh in range(2):
                mb = mb_scr[slot, pp, hh]
                p_scr[slot, pp, hh] = jnp.exp2(s_scr[slot, pp, hh] - jnp.concatenate([mb, mb], axis=1)).astype(BF16)

    def weighted_values(n):
        slot = n % NSLOT
        r = pl.ds(n * BLK, BLK)
        rk = pl.ds(n * BLK, 2 * BLK)
        for pp in range(npair):
            ul0 = jnp.dot(p_scr[slot, pp, 0], v0b[rk, :], preferred_element_type=F32)
            ul1 = jnp.dot(p_scr[slot, pp, 1], v1b[rk, :], preferred_element_type=F32)
            u = jnp.where(lo, ul0, ul1)
            l = pltpu.roll(jnp.where(lo, ul1, ul0), HEAD_DIM, 1)
            o = u / (l + jnp.exp2(snk[pp:pp + 1, :] - m_scr[slot, pp]))
            o_ref[r, pl.ds(pp * LANES, LANES)] = (o * _silu(g_refs[pp][r, :].astype(F32))).astype(o_ref.dtype)

    for n in range(nblk + 3):
        if n - 3 >= 0:
            weighted_values(n - 3)
        if n < nblk:
            scores(n)
        if 0 <= n - 1 < nblk:
            row_max(n - 1)
        if 0 <= n - 2 < nblk:
            exponentials(n - 2)


def _attn_a(z3, gq2, gk2, snk3, tabs, casts=()):
    bsz, seq, _ = z3.shape
    npair = A_HEADS // A_KV_HEADS // 2

    def col(off, pp):
        return pl.BlockSpec((None, seq, LANES), lambda b, gi: (b, 0, off + npair * gi + pp))

    def fixed(off):
        return pl.BlockSpec((None, seq, LANES), lambda b, gi: (b, 0, off))

    return _call_with_casts(
        functools.partial(_attn_a_kernel, seq=seq),
        grid=(bsz, A_KV_HEADS),
        in_specs=[col(_QA, 0), col(_QA, 1), col(_QA, 2), col(_QA, 3), fixed(_KA), fixed(_VA),
                  col(_GA, 0), col(_GA, 1), col(_GA, 2), col(_GA, 3),
                  pl.BlockSpec((1, LANES), lambda b, gi: (0, 0)),
                  pl.BlockSpec((1, LANES), lambda b, gi: (0, 0)),
                  pl.BlockSpec((None, npair, LANES), lambda b, gi: (gi, 0, 0)),
                  pl.BlockSpec((None, 2 * npair, 2, BLK, 2 * BLK), lambda b, gi: (N_B_PATTERNS, gi, 0, 0, 0))],
        out_specs=pl.BlockSpec((None, seq, npair * LANES), lambda b, gi: (b, 0, gi)),
        out_shape=jax.ShapeDtypeStruct((bsz, seq, A_WIDTH), BF16),
        scratch_shapes=[pltpu.VMEM((npair, seq, LANES), BF16),
                        pltpu.VMEM((PAD + seq, LANES), BF16),
                        pltpu.VMEM((PAD + seq, LANES), BF16),
                        pltpu.VMEM((PAD + seq, LANES), BF16),
                        pltpu.VMEM((NSLOT, npair, 2, BLK, 2 * BLK), F32),
                        pltpu.VMEM((NSLOT, npair, 2, BLK, LANES), F32),
                        pltpu.VMEM((NSLOT, npair, 2, BLK, 2 * BLK), BF16),
                        pltpu.VMEM((NSLOT, npair, BLK, LANES), F32)],
        name="attn_a",
        args=(z3, z3, z3, z3, z3, z3, z3, z3, z3, z3, gq2, gk2, snk3, tabs),
        casts=casts)


def _attn_b_kernel(q_ref, k_ref, v_ref, g_ref, gq_ref, gk_ref, bias_ref, o_ref,
                   qf, kf, vf, q4f, k4f, v4f, qb, kb, v0b, v1b, s_scr, mb_scr, p_scr, m_scr, l_scr, u_scr, *, seq):
    lo = _lane_lo(BLK)
    hi = jnp.logical_not(lo)
    nblk = seq // BLK
    zeros = jnp.zeros((PAD, LANES), BF16)
    for pi in range(2):
        kb[pi, pl.ds(0, PAD), :] = zeros
        v0b[pi, pl.ds(0, PAD), :] = zeros
        v1b[pi, pl.ds(0, PAD), :] = zeros

    def put(pi, dst, q, k, v):
        qb[pi, dst, :] = q.astype(BF16)
        kb[pi, dst, :] = k.astype(BF16)
        v0b[pi, dst, :] = jnp.where(lo, v, 1.0).astype(BF16)
        v1b[pi, dst, :] = jnp.where(lo, 1.0, v).astype(BF16)

    def strided(t):
        return pl.ds(_and(t, 3) * (4 * BLK) + _shr(t, 2), BLK, stride=4)

    gains = jnp.concatenate([gq_ref[...] * (SCALE * LOG2E), gk_ref[...]], axis=1)

    def prep(c, carry):
        r = pl.ds(c * BLK, BLK)
        qk = jnp.concatenate([q_ref[r, :], k_ref[r, :]], axis=1).astype(F32)
        n = _head_rms(qk, gains)
        q, k, v = n[:, :LANES], n[:, LANES:], v_ref[r, :].astype(F32)
        qf[r, :] = q
        kf[r, :] = k
        vf[r, :] = v
        put(0, pl.ds(PAD + c * BLK, BLK), q, k, v)
        return carry
    for c in range(nblk):
        prep(c, 0)

    def deint4(t, carry):
        src = strided(t)
        dst = pl.ds(t * BLK, BLK)
        q, k, v = qf[src, :], kf[src, :], vf[src, :]
        q4f[dst, :] = q
        k4f[dst, :] = k
        v4f[dst, :] = v
        put(1, pl.ds(PAD + t * BLK, BLK), q, k, v)
        return carry
    for t in range(nblk):
        deint4(t, 0)

    def deint16(t, carry):
        src = strided(t)
        put(2, pl.ds(PAD + t * BLK, BLK), q4f[src, :], k4f[src, :], v4f[src, :])
        return carry
    for t in range(nblk):
        deint16(t, 0)

    groups = [(pi, g) for pi in range(N_B_PATTERNS) for g in range(nblk // UNROLL)]

    def geometry(pi):
        single = pi == 2
        return single, (BLK if single else 2 * BLK), (PAD if single else PAD - BLK)

    def out_rows(pi, b):
        return pl.ds(b * BLK, BLK) if pi == 0 else strided(b)

    def scores(t):
        pi, g = groups[t]
        single, keys, koff = geometry(pi)
        for uu in range(UNROLL):
            b = g * UNROLL + uu
            first = int(b % (nblk if pi == 0 else 4) == 0)
            q = qb[pi, pl.ds(PAD + b * BLK, BLK), :]
            kk = kb[pi, pl.ds(koff + b * BLK, keys), :]
            for hh, sel in enumerate((lo, hi)):
                bias = bias_ref[pi, hh, 0, :, pl.ds(BLK, BLK)] if single else bias_ref[pi, hh, first]
                s_scr[(t % NSLOT) * UNROLL + uu, hh, :, pl.ds(0, keys)] = lax.dot_general(
                    jnp.where(sel, q, jnp.zeros_like(q)), kk, (((1,), (1,)), ((), ())),
                    preferred_element_type=F32) + bias

    def row_max(t):
        pi, g = groups[t]
        _, keys, _ = geometry(pi)
        for uu in range(UNROLL):
            b = g * UNROLL + uu
            slot = (t % NSLOT) * UNROLL + uu
            for hh in range(2):
                m = jnp.max(s_scr[slot, hh, :, pl.ds(0, keys)], axis=-1, keepdims=True)
                mb_scr[slot, hh] = jnp.broadcast_to(m, (BLK, LANES))
            m_scr[pi, out_rows(pi, b), :] = jnp.where(lo, mb_scr[slot, 0], mb_scr[slot, 1])

    def exponentials(t):
        pi, g = groups[t]
        _, keys, _ = geometry(pi)
        for uu in range(UNROLL):
            slot = (t % NSLOT) * UNROLL + uu
            for hh in range(2):
                mb = mb_scr[slot, hh]
                mb = mb if keys == BLK else jnp.concatenate([mb, mb], axis=1)
                p_scr[slot, hh, :, pl.ds(0, keys)] = jnp.exp2(s_scr[slot, hh, :, pl.ds(0, keys)] - mb).astype(BF16)

    def weighted_values(t):
        pi, g = groups[t]
        _, keys, koff = geometry(pi)
        for uu in range(UNROLL):
            b = g * UNROLL + uu
            slot = (t % NSLOT) * UNROLL + uu
            rk = pl.ds(koff + b * BLK, keys)
            ul0 = jnp.dot(p_scr[slot, 0, :, pl.ds(0, keys)], v0b[pi, rk, :], preferred_element_type=F32)
            ul1 = jnp.dot(p_scr[slot, 1, :, pl.ds(0, keys)], v1b[pi, rk, :], preferred_element_type=F32)
            u_scr[pi, out_rows(pi, b), :] = jnp.where(lo, ul0, ul1)
            l_scr[pi, out_rows(pi, b), :] = pltpu.roll(jnp.where(lo, ul1, ul0), HEAD_DIM, 1)

    for t in range(len(groups) + 3):
        if t - 3 >= 0:
            weighted_values(t - 3)
        if t < len(groups):
            scores(t)
        if 0 <= t - 1 < len(groups):
            row_max(t - 1)
        if 0 <= t - 2 < len(groups):
            exponentials(t - 2)

    def renat(t, carry):
        for ref in (m_scr, l_scr, u_scr):
            ref[3, strided(t), :] = ref[2, pl.ds(t * BLK, BLK), :]
        return carry
    lax.fori_loop(0, nblk, renat, 0)

    def combine(c, carry):
        r = pl.ds(c * BLK, BLK)
        slots = (0, 1, 3)
        ms = [m_scr[s, r, :] for s in slots]
        mx = jnp.maximum(jnp.maximum(ms[0], ms[1]), ms[2])
        ws = [jnp.exp2(m - mx) for m in ms]
        add = lambda a, b: a + b
        num = functools.reduce(add, [w * u_scr[s, r, :] for s, w in zip(slots, ws)])
        den = functools.reduce(add, [w * l_scr[s, r, :] for s, w in zip(slots, ws)])
        o_ref[r, :] = ((num / den) * _silu(g_ref[r, :].astype(F32))).astype(o_ref.dtype)
        return carry
    lax.fori_loop(0, nblk, combine, 0, unroll=2)


def _attn_b(z3, gq2, gk2, tabs, casts=()):
    bsz, seq, _ = z3.shape
    col = lambda off: pl.BlockSpec((None, seq, LANES), lambda b, p: (b, 0, off + p))
    f32_rows = pltpu.VMEM((seq, LANES), F32)
    bf16_ops = pltpu.VMEM((N_B_PATTERNS, PAD + seq, LANES), BF16)
    stats = pltpu.VMEM((N_B_PATTERNS + 1, seq, LANES), F32)
    return _call_with_casts(
        functools.partial(_attn_b_kernel, seq=seq),
        grid=(bsz, B_HEADS // 2),
        in_specs=[col(_QB), col(_KB), col(_VB), col(_GB),
                  pl.BlockSpec((1, LANES), lambda b, p: (0, 0)),
                  pl.BlockSpec((1, LANES), lambda b, p: (0, 0)),
                  pl.BlockSpec((N_B_PATTERNS, 2, 2, BLK, 2 * BLK), lambda b, p: (0, p, 0, 0, 0))],
        out_specs=pl.BlockSpec((None, seq, LANES), lambda b, p: (b, 0, p)),
        out_shape=jax.ShapeDtypeStruct((bsz, seq, B_WIDTH), BF16),
        scratch_shapes=[f32_rows] * 6 + [bf16_ops] * 4
                       + [pltpu.VMEM((NSLOT * UNROLL, 2, BLK, 2 * BLK), F32),
                          pltpu.VMEM((NSLOT * UNROLL, 2, BLK, LANES), F32),
                          pltpu.VMEM((NSLOT * UNROLL, 2, BLK, 2 * BLK), BF16)] + [stats] * 3,
        name="attn_b",
        args=(z3, z3, z3, z3, gq2, gk2, tabs),
        casts=casts)


def _even_out_kernel(ya_ref, yb_ref, wa_ref, wb_ref, x_ref, o_ref):
    acc = jnp.dot(ya_ref[...], wa_ref[...], preferred_element_type=F32)
    acc = acc + jnp.dot(yb_ref[...], wb_ref[...], preferred_element_type=F32)
    o_ref[...] = x_ref[...] + acc


def _resident(block, index_map, n_col_tiles):
    if n_col_tiles == 1:
        return pl.BlockSpec(block, index_map, pipeline_mode=pl.Buffered(1))
    return pl.BlockSpec(block, index_map)


def _even_out(ya, yb, w, layer, x2, *, tm=512, tn=2048):
    m, d = x2.shape
    ka, kb = ya.shape[1], yb.shape[1]
    assert ka == kb
    return pl.pallas_call(
        _even_out_kernel,
        grid=(m // tm, d // tn),
        in_specs=[pl.BlockSpec((tm, ka), lambda i, j: (i, 0)),
                  pl.BlockSpec((tm, kb), lambda i, j: (i, 0)),
                  _resident((None, ka, tn), lambda i, j: (layer, 0, j), d // tn),
                  _resident((None, kb, tn), lambda i, j: (layer, 1, j), d // tn),
                  pl.BlockSpec((tm, tn), lambda i, j: (i, j))],
        out_specs=pl.BlockSpec((tm, tn), lambda i, j: (i, j)),
        out_shape=jax.ShapeDtypeStruct((m, d), F32),
        compiler_params=_params(("arbitrary", "arbitrary")),
        name="even_out",
    )(ya, yb, w, w, x2)


def _odd_out_kernel(y_ref, w_ref, x_ref, o_ref):
    o_ref[...] = x_ref[...] + jnp.dot(y_ref[...], w_ref[...], preferred_element_type=F32)


def _odd_out(y, w, layer, x2, *, tm=512, tn=2048):
    m, d = x2.shape
    k = y.shape[1]
    return pl.pallas_call(
        _odd_out_kernel,
        grid=(m // tm, d // tn),
        in_specs=[pl.BlockSpec((tm, k), lambda i, j: (i, 0)),
                  _resident((None, k, tn), lambda i, j: (layer, 0, j), d // tn),
                  pl.BlockSpec((tm, tn), lambda i, j: (i, j))],
        out_specs=pl.BlockSpec((tm, tn), lambda i, j: (i, j)),
        out_shape=jax.ShapeDtypeStruct((m, d), F32),
        compiler_params=_params(("arbitrary", "arbitrary")),
        name="odd_out",
    )(y, w, x2)


def kernel(x, ev_ln_g, ev_w_in, ev_qk_g, ev_sinks, ev_w_out, od_ln_g, od_w_in, od_v_g, od_w_s,
           od_b_s, od_w_out, rel_bias):
    bsz, seq, d = x.shape
    n_even, n_odd = ev_ln_g.shape[0], od_ln_g.shape[0]
    flat = lambda w: w.reshape(-1, w.shape[-1])
    whole = lambda w: _Cast(flat(w), 0, w.shape[0] * w.shape[1])
    layer_of = lambda w, j: _Cast(flat(w), j * w.shape[1], w.shape[1])
    tabs, done = _bias_tables(rel_bias, [layer_of(ev_w_in, 0)])
    ev_in_b = {0: done[0][None]}
    od_in_b, ev_out_b, od_out_b = {}, None, None
    x2 = x.reshape(bsz * seq, d)
    for i in range(n_even + n_odd):
        j = i // 2
        if i % 2 == 0:
            z = _even_in(x2, ev_ln_g[j].reshape(1, d), ev_in_b[j], 0)
            z3 = z.reshape(bsz, seq, EVEN_IN)
            gains = jnp.tile(ev_qk_g[j].astype(F32), (1, 2))
            snk3 = jnp.repeat(ev_sinks[j].astype(F32), HEAD_DIM).reshape(A_KV_HEADS, -1, LANES)
            ya, done = _attn_a(z3, gains[0:1], gains[1:2], snk3, tabs, [whole(ev_w_out)] if j == 0 else [])
            if j == 0:
                ev_out_b = done[0].reshape(ev_w_out.shape)
            yb, done = _attn_b(z3, gains[2:3], gains[3:4], tabs, [layer_of(od_w_in, j)] if j < n_odd else [])
            if j < n_odd:
                od_in_b[j] = done[0][None]
            x2 = _even_out(ya.reshape(bsz * seq, A_WIDTH), yb.reshape(bsz * seq, B_WIDTH), ev_out_b, j, x2)
        else:
            casts = ([whole(od_w_out)] if j == 0 else []) + ([layer_of(ev_w_in, j + 1)] if j + 1 < n_even else [])
            y, done = _odd_in(x2, od_ln_g[j].reshape(1, d), od_in_b[j], 0, od_v_g[j].astype(F32),
                              od_w_s[j].astype(F32), od_b_s[j].astype(F32), casts)
            if j == 0:
                od_out_b = done.pop(0).reshape(od_w_out.shape)
            if j + 1 < n_even:
                ev_in_b[j + 1] = done.pop(0)[None]
            x2 = _odd_out(y, od_out_b, j, x2)
    return x2.reshape(bsz, seq, d)
```

```python
import functools
import math
from typing import NamedTuple

import numpy as np
import jax
import jax.numpy as jnp
from jax import lax
from jax.experimental import pallas as pl
from jax.experimental.pallas import tpu as pltpu

F32 = jnp.float32
BF16 = jnp.bfloat16

D_MODEL = 2048
HEAD_DIM = 64
A_HEADS = 16
A_KV_HEADS = 2
B_HEADS = 16
BLK = 128
NUM_BUCKETS = 32
REL_MAX_DISTANCE = 2048
A_WIDTH = A_HEADS * HEAD_DIM
B_WIDTH = B_HEADS * HEAD_DIM
EVEN_IN = 6400
C_WIDTH = 2 * D_MODEL
C_GROUPS = 16
C_GROUP_DIM = C_WIDTH // C_GROUPS
C_CHUNK = 128
EPS = 1e-6
NEG = -1e30
SCALE = HEAD_DIM ** -0.5
LOG2E = 1.4426950408889634
PATTERNS = ((1, 128), (4, 128), (16, 128), (1, 127))
N_B_PATTERNS = 3

LANES = 128
BF16_SUBLANES = 16
NORM_ROWS = 128
VMEM_LIMIT = 60 * 1024 * 1024

_QA, _KA, _VA, _GA = 0, 8, 9, 10
_QB, _KB, _VB, _GB = 18, 26, 34, 42


def _params(sem):
    return pltpu.CompilerParams(dimension_semantics=sem, vmem_limit_bytes=VMEM_LIMIT)


class _Cast(NamedTuple):
    src: jax.Array
    first_row: int
    n_rows: int


def _call_with_casts(body, *, grid, in_specs, out_specs, out_shape, scratch_shapes, name, args, casts=(),
                     cast_steps=None):
    n_in, n_cast = len(in_specs), len(casts)
    inner = grid[1] if cast_steps is None else cast_steps
    steps = grid[0] * inner
    tile_of = lambda i, j: i * inner + jnp.minimum(j, inner - 1)
    cast_in, cast_out, cast_shape = [], [], []
    for c in casts:
        cols = c.src.shape[1]
        tile = c.n_rows // steps
        first = c.first_row // tile
        assert tile * steps == c.n_rows and first * tile == c.first_row and tile % BF16_SUBLANES == 0
        cast_in.append(pl.BlockSpec((tile, cols), lambda i, j, first=first: (first + tile_of(i, j), 0)))
        cast_out.append(pl.BlockSpec((tile, cols), lambda i, j: (tile_of(i, j), 0)))
        cast_shape.append(jax.ShapeDtypeStruct((c.n_rows, cols), BF16))

    def kernel(*refs):
        ins, srcs = refs[:n_in], refs[n_in:n_in + n_cast]
        out, dsts = refs[n_in + n_cast], refs[n_in + n_cast + 1:n_in + 2 * n_cast + 1]
        for src, dst in zip(srcs, dsts):
            dst[...] = src[...].astype(dst.dtype)
        body(*ins, out, *refs[n_in + 2 * n_cast + 1:])

    res = pl.pallas_call(
        kernel,
        grid=grid,
        in_specs=list(in_specs) + cast_in,
        out_specs=[out_specs] + cast_out,
        out_shape=[out_shape] + cast_shape,
        scratch_shapes=scratch_shapes,
        compiler_params=_params(("arbitrary", "arbitrary")),
        name=name,
    )(*args, *[c.src for c in casts])
    return res[0], list(res[1:])


def _bucket_tables():
    a = np.arange(BLK)[:, None]
    b = np.arange(2 * BLK)[None, :]
    dist = BLK + a - b
    max_exact = NUM_BUCKETS // 2
    out = []
    for dil, max_dist in PATTERNS:
        n = np.maximum(dist * dil, 0)
        large = max_exact + (np.log(np.maximum(n, 1) / max_exact)
                             / np.log(REL_MAX_DISTANCE / max_exact)
                             * (NUM_BUCKETS - max_exact)).astype(np.int32)
        large = np.minimum(large, NUM_BUCKETS - 1)
        bucket = np.where(n < max_exact, n, large).astype(np.int32)
        valid = (dist >= 0) & (dist <= max_dist)
        out.append(np.where(valid, bucket, -1).astype(np.int32))
    return np.stack(out)


def _bias_kernel(tbl_ref, bucket_ref, out_ref):
    col0 = jnp.where(pl.program_id(0) < N_B_PATTERNS, A_HEADS, 0)
    bk = bucket_ref[...]
    prev_cols = lax.broadcasted_iota(jnp.int32, bk.shape, 1) < BLK

    def head(h, carry):
        acc = jnp.full(bk.shape, NEG, F32)
        for b in range(NUM_BUCKETS):
            acc = jnp.where(bk == b, tbl_ref[b, col0 + h] * LOG2E, acc)
        out_ref[h, 0] = acc
        out_ref[h, 1] = jnp.where(prev_cols, NEG, acc)
        return carry
    lax.fori_loop(0, out_ref.shape[0], head, 0)


def _bias_tables(rel_bias, casts=()):
    buckets = jnp.asarray(_bucket_tables())
    heads = max(A_HEADS, B_HEADS)
    return _call_with_casts(
        _bias_kernel,
        grid=(len(PATTERNS), 1),
        in_specs=[pl.BlockSpec(memory_space=pltpu.SMEM),
                  pl.BlockSpec((None, BLK, 2 * BLK), lambda t, _: (t, 0, 0))],
        out_specs=pl.BlockSpec((None, heads, 2, BLK, 2 * BLK), lambda t, _: (t, 0, 0, 0, 0)),
        out_shape=jax.ShapeDtypeStruct((len(PATTERNS), heads, 2, BLK, 2 * BLK), F32),
        scratch_shapes=[],
        name="bias_tables",
        args=(rel_bias.astype(F32), buckets),
        casts=casts)


def _norm_rows(x_ref, g_ref, h_scr, first, count):
    for c in range(first // NORM_ROWS, (first + count) // NORM_ROWS):
        r = pl.ds(c * NORM_ROWS, NORM_ROWS)
        x = x_ref[r, :]
        ms = jnp.mean(x * x, axis=-1, keepdims=True)
        h_scr[r, :] = (x * lax.rsqrt(ms + EPS) * g_ref[...]).astype(BF16)


def _even_in_kernel(x_ref, g_ref, w_ref, o_ref, h_scr, *, tm):
    j = pl.program_id(1)

    @pl.when(j == 0)
    def _():
        half = tm // 2
        for first in (0, half):
            _norm_rows(x_ref, g_ref, h_scr, first, half)
            rows = pl.ds(first, half)
            o_ref[rows, :] = jnp.dot(h_scr[rows, :], w_ref[...], preferred_element_type=F32).astype(o_ref.dtype)

    @pl.when(j > 0)
    def _():
        o_ref[...] = jnp.dot(h_scr[...], w_ref[...], preferred_element_type=F32).astype(o_ref.dtype)


def _even_in(x2, g, w, layer, *, tm=1024, tn=1280):
    m, d = x2.shape
    n = w.shape[2]
    return pl.pallas_call(
        functools.partial(_even_in_kernel, tm=tm),
        grid=(m // tm, n // tn),
        in_specs=[pl.BlockSpec((tm, d), lambda i, j: (i, 0)),
                  pl.BlockSpec((1, d), lambda i, j: (0, 0)),
                  pl.BlockSpec((None, d, tn), lambda i, j: (layer, 0, j))],
        out_specs=pl.BlockSpec((tm, tn), lambda i, j: (i, j)),
        out_shape=jax.ShapeDtypeStruct((m, n), BF16),
        scratch_shapes=[pltpu.VMEM((tm, d), BF16)],
        compiler_params=_params(("arbitrary", "arbitrary")),
        name="even_in",
    )(x2, g, w)


def _gelu(x):
    return 0.5 * x * (1.0 + lax.erf(x * (1.0 / math.sqrt(2.0))))


def _silu(x):
    return (0.5 * x) * (1.0 + jnp.tanh(0.5 * x))


def _odd_in_kernel(x_ref, g_ref, wv_ref, wu_ref, wg_ref, vg_ref, ws_ref, bs_ref, y_ref,
                   h_scr, v_scr, ssq_scr, *, tm, tn, v_tiles):
    j = pl.program_id(1)
    nbv = C_WIDTH // (tn * v_tiles)

    def v_tiles_of(first, count):
        rows = pl.ds(first, count)
        v = _gelu(jnp.dot(h_scr[rows, :], wv_ref[...], preferred_element_type=F32))
        for k in range(v_tiles):
            v_scr[j * v_tiles + k, rows, :] = v[:, k * tn:(k + 1) * tn].astype(v_scr.dtype)
        ssq_scr[rows, :] += jnp.sum(v * v, axis=-1, keepdims=True)

    @pl.when(j == 0)
    def _():
        ssq_scr[...] = jnp.zeros_like(ssq_scr)
        half = tm // 2
        for first in (0, half):
            _norm_rows(x_ref, g_ref, h_scr, first, half)
            v_tiles_of(first, half)

    @pl.when(jnp.logical_and(j > 0, j < nbv))
    def _():
        v_tiles_of(0, tm)

    @pl.when(j >= nbv)
    def _():
        jj = j - nbv
        h = h_scr[...]
        ug = (_gelu(jnp.dot(h, wu_ref[...], preferred_element_type=F32))
              * _silu(jnp.dot(h, wg_ref[...], preferred_element_type=F32)))
        tril = (lax.broadcasted_iota(jnp.int32, (C_CHUNK, C_CHUNK), 0)
                >= lax.broadcasted_iota(jnp.int32, (C_CHUNK, C_CHUNK), 1))
        groups_per_tile = tn // C_GROUP_DIM
        vg = vg_ref[jj]
        for gg in range(groups_per_tile):
            grp = jj * groups_per_tile + gg
            wt = jnp.where(tril, ws_ref[grp], 0.0).astype(BF16)
            cols = slice(gg * C_GROUP_DIM, (gg + 1) * C_GROUP_DIM)
            for c in range(tm // C_CHUNK):
                rows = slice(c * C_CHUNK, (c + 1) * C_CHUNK)
                inv = lax.rsqrt(ssq_scr[rows, :][:, :1] * (1.0 / C_WIDTH) + EPS)
                vn = (v_scr[jj, rows, cols].astype(F32) * inv * vg[:, cols]).astype(BF16)
                s = jnp.dot(wt, vn, preferred_element_type=F32) + bs_ref[grp]
                y_ref[rows, cols] = (ug[rows, cols] * s).astype(y_ref.dtype)


def _odd_in(x2, g, w, layer, vg, ws, bs, casts=(), *, tm=1024, tn=512, v_tiles=2):
    m, d = x2.shape
    nb = C_WIDTH // tn
    nbv = nb // v_tiles
    return _call_with_casts(
        functools.partial(_odd_in_kernel, tm=tm, tn=tn, v_tiles=v_tiles),
        grid=(m // tm, nbv + nb),
        in_specs=[pl.BlockSpec((tm, d), lambda i, j: (i, 0)),
                  pl.BlockSpec((1, d), lambda i, j: (0, 0)),
                  pl.BlockSpec((None, d, tn * v_tiles), lambda i, j: (layer, 0, nbv + jnp.minimum(j, nbv - 1))),
                  pl.BlockSpec((None, d, tn), lambda i, j: (layer, 0, jnp.maximum(j - nbv, 0))),
                  pl.BlockSpec((None, d, tn), lambda i, j: (layer, 0, 2 * nb + jnp.maximum(j - nbv, 0))),
                  pl.BlockSpec((nb, 1, tn), lambda i, j: (0, 0, 0)),
                  pl.BlockSpec((C_GROUPS, C_CHUNK, C_CHUNK), lambda i, j: (0, 0, 0)),
                  pl.BlockSpec((C_GROUPS, C_CHUNK, 1), lambda i, j: (0, 0, 0))],
        out_specs=pl.BlockSpec((tm, tn), lambda i, j: (i, jnp.maximum(j - nbv, 0))),
        out_shape=jax.ShapeDtypeStruct((m, C_WIDTH), BF16),
        scratch_shapes=[pltpu.VMEM((tm, d), BF16),
                        pltpu.VMEM((nb, tm, tn), BF16),
                        pltpu.VMEM((tm, LANES), F32)],
        name="odd_in",
        args=(x2, g, w, w, w, vg.reshape(nb, 1, tn), ws, bs.reshape(C_GROUPS, C_CHUNK, 1)),
        casts=casts,
        cast_steps=nb)


PAD = BLK
UNROLL = 4
NSLOT = 3


def _and(b, mask):
    return b & mask if isinstance(b, int) else lax.bitwise_and(b, mask)


def _shr(b, s):
    return b >> s if isinstance(b, int) else lax.shift_right_logical(b, s)


def _lane_lo(rows):
    return lax.broadcasted_iota(jnp.int32, (rows, LANES), 1) < HEAD_DIM


def _head_rms(x, gain):
    w = x.shape[1]
    r = lax.broadcasted_iota(jnp.int32, (w, w), 0) // HEAD_DIM
    c = lax.broadcasted_iota(jnp.int32, (w, w), 1) // HEAD_DIM
    mean_bd = jnp.where(r == c, 1.0 / HEAD_DIM, 0.0).astype(BF16)
    x2 = x * x
    hi = x2.astype(BF16)
    lo = (x2 - hi.astype(F32)).astype(BF16)
    ms = (jnp.dot(hi, mean_bd, preferred_element_type=F32)
          + jnp.dot(lo, mean_bd, preferred_element_type=F32))
    return x * lax.rsqrt(ms + EPS) * gain


def _attn_a_kernel(q0_ref, q1_ref, q2_ref, q3_ref, k_ref, v_ref, g0_ref, g1_ref, g2_ref, g3_ref,
                   gq_ref, gk_ref, snk_ref, bias_ref, o_ref, qb, kb, v0b, v1b, s_scr, mb_scr, p_scr, m_scr, *, seq):
    q_refs = (q0_ref, q1_ref, q2_ref, q3_ref)
    g_refs = (g0_ref, g1_ref, g2_ref, g3_ref)
    npair = len(q_refs)
    nblk = seq // BLK
    lane = lax.broadcasted_iota(jnp.int32, (BLK, LANES), 1)
    keep = (lane // HEAD_DIM) == pl.program_id(1)
    lo = lane < HEAD_DIM
    hi = jnp.logical_not(lo)
    zeros = jnp.zeros((PAD, LANES), BF16)
    kb[pl.ds(0, PAD), :] = zeros
    v0b[pl.ds(0, PAD), :] = zeros
    v1b[pl.ds(0, PAD), :] = zeros
    gq = gq_ref[...] * (SCALE * LOG2E)
    gq = jnp.concatenate([gq, gq], axis=1)
    snk = snk_ref[...] * LOG2E

    def prep(c, carry):
        r = pl.ds(c * BLK, BLK)
        ro = pl.ds(PAD + c * BLK, BLK)
        for pp in range(0, npair, 2):
            qq = jnp.concatenate([q_refs[pp][r, :], q_refs[pp + 1][r, :]], axis=1).astype(F32)
            qn = _head_rms(qq, gq).astype(BF16)
            qb[pp, r, :] = qn[:, :LANES]
            qb[pp + 1, r, :] = qn[:, LANES:]
        kn = _head_rms(k_ref[r, :].astype(F32), gk_ref[...])
        kb[ro, :] = jnp.where(keep, kn, pltpu.roll(kn, HEAD_DIM, 1)).astype(BF16)
        v = v_ref[r, :].astype(F32)
        vd = jnp.where(keep, v, pltpu.roll(v, HEAD_DIM, 1))
        v0b[ro, :] = jnp.where(lo, vd, 1.0).astype(BF16)
        v1b[ro, :] = jnp.where(lo, 1.0, vd).astype(BF16)
        return carry
    lax.fori_loop(0, nblk, prep, 0, unroll=4)

    def scores(n):
        kk = kb[pl.ds(n * BLK, 2 * BLK), :]
        for pp in range(npair):
            q = qb[pp, pl.ds(n * BLK, BLK), :]
            for hh, sel in enumerate((lo, hi)):
                s_scr[n % NSLOT, pp, hh] = lax.dot_general(
                    jnp.where(sel, q, jnp.zeros_like(q)), kk, (((1,), (1,)), ((), ())),
                    preferred_element_type=F32) + bias_ref[2 * pp + hh, int(n == 0)]

    def row_max(n):
        slot = n % NSLOT
        for pp in range(npair):
            for hh in range(2):
                m = jnp.max(s_scr[slot, pp, hh], axis=-1, keepdims=True)
                mb_scr[slot, pp, hh] = jnp.broadcast_to(m, (BLK, LANES))
            m_scr[slot, pp] = jnp.where(lo, mb_scr[slot, pp, 0], mb_scr[slot, pp, 1])

    def exponentials(n):
        slot = n % NSLOT
        for pp in range(npair):
            for hh in range(2):
                mb = mb_scr[slot, pp, hh]
                p_scr[slot, pp, hh] = jnp.exp2(s_scr[slot, pp, hh] - jnp.concatenate([mb, mb], axis=1)).astype(BF16)

    def weighted_values(n):
        slot = n % NSLOT
        r = pl.ds(n * BLK, BLK)
        rk = pl.ds(n * BLK, 2 * BLK)
        for pp in range(npair):
            ul0 = jnp.dot(p_scr[slot, pp, 0], v0b[rk, :], preferred_element_type=F32)
            ul1 = jnp.dot(p_scr[slot, pp, 1], v1b[rk, :], preferred_element_type=F32)
            u = jnp.where(lo, ul0, ul1)
            l = pltpu.roll(jnp.where(lo, ul1, ul0), HEAD_DIM, 1)
            o = u / (l + jnp.exp2(snk[pp:pp + 1, :] - m_scr[slot, pp]))
            o_ref[r, pl.ds(pp * LANES, LANES)] = (o * _silu(g_refs[pp][r, :].astype(F32))).astype(o_ref.dtype)

    for n in range(nblk + 3):
        if n - 3 >= 0:
            weighted_values(n - 3)
        if n < nblk:
            scores(n)
        if 0 <= n - 1 < nblk:
            row_max(n - 1)
        if 0 <= n - 2 < nblk:
            exponentials(n - 2)


def _attn_a(z3, gq2, gk2, snk3, tabs, casts=()):
    bsz, seq, _ = z3.shape
    npair = A_HEADS // A_KV_HEADS // 2

    def col(off, pp):
        return pl.BlockSpec((None, seq, LANES), lambda b, gi: (b, 0, off + npair * gi + pp))

    def fixed(off):
        return pl.BlockSpec((None, seq, LANES), lambda b, gi: (b, 0, off))

    return _call_with_casts(
        functools.partial(_attn_a_kernel, seq=seq),
        grid=(bsz, A_KV_HEADS),
        in_specs=[col(_QA, 0), col(_QA, 1), col(_QA, 2), col(_QA, 3), fixed(_KA), fixed(_VA),
                  col(_GA, 0), col(_GA, 1), col(_GA, 2), col(_GA, 3),
                  pl.BlockSpec((1, LANES), lambda b, gi: (0, 0)),
                  pl.BlockSpec((1, LANES), lambda b, gi: (0, 0)),
                  pl.BlockSpec((None, npair, LANES), lambda b, gi: (gi, 0, 0)),
                  pl.BlockSpec((None, 2 * npair, 2, BLK, 2 * BLK), lambda b, gi: (N_B_PATTERNS, gi, 0, 0, 0))],
        out_specs=pl.BlockSpec((None, seq, npair * LANES), lambda b, gi: (b, 0, gi)),
        out_shape=jax.ShapeDtypeStruct((bsz, seq, A_WIDTH), BF16),
        scratch_shapes=[pltpu.VMEM((npair, seq, LANES), BF16),
                        pltpu.VMEM((PAD + seq, LANES), BF16),
                        pltpu.VMEM((PAD + seq, LANES), BF16),
                        pltpu.VMEM((PAD + seq, LANES), BF16),
                        pltpu.VMEM((NSLOT, npair, 2, BLK, 2 * BLK), F32),
                        pltpu.VMEM((NSLOT, npair, 2, BLK, LANES), F32),
                        pltpu.VMEM((NSLOT, npair, 2, BLK, 2 * BLK), BF16),
                        pltpu.VMEM((NSLOT, npair, BLK, LANES), F32)],
        name="attn_a",
        args=(z3, z3, z3, z3, z3, z3, z3, z3, z3, z3, gq2, gk2, snk3, tabs),
        casts=casts)


def _attn_b_kernel(q_ref, k_ref, v_ref, g_ref, gq_ref, gk_ref, bias_ref, o_ref,
                   qf, kf, vf, q4f, k4f, v4f, qb, kb, v0b, v1b, s_scr, mb_scr, p_scr, m_scr, l_scr, u_scr, *, seq):
    lo = _lane_lo(BLK)
    hi = jnp.logical_not(lo)
    nblk = seq // BLK
    zeros = jnp.zeros((PAD, LANES), BF16)
    for pi in range(2):
        kb[pi, pl.ds(0, PAD), :] = zeros
        v0b[pi, pl.ds(0, PAD), :] = zeros
        v1b[pi, pl.ds(0, PAD), :] = zeros

    def put(pi, dst, q, k, v):
        qb[pi, dst, :] = q.astype(BF16)
        kb[pi, dst, :] = k.astype(BF16)
        v0b[pi, dst, :] = jnp.where(lo, v, 1.0).astype(BF16)
        v1b[pi, dst, :] = jnp.where(lo, 1.0, v).astype(BF16)

    def strided(t):
        return pl.ds(_and(t, 3) * (4 * BLK) + _shr(t, 2), BLK, stride=4)

    gains = jnp.concatenate([gq_ref[...] * (SCALE * LOG2E), gk_ref[...]], axis=1)

    def prep(c, carry):
        r = pl.ds(c * BLK, BLK)
        qk = jnp.concatenate([q_ref[r, :], k_ref[r, :]], axis=1).astype(F32)
        n = _head_rms(qk, gains)
        q, k, v = n[:, :LANES], n[:, LANES:], v_ref[r, :].astype(F32)
        qf[r, :] = q
        kf[r, :] = k
        vf[r, :] = v
        put(0, pl.ds(PAD + c * BLK, BLK), q, k, v)
        return carry
    for c in range(nblk):
        prep(c, 0)

    def deint4(t, carry):
        src = strided(t)
        dst = pl.ds(t * BLK, BLK)
        q, k, v = qf[src, :], kf[src, :], vf[src, :]
        q4f[dst, :] = q
        k4f[dst, :] = k
        v4f[dst, :] = v
        put(1, pl.ds(PAD + t * BLK, BLK), q, k, v)
        return carry
    for t in range(nblk):
        deint4(t, 0)

    def deint16(t, carry):
        src = strided(t)
        put(2, pl.ds(PAD + t * BLK, BLK), q4f[src, :], k4f[src, :], v4f[src, :])
        return carry
    for t in range(nblk):
        deint16(t, 0)

    groups = [(pi, g) for pi in range(N_B_PATTERNS) for g in range(nblk // UNROLL)]

    def geometry(pi):
        single = pi == 2
        return single, (BLK if single else 2 * BLK), (PAD if single else PAD - BLK)

    def out_rows(pi, b):
        return pl.ds(b * BLK, BLK) if pi == 0 else strided(b)

    def scores(t):
        pi, g = groups[t]
        single, keys, koff = geometry(pi)
        for uu in range(UNROLL):
            b = g * UNROLL + uu
            first = int(b % (nblk if pi == 0 else 4) == 0)
            q = qb[pi, pl.ds(PAD + b * BLK, BLK), :]
            kk = kb[pi, pl.ds(koff + b * BLK, keys), :]
            for hh, sel in enumerate((lo, hi)):
                bias = bias_ref[pi, hh, 0, :, pl.ds(BLK, BLK)] if single else bias_ref[pi, hh, first]
                s_scr[(t % NSLOT) * UNROLL + uu, hh, :, pl.ds(0, keys)] = lax.dot_general(
                    jnp.where(sel, q, jnp.zeros_like(q)), kk, (((1,), (1,)), ((), ())),
                    preferred_element_type=F32) + bias

    def row_max(t):
        pi, g = groups[t]
        _, keys, _ = geometry(pi)
        for uu in range(UNROLL):
            b = g * UNROLL + uu
            slot = (t % NSLOT) * UNROLL + uu
            for hh in range(2):
                m = jnp.max(s_scr[slot, hh, :, pl.ds(0, keys)], axis=-1, keepdims=True)
                mb_scr[slot, hh] = jnp.broadcast_to(m, (BLK, LANES))
            m_scr[pi, out_rows(pi, b), :] = jnp.where(lo, mb_scr[slot, 0], mb_scr[slot, 1])

    def exponentials(t):
        pi, g = groups[t]
        _, keys, _ = geometry(pi)
        for uu in range(UNROLL):
            slot = (t % NSLOT) * UNROLL + uu
            for hh in range(2):
                mb = mb_scr[slot, hh]
                mb = mb if keys == BLK else jnp.concatenate([mb, mb], axis=1)
                p_scr[slot, hh, :, pl.ds(0, keys)] = jnp.exp2(s_scr[slot, hh, :, pl.ds(0, keys)] - mb).astype(BF16)

    def weighted_values(t):
        pi, g = groups[t]
        _, keys, koff = geometry(pi)
        for uu in range(UNROLL):
            b = g * UNROLL + uu
            slot = (t % NSLOT) * UNROLL + uu
            rk = pl.ds(koff + b * BLK, keys)
            ul0 = jnp.dot(p_scr[slot, 0, :, pl.ds(0, keys)], v0b[pi, rk, :], preferred_element_type=F32)
            ul1 = jnp.dot(p_scr[slot, 1, :, pl.ds(0, keys)], v1b[pi, rk, :], preferred_element_type=F32)
            u_scr[pi, out_rows(pi, b), :] = jnp.where(lo, ul0, ul1)
            l_scr[pi, out_rows(pi, b), :] = pltpu.roll(jnp.where(lo, ul1, ul0), HEAD_DIM, 1)

    for t in range(len(groups) + 3):
        if t - 3 >= 0:
            weighted_values(t - 3)
        if t < len(groups):
            scores(t)
        if 0 <= t - 1 < len(groups):
            row_max(t - 1)
        if 0 <= t - 2 < len(groups):
            exponentials(t - 2)

    def renat(t, carry):
        for ref in (m_scr, l_scr, u_scr):
            ref[3, strided(t), :] = ref[2, pl.ds(t * BLK, BLK), :]
        return carry
    lax.fori_loop(0, nblk, renat, 0)

    def combine(c, carry):
        r = pl.ds(c * BLK, BLK)
        slots = (0, 1, 3)
        ms = [m_scr[s, r, :] for s in slots]
        mx = jnp.maximum(jnp.maximum(ms[0], ms[1]), ms[2])
        ws = [jnp.exp2(m - mx) for m in ms]
        add = lambda a, b: a + b
        num = functools.reduce(add, [w * u_scr[s, r, :] for s, w in zip(slots, ws)])
        den = functools.reduce(add, [w * l_scr[s, r, :] for s, w in zip(slots, ws)])
        o_ref[r, :] = ((num / den) * _silu(g_ref[r, :].astype(F32))).astype(o_ref.dtype)
        return carry
    lax.fori_loop(0, nblk, combine, 0, unroll=2)


def _attn_b(z3, gq2, gk2, tabs, casts=()):
    bsz, seq, _ = z3.shape
    col = lambda off: pl.BlockSpec((None, seq, LANES), lambda b, p: (b, 0, off + p))
    f32_rows = pltpu.VMEM((seq, LANES), F32)
    bf16_ops = pltpu.VMEM((N_B_PATTERNS, PAD + seq, LANES), BF16)
    stats = pltpu.VMEM((N_B_PATTERNS + 1, seq, LANES), F32)
    return _call_with_casts(
        functools.partial(_attn_b_kernel, seq=seq),
        grid=(bsz, B_HEADS // 2),
        in_specs=[col(_QB), col(_KB), col(_VB), col(_GB),
                  pl.BlockSpec((1, LANES), lambda b, p: (0, 0)),
                  pl.BlockSpec((1, LANES), lambda b, p: (0, 0)),
                  pl.BlockSpec((N_B_PATTERNS, 2, 2, BLK, 2 * BLK), lambda b, p: (0, p, 0, 0, 0))],
        out_specs=pl.BlockSpec((None, seq, LANES), lambda b, p: (b, 0, p)),
        out_shape=jax.ShapeDtypeStruct((bsz, seq, B_WIDTH), BF16),
        scratch_shapes=[f32_rows] * 6 + [bf16_ops] * 4
                       + [pltpu.VMEM((NSLOT * UNROLL, 2, BLK, 2 * BLK), F32),
                          pltpu.VMEM((NSLOT * UNROLL, 2, BLK, LANES), F32),
                          pltpu.VMEM((NSLOT * UNROLL, 2, BLK, 2 * BLK), BF16)] + [stats] * 3,
        name="attn_b",
        args=(z3, z3, z3, z3, gq2, gk2, tabs),
        casts=casts)


RING = 3


def _residual_ring(x_hbm, xbuf, sem, tm):
    i = pl.program_id(0)
    n = pl.num_programs(0)

    def copy(tile):
        slot = lax.rem(tile, RING)
        return pltpu.make_async_copy(x_hbm.at[pl.ds(tile * tm, tm), :], xbuf.at[slot], sem.at[slot])

    @pl.when(i == 0)
    def _():
        for tile in range(RING - 1):
            copy(tile).start()

    @pl.when(i + RING - 1 < n)
    def _():
        copy(i + RING - 1).start()

    copy(i).wait()
    return xbuf[lax.rem(i, RING)]


def _even_out_kernel(ya_ref, yb_ref, wa_ref, wb_ref, x_hbm, o_ref, xbuf, sem, *, tm):
    x = _residual_ring(x_hbm, xbuf, sem, tm)
    acc = jnp.dot(ya_ref[...], wa_ref[...], preferred_element_type=F32)
    acc = acc + jnp.dot(yb_ref[...], wb_ref[...], preferred_element_type=F32)
    o_ref[...] = x + acc


def _resident(block, index_map):
    return pl.BlockSpec(block, index_map, pipeline_mode=pl.Buffered(1))


def _ring_scratch(tm, d):
    return [pltpu.VMEM((RING, tm, d), F32), pltpu.SemaphoreType.DMA((RING,))]


def _even_out(ya, yb, w, layer, x2, *, tm=512):
    m, d = x2.shape
    ka, kb = ya.shape[1], yb.shape[1]
    assert ka == kb and m // tm >= RING
    return pl.pallas_call(
        functools.partial(_even_out_kernel, tm=tm),
        grid=(m // tm,),
        in_specs=[pl.BlockSpec((tm, ka), lambda i: (i, 0)),
                  pl.BlockSpec((tm, kb), lambda i: (i, 0)),
                  _resident((None, ka, d), lambda i: (layer, 0, 0)),
                  _resident((None, kb, d), lambda i: (layer, 1, 0)),
                  pl.BlockSpec(memory_space=pl.ANY)],
        out_specs=pl.BlockSpec((tm, d), lambda i: (i, 0)),
        out_shape=jax.ShapeDtypeStruct((m, d), F32),
        scratch_shapes=_ring_scratch(tm, d),
        compiler_params=_params(("arbitrary",)),
        name="even_out",
    )(ya, yb, w, w, x2)


def _odd_out_kernel(y_ref, w_ref, x_hbm, o_ref, xbuf, sem, *, tm):
    x = _residual_ring(x_hbm, xbuf, sem, tm)
    o_ref[...] = x + jnp.dot(y_ref[...], w_ref[...], preferred_element_type=F32)


def _odd_out(y, w, layer, x2, *, tm=512):
    m, d = x2.shape
    k = y.shape[1]
    assert m // tm >= RING
    return pl.pallas_call(
        functools.partial(_odd_out_kernel, tm=tm),
        grid=(m // tm,),
        in_specs=[pl.BlockSpec((tm, k), lambda i: (i, 0)),
                  _resident((None, k, d), lambda i: (layer, 0, 0)),
                  pl.BlockSpec(memory_space=pl.ANY)],
        out_specs=pl.BlockSpec((tm, d), lambda i: (i, 0)),
        out_shape=jax.ShapeDtypeStruct((m, d), F32),
        scratch_shapes=_ring_scratch(tm, d),
        compiler_params=_params(("arbitrary",)),
        name="odd_out",
    )(y, w, x2)


def kernel(x, ev_ln_g, ev_w_in, ev_qk_g, ev_sinks, ev_w_out, od_ln_g, od_w_in, od_v_g, od_w_s,
           od_b_s, od_w_out, rel_bias):
    bsz, seq, d = x.shape
    n_even, n_odd = ev_ln_g.shape[0], od_ln_g.shape[0]
    flat = lambda w: w.reshape(-1, w.shape[-1])
    whole = lambda w: _Cast(flat(w), 0, w.shape[0] * w.shape[1])
    layer_of = lambda w, j: _Cast(flat(w), j * w.shape[1], w.shape[1])
    tabs, done = _bias_tables(rel_bias, [layer_of(ev_w_in, 0)])
    ev_in_b = {0: done[0][None]}
    od_in_b, ev_out_b, od_out_b = {}, None, None
    x2 = x.reshape(bsz * seq, d)
    for i in range(n_even + n_odd):
        j = i // 2
        if i % 2 == 0:
            z = _even_in(x2, ev_ln_g[j].reshape(1, d), ev_in_b[j], 0)
            z3 = z.reshape(bsz, seq, EVEN_IN)
            gains = jnp.tile(ev_qk_g[j].astype(F32), (1, 2))
            snk3 = jnp.repeat(ev_sinks[j].astype(F32), HEAD_DIM).reshape(A_KV_HEADS, -1, LANES)
            ya, done = _attn_a(z3, gains[0:1], gains[1:2], snk3, tabs, [whole(ev_w_out)] if j == 0 else [])
            if j == 0:
                ev_out_b = done[0].reshape(ev_w_out.shape)
            yb, done = _attn_b(z3, gains[2:3], gains[3:4], tabs, [layer_of(od_w_in, j)] if j < n_odd else [])
            if j < n_odd:
                od_in_b[j] = done[0][None]
            x2 = _even_out(ya.reshape(bsz * seq, A_WIDTH), yb.reshape(bsz * seq, B_WIDTH), ev_out_b, j, x2)
        else:
            casts = ([whole(od_w_out)] if j == 0 else []) + ([layer_of(ev_w_in, j + 1)] if j + 1 < n_even else [])
            y, done = _odd_in(x2, od_ln_g[j].reshape(1, d), od_in_b[j], 0, od_v_g[j].astype(F32),
                              od_w_s[j].astype(F32), od_b_s[j].astype(F32), casts)
            if j == 0:
                od_out_b = done.pop(0).reshape(od_w_out.shape)
            if j + 1 < n_even:
                ev_in_b[j + 1] = done.pop(0)[None]
            x2 = _odd_out(y, od_out_b, j, x2)
    return x2.reshape(bsz, seq, d)
```

```python
import functools
import math
from typing import NamedTuple

import numpy as np
import jax
import jax.numpy as jnp
from jax import lax
from jax.experimental import pallas as pl
from jax.experimental.pallas import tpu as pltpu

F32 = jnp.float32
BF16 = jnp.bfloat16

D_MODEL = 2048
HEAD_DIM = 64
A_HEADS = 16
A_KV_HEADS = 2
B_HEADS = 16
BLK = 128
NUM_BUCKETS = 32
REL_MAX_DISTANCE = 2048
A_WIDTH = A_HEADS * HEAD_DIM
B_WIDTH = B_HEADS * HEAD_DIM
EVEN_IN = 6400
C_WIDTH = 2 * D_MODEL
C_GROUPS = 16
C_GROUP_DIM = C_WIDTH // C_GROUPS
C_CHUNK = 128
EPS = 1e-6
NEG = -1e30
SCALE = HEAD_DIM ** -0.5
LOG2E = 1.4426950408889634
PATTERNS = ((1, 128), (4, 128), (16, 128), (1, 127))
N_B_PATTERNS = 3

LANES = 128
BF16_SUBLANES = 16
NORM_ROWS = 128
VMEM_LIMIT = 60 * 1024 * 1024

_QA, _KA, _VA, _GA = 0, 8, 9, 10
_QB, _KB, _VB, _GB = 18, 26, 34, 42


def _params(sem):
    return pltpu.CompilerParams(dimension_semantics=sem, vmem_limit_bytes=VMEM_LIMIT)


class _Cast(NamedTuple):
    src: jax.Array
    first_row: int
    n_rows: int


def _call_with_casts(body, *, grid, in_specs, out_specs, out_shape, scratch_shapes, name, args, casts=(),
                     cast_steps=None):
    n_in, n_cast = len(in_specs), len(casts)
    inner = grid[1] if cast_steps is None else cast_steps
    steps = grid[0] * inner
    tile_of = lambda i, j: i * inner + jnp.minimum(j, inner - 1)
    cast_in, cast_out, cast_shape = [], [], []
    for c in casts:
        cols = c.src.shape[1]
        tile = c.n_rows // steps
        first = c.first_row // tile
        assert tile * steps == c.n_rows and first * tile == c.first_row and tile % BF16_SUBLANES == 0
        cast_in.append(pl.BlockSpec((tile, cols), lambda i, j, first=first: (first + tile_of(i, j), 0)))
        cast_out.append(pl.BlockSpec((tile, cols), lambda i, j: (tile_of(i, j), 0)))
        cast_shape.append(jax.ShapeDtypeStruct((c.n_rows, cols), BF16))

    def kernel(*refs):
        ins, srcs = refs[:n_in], refs[n_in:n_in + n_cast]
        out, dsts = refs[n_in + n_cast], refs[n_in + n_cast + 1:n_in + 2 * n_cast + 1]
        for src, dst in zip(srcs, dsts):
            dst[...] = src[...].astype(dst.dtype)
        body(*ins, out, *refs[n_in + 2 * n_cast + 1:])

    res = pl.pallas_call(
        kernel,
        grid=grid,
        in_specs=list(in_specs) + cast_in,
        out_specs=[out_specs] + cast_out,
        out_shape=[out_shape] + cast_shape,
        scratch_shapes=scratch_shapes,
        compiler_params=_params(("arbitrary", "arbitrary")),
        name=name,
    )(*args, *[c.src for c in casts])
    return res[0], list(res[1:])


def _bucket_tables():
    dist = 2 * BLK - np.arange(4 * BLK)[None, :]
    max_exact = NUM_BUCKETS // 2
    out = []
    for dil, max_dist in PATTERNS:
        n = np.maximum(dist * dil, 0)
        large = max_exact + (np.log(np.maximum(n, 1) / max_exact)
                             / np.log(REL_MAX_DISTANCE / max_exact)
                             * (NUM_BUCKETS - max_exact)).astype(np.int32)
        large = np.minimum(large, NUM_BUCKETS - 1)
        bucket = np.where(n < max_exact, n, large).astype(np.int32)
        valid = (dist >= 0) & (dist <= max_dist)
        out.append(np.tile(np.where(valid, bucket, -1).astype(np.int32), (8, 1)))
    return np.stack(out)


def _bias_kernel(tbl_ref, bucket_ref, out_ref):
    col0 = jnp.where(pl.program_id(0) < N_B_PATTERNS, A_HEADS, 0)
    bk = bucket_ref[...]
    prev_cols = lax.broadcasted_iota(jnp.int32, (8, 2 * BLK), 1) < BLK

    def head(h, carry):
        vec = jnp.full(bk.shape, NEG, F32)
        for b in range(NUM_BUCKETS):
            vec = jnp.where(bk == b, tbl_ref[b, col0 + h] * LOG2E, vec)
        for rg in range(BLK // 8):
            rows = pl.ds(rg * 8, 8)
            t = pltpu.roll(vec, rg * 8, 1, stride=1, stride_axis=0)[:, BLK:3 * BLK]
            out_ref[h, 0, rows, :] = t
            out_ref[h, 1, rows, :] = jnp.where(prev_cols, NEG, t)
        return carry
    lax.fori_loop(0, out_ref.shape[0], head, 0)


def _bias_tables(rel_bias, casts=()):
    buckets = jnp.asarray(_bucket_tables())
    heads = max(A_HEADS, B_HEADS)
    return _call_with_casts(
        _bias_kernel,
        grid=(len(PATTERNS), 1),
        in_specs=[pl.BlockSpec(memory_space=pltpu.SMEM),
                  pl.BlockSpec((None, 8, 4 * BLK), lambda t, _: (t, 0, 0))],
        out_specs=pl.BlockSpec((None, heads, 2, BLK, 2 * BLK), lambda t, _: (t, 0, 0, 0, 0)),
        out_shape=jax.ShapeDtypeStruct((len(PATTERNS), heads, 2, BLK, 2 * BLK), F32),
        scratch_shapes=[],
        name="bias_tables",
        args=(rel_bias.astype(F32), buckets),
        casts=casts)


def _norm_rows(x_ref, g_ref, h_scr, first, count):
    for c in range(first // NORM_ROWS, (first + count) // NORM_ROWS):
        r = pl.ds(c * NORM_ROWS, NORM_ROWS)
        x = x_ref[r, :]
        ms = jnp.mean(x * x, axis=-1, keepdims=True)
        h_scr[r, :] = (x * lax.rsqrt(ms + EPS) * g_ref[...]).astype(BF16)


def _even_in_kernel(x_ref, g_ref, w_ref, o_ref, h_scr, *, tm):
    j = pl.program_id(1)

    @pl.when(j == 0)
    def _():
        half = tm // 2
        for first in (0, half):
            _norm_rows(x_ref, g_ref, h_scr, first, half)
            rows = pl.ds(first, half)
            o_ref[rows, :] = jnp.dot(h_scr[rows, :], w_ref[...], preferred_element_type=F32).astype(o_ref.dtype)

    @pl.when(j > 0)
    def _():
        o_ref[...] = jnp.dot(h_scr[...], w_ref[...], preferred_element_type=F32).astype(o_ref.dtype)


def _even_in(x2, g, w, layer, *, tm=1024, tn=1280):
    m, d = x2.shape
    n = w.shape[2]
    return pl.pallas_call(
        functools.partial(_even_in_kernel, tm=tm),
        grid=(m // tm, n // tn),
        in_specs=[pl.BlockSpec((tm, d), lambda i, j: (i, 0)),
                  pl.BlockSpec((1, d), lambda i, j: (0, 0)),
                  pl.BlockSpec((None, d, tn), lambda i, j: (layer, 0, j))],
        out_specs=pl.BlockSpec((tm, tn), lambda i, j: (i, j)),
        out_shape=jax.ShapeDtypeStruct((m, n), BF16),
        scratch_shapes=[pltpu.VMEM((tm, d), BF16)],
        compiler_params=_params(("arbitrary", "arbitrary")),
        name="even_in",
    )(x2, g, w)


def _gelu(x):
    return 0.5 * x * (1.0 + lax.erf(x * (1.0 / math.sqrt(2.0))))


def _silu(x):
    return (0.5 * x) * (1.0 + jnp.tanh(0.5 * x))


def _odd_in_kernel(x_ref, g_ref, wv_ref, wu_ref, wg_ref, vg_ref, ws_ref, bs_ref, y_ref,
                   h_scr, v_scr, ssq_scr, *, tm, tn, v_tiles):
    j = pl.program_id(1)
    nbv = C_WIDTH // (tn * v_tiles)

    def v_tiles_of(first, count):
        rows = pl.ds(first, count)
        v = _gelu(jnp.dot(h_scr[rows, :], wv_ref[...], preferred_element_type=F32))
        for k in range(v_tiles):
            v_scr[j * v_tiles + k, rows, :] = v[:, k * tn:(k + 1) * tn].astype(v_scr.dtype)
        ssq_scr[rows, :] += jnp.sum(v * v, axis=-1, keepdims=True)

    @pl.when(j == 0)
    def _():
        ssq_scr[...] = jnp.zeros_like(ssq_scr)
        half = tm // 2
        for first in (0, half):
            _norm_rows(x_ref, g_ref, h_scr, first, half)
            v_tiles_of(first, half)

    @pl.when(jnp.logical_and(j > 0, j < nbv))
    def _():
        v_tiles_of(0, tm)

    @pl.when(j >= nbv)
    def _():
        jj = j - nbv
        h = h_scr[...]
        ug = (_gelu(jnp.dot(h, wu_ref[...], preferred_element_type=F32))
              * _silu(jnp.dot(h, wg_ref[...], preferred_element_type=F32)))
        tril = (lax.broadcasted_iota(jnp.int32, (C_CHUNK, C_CHUNK), 0)
                >= lax.broadcasted_iota(jnp.int32, (C_CHUNK, C_CHUNK), 1))
        groups_per_tile = tn // C_GROUP_DIM
        vg = vg_ref[jj]
        for gg in range(groups_per_tile):
            grp = jj * groups_per_tile + gg
            wt = jnp.where(tril, ws_ref[grp], 0.0).astype(BF16)
            cols = slice(gg * C_GROUP_DIM, (gg + 1) * C_GROUP_DIM)
            for c in range(tm // C_CHUNK):
                rows = slice(c * C_CHUNK, (c + 1) * C_CHUNK)
                inv = lax.rsqrt(ssq_scr[rows, :][:, :1] * (1.0 / C_WIDTH) + EPS)
                vn = (v_scr[jj, rows, cols].astype(F32) * inv * vg[:, cols]).astype(BF16)
                s = jnp.dot(wt, vn, preferred_element_type=F32) + bs_ref[grp]
                y_ref[rows, cols] = (ug[rows, cols] * s).astype(y_ref.dtype)


def _odd_in(x2, g, w, layer, vg, ws, bs, casts=(), *, tm=1024, tn=512, v_tiles=2):
    m, d = x2.shape
    nb = C_WIDTH // tn
    nbv = nb // v_tiles
    return _call_with_casts(
        functools.partial(_odd_in_kernel, tm=tm, tn=tn, v_tiles=v_tiles),
        grid=(m // tm, nbv + nb),
        in_specs=[pl.BlockSpec((tm, d), lambda i, j: (i, 0)),
                  pl.BlockSpec((1, d), lambda i, j: (0, 0)),
                  pl.BlockSpec((None, d, tn * v_tiles), lambda i, j: (layer, 0, nbv + jnp.minimum(j, nbv - 1))),
                  pl.BlockSpec((None, d, tn), lambda i, j: (layer, 0, jnp.maximum(j - nbv, 0))),
                  pl.BlockSpec((None, d, tn), lambda i, j: (layer, 0, 2 * nb + jnp.maximum(j - nbv, 0))),
                  pl.BlockSpec((nb, 1, tn), lambda i, j: (0, 0, 0)),
                  pl.BlockSpec((C_GROUPS, C_CHUNK, C_CHUNK), lambda i, j: (0, 0, 0)),
                  pl.BlockSpec((C_GROUPS, C_CHUNK, 1), lambda i, j: (0, 0, 0))],
        out_specs=pl.BlockSpec((tm, tn), lambda i, j: (i, jnp.maximum(j - nbv, 0))),
        out_shape=jax.ShapeDtypeStruct((m, C_WIDTH), BF16),
        scratch_shapes=[pltpu.VMEM((tm, d), BF16),
                        pltpu.VMEM((nb, tm, tn), BF16),
                        pltpu.VMEM((tm, LANES), F32)],
        name="odd_in",
        args=(x2, g, w, w, w, vg.reshape(nb, 1, tn), ws, bs.reshape(C_GROUPS, C_CHUNK, 1)),
        casts=casts,
        cast_steps=nb)


PAD = BLK
UNROLL = 4
NSLOT = 3


def _and(b, mask):
    return b & mask if isinstance(b, int) else lax.bitwise_and(b, mask)


def _shr(b, s):
    return b >> s if isinstance(b, int) else lax.shift_right_logical(b, s)


def _lane_lo(rows):
    return lax.broadcasted_iota(jnp.int32, (rows, LANES), 1) < HEAD_DIM


def _head_rms(x, gain):
    w = x.shape[1]
    r = lax.broadcasted_iota(jnp.int32, (w, w), 0) // HEAD_DIM
    c = lax.broadcasted_iota(jnp.int32, (w, w), 1) // HEAD_DIM
    mean_bd = jnp.where(r == c, 1.0 / HEAD_DIM, 0.0).astype(BF16)
    x2 = x * x
    hi = x2.astype(BF16)
    lo = (x2 - hi.astype(F32)).astype(BF16)
    ms = (jnp.dot(hi, mean_bd, preferred_element_type=F32)
          + jnp.dot(lo, mean_bd, preferred_element_type=F32))
    return x * lax.rsqrt(ms + EPS) * gain


def _attn_a_kernel(q0_ref, q1_ref, q2_ref, q3_ref, k_ref, v_ref, g0_ref, g1_ref, g2_ref, g3_ref,
                   gq_ref, gk_ref, snk_ref, bias_ref, o_ref, qb, kb, v0b, v1b, s_scr, mb_scr, p_scr, m_scr, *, seq):
    q_refs = (q0_ref, q1_ref, q2_ref, q3_ref)
    g_refs = (g0_ref, g1_ref, g2_ref, g3_ref)
    npair = len(q_refs)
    nblk = seq // BLK
    lane = lax.broadcasted_iota(jnp.int32, (BLK, LANES), 1)
    keep = (lane // HEAD_DIM) == pl.program_id(1)
    lo = lane < HEAD_DIM
    hi = jnp.logical_not(lo)
    zeros = jnp.zeros((PAD, LANES), BF16)
    kb[pl.ds(0, PAD), :] = zeros
    v0b[pl.ds(0, PAD), :] = zeros
    v1b[pl.ds(0, PAD), :] = zeros
    gq = gq_ref[...] * (SCALE * LOG2E)
    gq = jnp.concatenate([gq, gq], axis=1)
    snk = snk_ref[...] * LOG2E

    def prep(c, carry):
        r = pl.ds(c * BLK, BLK)
        ro = pl.ds(PAD + c * BLK, BLK)
        for pp in range(0, npair, 2):
            qq = jnp.concatenate([q_refs[pp][r, :], q_refs[pp + 1][r, :]], axis=1).astype(F32)
            qn = _head_rms(qq, gq).astype(BF16)
            qb[pp, r, :] = qn[:, :LANES]
            qb[pp + 1, r, :] = qn[:, LANES:]
        kn = _head_rms(k_ref[r, :].astype(F32), gk_ref[...])
        kb[ro, :] = jnp.where(keep, kn, pltpu.roll(kn, HEAD_DIM, 1)).astype(BF16)
        v = v_ref[r, :].astype(F32)
        vd = jnp.where(keep, v, pltpu.roll(v, HEAD_DIM, 1))
        v0b[ro, :] = jnp.where(lo, vd, 1.0).astype(BF16)
        v1b[ro, :] = jnp.where(lo, 1.0, vd).astype(BF16)
        return carry
    lax.fori_loop(0, nblk, prep, 0, unroll=4)

    def scores(n):
        kk = kb[pl.ds(n * BLK, 2 * BLK), :]
        for pp in range(npair):
            q = qb[pp, pl.ds(n * BLK, BLK), :]
            for hh, sel in enumerate((lo, hi)):
                s_scr[n % NSLOT, pp, hh] = lax.dot_general(
                    jnp.where(sel, q, jnp.zeros_like(q)), kk, (((1,), (1,)), ((), ())),
                    preferred_element_type=F32) + bias_ref[2 * pp + hh, int(n == 0)]

    def row_max(n):
        slot = n % NSLOT
        for pp in range(npair):
            for hh in range(2):
                m = jnp.max(s_scr[slot, pp, hh], axis=-1, keepdims=True)
                mb_scr[slot, pp, hh] = jnp.broadcast_to(m, (BLK, LANES))
            m_scr[slot, pp] = jnp.where(lo, mb_scr[slot, pp, 0], mb_scr[slot, pp, 1])

    def exponentials(n):
        slot = n % NSLOT
        for pp in range(npair):
            for hh in range(2):
                mb = mb_scr[slot, pp, hh]
                p_scr[slot, pp, hh] = jnp.exp2(s_scr[slot, pp, hh] - jnp.concatenate([mb, mb], axis=1)).astype(BF16)

    def weighted_values(n):
        slot = n % NSLOT
        r = pl.ds(n * BLK, BLK)
        rk = pl.ds(n * BLK, 2 * BLK)
        for pp in range(npair):
            ul0 = jnp.dot(p_scr[slot, pp, 0], v0b[rk, :], preferred_element_type=F32)
            ul1 = jnp.dot(p_scr[slot, pp, 1], v1b[rk, :], preferred_element_type=F32)
            u = jnp.where(lo, ul0, ul1)
            l = pltpu.roll(jnp.where(lo, ul1, ul0), HEAD_DIM, 1)
            o = u / (l + jnp.exp2(snk[pp:pp + 1, :] - m_scr[slot, pp]))
            o_ref[r, pl.ds(pp * LANES, LANES)] = (o * _silu(g_refs[pp][r, :].astype(F32))).astype(o_ref.dtype)

    for n in range(nblk + 3):
        if n - 3 >= 0:
            weighted_values(n - 3)
        if n < nblk:
            scores(n)
        if 0 <= n - 1 < nblk:
            row_max(n - 1)
        if 0 <= n - 2 < nblk:
            exponentials(n - 2)


def _attn_a(z3, gq2, gk2, snk3, tabs, casts=()):
    bsz, seq, _ = z3.shape
    npair = A_HEADS // A_KV_HEADS // 2

    def col(off, pp):
        return pl.BlockSpec((None, seq, LANES), lambda b, gi: (b, 0, off + npair * gi + pp))

    def fixed(off):
        return pl.BlockSpec((None, seq, LANES), lambda b, gi: (b, 0, off))

    return _call_with_casts(
        functools.partial(_attn_a_kernel, seq=seq),
        grid=(bsz, A_KV_HEADS),
        in_specs=[col(_QA, 0), col(_QA, 1), col(_QA, 2), col(_QA, 3), fixed(_KA), fixed(_VA),
                  col(_GA, 0), col(_GA, 1), col(_GA, 2), col(_GA, 3),
                  pl.BlockSpec((1, LANES), lambda b, gi: (0, 0)),
                  pl.BlockSpec((1, LANES), lambda b, gi: (0, 0)),
                  pl.BlockSpec((None, npair, LANES), lambda b, gi: (gi, 0, 0)),
                  pl.BlockSpec((None, 2 * npair, 2, BLK, 2 * BLK), lambda b, gi: (N_B_PATTERNS, gi, 0, 0, 0))],
        out_specs=pl.BlockSpec((None, seq, npair * LANES), lambda b, gi: (b, 0, gi)),
        out_shape=jax.ShapeDtypeStruct((bsz, seq, A_WIDTH), BF16),
        scratch_shapes=[pltpu.VMEM((npair, seq, LANES), BF16),
                        pltpu.VMEM((PAD + seq, LANES), BF16),
                        pltpu.VMEM((PAD + seq, LANES), BF16),
                        pltpu.VMEM((PAD + seq, LANES), BF16),
                        pltpu.VMEM((NSLOT, npair, 2, BLK, 2 * BLK), F32),
                        pltpu.VMEM((NSLOT, npair, 2, BLK, LANES), F32),
                        pltpu.VMEM((NSLOT, npair, 2, BLK, 2 * BLK), BF16),
                        pltpu.VMEM((NSLOT, npair, BLK, LANES), F32)],
        name="attn_a",
        args=(z3, z3, z3, z3, z3, z3, z3, z3, z3, z3, gq2, gk2, snk3, tabs),
        casts=casts)


def _attn_b_kernel(q_ref, k_ref, v_ref, g_ref, gq_ref, gk_ref, bias_ref, o_ref,
                   qf, kf, vf, q4f, k4f, v4f, qb, kb, v0b, v1b, s_scr, mb_scr, p_scr, m_scr, l_scr, u_scr, *, seq):
    lo = _lane_lo(BLK)
    hi = jnp.logical_not(lo)
    nblk = seq // BLK
    zeros = jnp.zeros((PAD, LANES), BF16)
    for pi in range(2):
        kb[pi, pl.ds(0, PAD), :] = zeros
        v0b[pi, pl.ds(0, PAD), :] = zeros
        v1b[pi, pl.ds(0, PAD), :] = zeros

    def put(pi, dst, q, k, v):
        qb[pi, dst, :] = q.astype(BF16)
        kb[pi, dst, :] = k.astype(BF16)
        v0b[pi, dst, :] = jnp.where(lo, v, 1.0).astype(BF16)
        v1b[pi, dst, :] = jnp.where(lo, 1.0, v).astype(BF16)

    def strided(t):
        return pl.ds(_and(t, 3) * (4 * BLK) + _shr(t, 2), BLK, stride=4)

    gains = jnp.concatenate([gq_ref[...] * (SCALE * LOG2E), gk_ref[...]], axis=1)

    def prep(c, carry):
        r = pl.ds(c * BLK, BLK)
        qk = jnp.concatenate([q_ref[r, :], k_ref[r, :]], axis=1).astype(F32)
        n = _head_rms(qk, gains)
        q, k, v = n[:, :LANES], n[:, LANES:], v_ref[r, :].astype(F32)
        qf[r, :] = q
        kf[r, :] = k
        vf[r, :] = v
        put(0, pl.ds(PAD + c * BLK, BLK), q, k, v)
        return carry
    for c in range(nblk):
        prep(c, 0)

    def deint4(t, carry):
        src = strided(t)
        dst = pl.ds(t * BLK, BLK)
        q, k, v = qf[src, :], kf[src, :], vf[src, :]
        q4f[dst, :] = q
        k4f[dst, :] = k
        v4f[dst, :] = v
        put(1, pl.ds(PAD + t * BLK, BLK), q, k, v)
        return carry
    for t in range(nblk):
        deint4(t, 0)

    def deint16(t, carry):
        src = strided(t)
        put(2, pl.ds(PAD + t * BLK, BLK), q4f[src, :], k4f[src, :], v4f[src, :])
        return carry
    for t in range(nblk):
        deint16(t, 0)

    groups = [(pi, g) for pi in range(N_B_PATTERNS) for g in range(nblk // UNROLL)]

    def geometry(pi):
        single = pi == 2
        return single, (BLK if single else 2 * BLK), (PAD if single else PAD - BLK)

    def out_rows(pi, b):
        return pl.ds(b * BLK, BLK) if pi == 0 else strided(b)

    def scores(t):
        pi, g = groups[t]
        single, keys, koff = geometry(pi)
        for uu in range(UNROLL):
            b = g * UNROLL + uu
            first = int(b % (nblk if pi == 0 else 4) == 0)
            q = qb[pi, pl.ds(PAD + b * BLK, BLK), :]
            kk = kb[pi, pl.ds(koff + b * BLK, keys), :]
            for hh, sel in enumerate((lo, hi)):
                bias = bias_ref[pi, hh, 0, :, pl.ds(BLK, BLK)] if single else bias_ref[pi, hh, first]
                s_scr[(t % NSLOT) * UNROLL + uu, hh, :, pl.ds(0, keys)] = lax.dot_general(
                    jnp.where(sel, q, jnp.zeros_like(q)), kk, (((1,), (1,)), ((), ())),
                    preferred_element_type=F32) + bias

    def row_max(t):
        pi, g = groups[t]
        _, keys, _ = geometry(pi)
        for uu in range(UNROLL):
            b = g * UNROLL + uu
            slot = (t % NSLOT) * UNROLL + uu
            for hh in range(2):
                m = jnp.max(s_scr[slot, hh, :, pl.ds(0, keys)], axis=-1, keepdims=True)
                mb_scr[slot, hh] = jnp.broadcast_to(m, (BLK, LANES))
            m_scr[pi, out_rows(pi, b), :] = jnp.where(lo, mb_scr[slot, 0], mb_scr[slot, 1])

    def exponentials(t):
        pi, g = groups[t]
        _, keys, _ = geometry(pi)
        for uu in range(UNROLL):
            slot = (t % NSLOT) * UNROLL + uu
            for hh in range(2):
                mb = mb_scr[slot, hh]
                mb = mb if keys == BLK else jnp.concatenate([mb, mb], axis=1)
                p_scr[slot, hh, :, pl.ds(0, keys)] = jnp.exp2(s_scr[slot, hh, :, pl.ds(0, keys)] - mb).astype(BF16)

    def weighted_values(t):
        pi, g = groups[t]
        _, keys, koff = geometry(pi)
        for uu in range(UNROLL):
            b = g * UNROLL + uu
            slot = (t % NSLOT) * UNROLL + uu
            rk = pl.ds(koff + b * BLK, keys)
            ul0 = jnp.dot(p_scr[slot, 0, :, pl.ds(0, keys)], v0b[pi, rk, :], preferred_element_type=F32)
            ul1 = jnp.dot(p_scr[slot, 1, :, pl.ds(0, keys)], v1b[pi, rk, :], preferred_element_type=F32)
            u_scr[pi, out_rows(pi, b), :] = jnp.where(lo, ul0, ul1)
            l_scr[pi, out_rows(pi, b), :] = pltpu.roll(jnp.where(lo, ul1, ul0), HEAD_DIM, 1)

    for t in range(len(groups) + 3):
        if t - 3 >= 0:
            weighted_values(t - 3)
        if t < len(groups):
            scores(t)
        if 0 <= t - 1 < len(groups):
            row_max(t - 1)
        if 0 <= t - 2 < len(groups):
            exponentials(t - 2)

    def renat(t, carry):
        for ref in (m_scr, l_scr, u_scr):
            ref[3, strided(t), :] = ref[2, pl.ds(t * BLK, BLK), :]
        return carry
    lax.fori_loop(0, nblk, renat, 0)

    def combine(c, carry):
        r = pl.ds(c * BLK, BLK)
        slots = (0, 1, 3)
        ms = [m_scr[s, r, :] for s in slots]
        mx = jnp.maximum(jnp.maximum(ms[0], ms[1]), ms[2])
        ws = [jnp.exp2(m - mx) for m in ms]
        add = lambda a, b: a + b
        num = functools.reduce(add, [w * u_scr[s, r, :] for s, w in zip(slots, ws)])
        den = functools.reduce(add, [w * l_scr[s, r, :] for s, w in zip(slots, ws)])
        o_ref[r, :] = ((num / den) * _silu(g_ref[r, :].astype(F32))).astype(o_ref.dtype)
        return carry
    lax.fori_loop(0, nblk, combine, 0, unroll=2)


def _attn_b(z3, gq2, gk2, tabs, casts=()):
    bsz, seq, _ = z3.shape
    col = lambda off: pl.BlockSpec((None, seq, LANES), lambda b, p: (b, 0, off + p))
    f32_rows = pltpu.VMEM((seq, LANES), F32)
    bf16_ops = pltpu.VMEM((N_B_PATTERNS, PAD + seq, LANES), BF16)
    stats = pltpu.VMEM((N_B_PATTERNS + 1, seq, LANES), F32)
    return _call_with_casts(
        functools.partial(_attn_b_kernel, seq=seq),
        grid=(bsz, B_HEADS // 2),
        in_specs=[col(_QB), col(_KB), col(_VB), col(_GB),
                  pl.BlockSpec((1, LANES), lambda b, p: (0, 0)),
                  pl.BlockSpec((1, LANES), lambda b, p: (0, 0)),
                  pl.BlockSpec((N_B_PATTERNS, 2, 2, BLK, 2 * BLK), lambda b, p: (0, p, 0, 0, 0))],
        out_specs=pl.BlockSpec((None, seq, LANES), lambda b, p: (b, 0, p)),
        out_shape=jax.ShapeDtypeStruct((bsz, seq, B_WIDTH), BF16),
        scratch_shapes=[f32_rows] * 6 + [bf16_ops] * 4
                       + [pltpu.VMEM((NSLOT * UNROLL, 2, BLK, 2 * BLK), F32),
                          pltpu.VMEM((NSLOT * UNROLL, 2, BLK, LANES), F32),
                          pltpu.VMEM((NSLOT * UNROLL, 2, BLK, 2 * BLK), BF16)] + [stats] * 3,
        name="attn_b",
        args=(z3, z3, z3, z3, gq2, gk2, tabs),
        casts=casts)


def _even_out_kernel(ya_ref, yb_ref, wa_ref, wb_ref, x_ref, o_ref):
    acc = jnp.dot(ya_ref[...], wa_ref[...], preferred_element_type=F32)
    acc = acc + jnp.dot(yb_ref[...], wb_ref[...], preferred_element_type=F32)
    o_ref[...] = x_ref[...] + acc


def _resident(block, index_map, n_col_tiles):
    if n_col_tiles == 1:
        return pl.BlockSpec(block, index_map, pipeline_mode=pl.Buffered(1))
    return pl.BlockSpec(block, index_map)


def _even_out(ya, yb, w, layer, x2, *, tm=512, tn=2048):
    m, d = x2.shape
    ka, kb = ya.shape[1], yb.shape[1]
    assert ka == kb
    return pl.pallas_call(
        _even_out_kernel,
        grid=(m // tm, d // tn),
        in_specs=[pl.BlockSpec((tm, ka), lambda i, j: (i, 0)),
                  pl.BlockSpec((tm, kb), lambda i, j: (i, 0)),
                  _resident((None, ka, tn), lambda i, j: (layer, 0, j), d // tn),
                  _resident((None, kb, tn), lambda i, j: (layer, 1, j), d // tn),
                  pl.BlockSpec((tm, tn), lambda i, j: (i, j))],
        out_specs=pl.BlockSpec((tm, tn), lambda i, j: (i, j)),
        out_shape=jax.ShapeDtypeStruct((m, d), F32),
        compiler_params=_params(("arbitrary", "arbitrary")),
        name="even_out",
    )(ya, yb, w, w, x2)


def _odd_out_kernel(y_ref, w_ref, x_ref, o_ref):
    o_ref[...] = x_ref[...] + jnp.dot(y_ref[...], w_ref[...], preferred_element_type=F32)


def _odd_out(y, w, layer, x2, *, tm=512, tn=2048):
    m, d = x2.shape
    k = y.shape[1]
    return pl.pallas_call(
        _odd_out_kernel,
        grid=(m // tm, d // tn),
        in_specs=[pl.BlockSpec((tm, k), lambda i, j: (i, 0)),
                  _resident((None, k, tn), lambda i, j: (layer, 0, j), d // tn),
                  pl.BlockSpec((tm, tn), lambda i, j: (i, j))],
        out_specs=pl.BlockSpec((tm, tn), lambda i, j: (i, j)),
        out_shape=jax.ShapeDtypeStruct((m, d), F32),
        compiler_params=_params(("arbitrary", "arbitrary")),
        name="odd_out",
    )(y, w, x2)


def kernel(x, ev_ln_g, ev_w_in, ev_qk_g, ev_sinks, ev_w_out, od_ln_g, od_w_in, od_v_g, od_w_s,
           od_b_s, od_w_out, rel_bias):
    bsz, seq, d = x.shape
    n_even, n_odd = ev_ln_g.shape[0], od_ln_g.shape[0]
    flat = lambda w: w.reshape(-1, w.shape[-1])
    whole = lambda w: _Cast(flat(w), 0, w.shape[0] * w.shape[1])
    layer_of = lambda w, j: _Cast(flat(w), j * w.shape[1], w.shape[1])
    tabs, done = _bias_tables(rel_bias, [layer_of(ev_w_in, 0)])
    ev_in_b = {0: done[0][None]}
    od_in_b, ev_out_b, od_out_b = {}, None, None
    x2 = x.reshape(bsz * seq, d)
    for i in range(n_even + n_odd):
        j = i // 2
        if i % 2 == 0:
            z = _even_in(x2, ev_ln_g[j].reshape(1, d), ev_in_b[j], 0)
            z3 = z.reshape(bsz, seq, EVEN_IN)
            gains = jnp.tile(ev_qk_g[j].astype(F32), (1, 2))
            snk3 = jnp.repeat(ev_sinks[j].astype(F32), HEAD_DIM).reshape(A_KV_HEADS, -1, LANES)
            ya, done = _attn_a(z3, gains[0:1], gains[1:2], snk3, tabs, [whole(ev_w_out)] if j == 0 else [])
            if j == 0:
                ev_out_b = done[0].reshape(ev_w_out.shape)
            yb, done = _attn_b(z3, gains[2:3], gains[3:4], tabs, [layer_of(od_w_in, j)] if j < n_odd else [])
            if j < n_odd:
                od_in_b[j] = done[0][None]
            x2 = _even_out(ya.reshape(bsz * seq, A_WIDTH), yb.reshape(bsz * seq, B_WIDTH), ev_out_b, j, x2)
        else:
            casts = ([whole(od_w_out)] if j == 0 else []) + ([layer_of(ev_w_in, j + 1)] if j + 1 < n_even else [])
            y, done = _odd_in(x2, od_ln_g[j].reshape(1, d), od_in_b[j], 0, od_v_g[j].astype(F32),
                              od_w_s[j].astype(F32), od_b_s[j].astype(F32), casts)
            if j == 0:
                od_out_b = done.pop(0).reshape(od_w_out.shape)
            if j + 1 < n_even:
                ev_in_b[j + 1] = done.pop(0)[None]
            x2 = _odd_out(y, od_out_b, j, x2)
    return x2.reshape(bsz, seq, d)
```
